```python
import math
import jax, jax.numpy as jnp
from jax import lax
import numpy as np

D_MODEL = 4096
BATCH = 4
SEQ = 4096
DEPTH = 1

CTX_LEN = 256
GRID_W = 64
N_MOD = 9
D_FF = 11008
SSM_WIDTH = D_MODEL // 4
SSM_GROUP = 16
SSM_GROUPS = SSM_WIDTH // SSM_GROUP
SSM_STATE = 64
POOL_WINDOWS = (2, 4, 8, 16)
POOL_WIDTH = D_MODEL // 2
POOL_GROUP = POOL_WIDTH // len(POOL_WINDOWS)
D_IN = SSM_WIDTH + POOL_WIDTH + 2 * D_MODEL
RMS_EPS = 1e-6
LAMBDA_RE_MAX = -1e-4
STEP_MIN = 1e-3
STEP_MAX = 1e-1
HALF = 0.5

kernel_name = 'hybrid_s5_pool_macaron_prefix_dit'


def _rmsnorm(v, g):
    vf = v.astype(jnp.float32)
    vf = vf * lax.rsqrt(jnp.mean(vf * vf, axis=-1, keepdims=True) + RMS_EPS)
    return (vf * g.astype(jnp.float32)).astype(v.dtype)


def _pre(v, g, shift, scale):
    return _rmsnorm(v, g) * (1 + scale) + shift


def _swiglu(u, w_in, w_out):
    gate, up = jnp.split(u @ w_in, 2, axis=-1)
    return (jax.nn.silu(gate) * up) @ w_out


def _discretise(lam_re, lam_im, log_step, b_re, b_im):
    lam = lax.complex(jnp.minimum(lam_re.astype(jnp.float32), LAMBDA_RE_MAX), lam_im.astype(jnp.float32))
    step = jnp.exp(log_step.astype(jnp.float32))[..., None]
    lam_bar = jnp.exp(lam * step)
    b = lax.complex(b_re.astype(jnp.float32), b_im.astype(jnp.float32))
    b_bar = ((lam_bar - 1) / lam)[..., None] * b
    return lam_bar, b_bar


def _diag_scan(bu, lam_bar, reverse):
    a = jnp.broadcast_to(lam_bar, (bu.shape[0], 1) + lam_bar.shape)

    def combine(e1, e2):
        a1, b1 = e1
        a2, b2 = e2
        return a1 * a2, a2 * b1 + b2

    _, h = lax.associative_scan(combine, (a, bu), reverse=reverse, axis=0)
    return h


def _ssm_states(u_s, lam_bar, b_bar, h0):
    bsz, n, _ = u_s.shape
    ug = u_s.astype(jnp.float32).reshape(bsz, n, SSM_GROUPS, SSM_GROUP).astype(jnp.complex64)
    bu_f = jnp.einsum('blgk,gpk->lbgp', ug, b_bar[0])
    bu_b = jnp.einsum('blgk,gpk->lbgp', ug, b_bar[1])
    if h0 is not None:
        h0_f, h0_b = h0
        bu_f = bu_f.at[0].add(lam_bar[0] * h0_f)
        bu_b = bu_b.at[-1].add(lam_bar[1] * h0_b)
    h_f = _diag_scan(bu_f, lam_bar[0], reverse=False)
    h_b = _diag_scan(bu_b, lam_bar[1], reverse=True)
    return h_f, h_b


def _ssm_branch(u_s, h_f, h_b, c_re, c_im, d, w_glu, w_branch):
    bsz, n, _ = u_s.shape
    cm = lax.complex(c_re.astype(jnp.float32), c_im.astype(jnp.float32))
    y = jnp.real(jnp.einsum('lbgp,gkp->blgk', h_f, cm[0]) + jnp.einsum('lbgp,gkp->blgk', h_b, cm[1]))
    y = y.reshape(bsz, n, SSM_WIDTH).astype(u_s.dtype) + d * u_s
    z = jax.nn.gelu(y)
    a, b = jnp.split(z @ w_glu, 2, axis=-1)
    return (a * jax.nn.sigmoid(b)) @ w_branch


def _box_sum(v, w, axis):
    n = v.shape[axis]
    pad = [(0, 0)] * v.ndim
    pad[axis] = (1, 0)
    s = jnp.pad(jnp.cumsum(v, axis=axis), pad)
    idx = jnp.arange(n)
    lo = jnp.clip(idx - w // 2, 0, n)
    hi = jnp.clip(idx + (w - w // 2), 0, n)
    total = jnp.take(s, hi, axis=axis) - jnp.take(s, lo, axis=axis)
    return total, (hi - lo).astype(jnp.float32)


def _window_mean(v, w, on_grid):
    bsz, n, ch = v.shape
    vf = v.astype(jnp.float32)
    if on_grid:
        rows = n // GRID_W
        g = vf.reshape(bsz, rows, GRID_W, ch)
        s, cnt_r = _box_sum(g, w, 1)
        s, cnt_c = _box_sum(s, w, 2)
        mean = (s / (cnt_r[:, None] * cnt_c[None, :])[None, :, :, None]).reshape(bsz, n, ch)
    else:
        s, cnt = _box_sum(vf, w, 1)
        mean = s / cnt[None, :, None]
    return mean.astype(v.dtype)


def _pool_branch(u_p, on_grid, pool_w, pool_scale, w_branch):
    bsz, n, _ = u_p.shape
    groups = jnp.split(u_p, len(POOL_WINDOWS), axis=-1)
    mixed = jnp.stack([_window_mean(v, w, on_grid) - v for v, w in zip(groups, POOL_WINDOWS)], axis=2)
    y = jnp.einsum('blgc,gcd->blgd', mixed, pool_w).reshape(bsz, n, POOL_WIDTH)
    return (y * pool_scale) @ w_branch


def _token_mixer(u, lp, lam_bar, b_bar, h0, on_grid):
    splits = [SSM_WIDTH, SSM_WIDTH + POOL_WIDTH, SSM_WIDTH + POOL_WIDTH + D_MODEL]
    u_s, u_p, g_a, g_b = jnp.split(u @ lp['w_in'], splits, axis=-1)
    h_f, h_b = _ssm_states(u_s, lam_bar, b_bar, h0)
    y_a = _ssm_branch(u_s, h_f, h_b, lp['ssm_c_re'], lp['ssm_c_im'], lp['ssm_d'], lp['w_glu'], lp['w_branch_a'])
    y_b = _pool_branch(u_p, on_grid, lp['pool_w'], lp['pool_scale'], lp['w_branch_b'])
    merged = jax.nn.sigmoid(g_a) * y_a + jax.nn.sigmoid(g_b) * y_b
    return merged @ lp['w_out'], (h_f[-1], h_b[0])


def setup_inputs(seed: int = 0) -> dict:
    key = jax.random.key(seed)
    ks = jax.random.split(key, 32)
    f32 = jnp.float32

    def nrm(k, shape, scale):
        return jax.random.normal(k, shape, f32) * scale

    L, G, P, K = DEPTH, SSM_GROUPS, SSM_STATE, SSM_GROUP
    x = nrm(ks[0], (BATCH, SEQ, D_MODEL), 1.0)
    c = nrm(ks[1], (BATCH, D_MODEL), 1.0)
    ctx = nrm(ks[2], (BATCH, CTX_LEN, D_MODEL), 1.0)
    c_ctx = nrm(ks[3], (D_MODEL,), 1.0)
    w_mod = nrm(ks[4], (L, D_MODEL, N_MOD * D_MODEL), 0.5 * D_MODEL ** -0.5)
    b_mod = nrm(ks[5], (L, N_MOD * D_MODEL), 0.02)
    norm_g = 1.0 + nrm(ks[6], (L, 3, D_MODEL), 0.02)
    final_g = 1.0 + nrm(ks[7], (D_MODEL,), 0.02)
    ffn1_w_in = nrm(ks[8], (L, D_MODEL, 2 * D_FF), D_MODEL ** -0.5)
    ffn1_w_out = nrm(ks[9], (L, D_FF, D_MODEL), D_FF ** -0.5)
    ffn2_w_in = nrm(ks[10], (L, D_MODEL, 2 * D_FF), D_MODEL ** -0.5)
    ffn2_w_out = nrm(ks[11], (L, D_FF, D_MODEL), D_FF ** -0.5)
    w_in = nrm(ks[12], (L, D_MODEL, D_IN), D_MODEL ** -0.5)
    n_idx = jnp.arange(P, dtype=f32)
    ssm_lambda_re = -0.5 + nrm(ks[13], (L, 2, G, P), 0.01)
    ssm_lambda_im = math.pi * n_idx + nrm(ks[14], (L, 2, G, P), 0.01)
    ssm_log_step = jax.random.uniform(ks[15], (L, 2, G), f32, math.log(STEP_MIN), math.log(STEP_MAX))
    ssm_b_re = nrm(ks[16], (L, 2, G, P, K), (2 * K) ** -0.5)
    ssm_b_im = nrm(ks[17], (L, 2, G, P, K), (2 * K) ** -0.5)
    ssm_c_re = nrm(ks[18], (L, 2, G, K, P), (2 * P) ** -0.5)
    ssm_c_im = nrm(ks[19], (L, 2, G, K, P), (2 * P) ** -0.5)
    ssm_d = nrm(ks[20], (L, SSM_WIDTH), 1.0)
    w_glu = nrm(ks[21], (L, SSM_WIDTH, 2 * SSM_WIDTH), SSM_WIDTH ** -0.5)
    w_branch_a = nrm(ks[22], (L, SSM_WIDTH, D_MODEL), SSM_WIDTH ** -0.5)
    pool_w = nrm(ks[23], (L, len(POOL_WINDOWS), POOL_GROUP, POOL_GROUP), POOL_GROUP ** -0.5)
    pool_scale = 1.0 + nrm(ks[24], (L, POOL_WIDTH), 0.02)
    w_branch_b = nrm(ks[25], (L, POOL_WIDTH, D_MODEL), POOL_WIDTH ** -0.5)
    w_out = nrm(ks[26], (L, D_MODEL, D_MODEL), D_MODEL ** -0.5)
    return {'x': x, 'c': c, 'ctx': ctx, 'c_ctx': c_ctx, 'w_mod': w_mod, 'b_mod': b_mod,
            'norm_g': norm_g, 'final_g': final_g,
            'ffn1_w_in': ffn1_w_in, 'ffn1_w_out': ffn1_w_out, 'ffn2_w_in': ffn2_w_in, 'ffn2_w_out': ffn2_w_out,
            'w_in': w_in, 'ssm_lambda_re': ssm_lambda_re, 'ssm_lambda_im': ssm_lambda_im,
            'ssm_log_step': ssm_log_step, 'ssm_b_re': ssm_b_re, 'ssm_b_im': ssm_b_im,
            'ssm_c_re': ssm_c_re, 'ssm_c_im': ssm_c_im, 'ssm_d': ssm_d, 'w_glu': w_glu,
            'w_branch_a': w_branch_a, 'pool_w': pool_w, 'pool_scale': pool_scale,
            'w_branch_b': w_branch_b, 'w_out': w_out}


def reference(x, c, ctx, c_ctx, w_mod, b_mod, norm_g, final_g, ffn1_w_in, ffn1_w_out, ffn2_w_in, ffn2_w_out,
              w_in, ssm_lambda_re, ssm_lambda_im, ssm_log_step, ssm_b_re, ssm_b_im, ssm_c_re, ssm_c_im, ssm_d,
              w_glu, w_branch_a, pool_w, pool_scale, w_branch_b, w_out):
    bsz = x.shape[0]
    for l in range(DEPTH):
        last = l == DEPTH - 1
        mx = (jax.nn.silu(c) @ w_mod[l] + b_mod[l]).reshape(bsz, N_MOD, 1, D_MODEL)
        mx = [mx[:, i] for i in range(N_MOD)]
        mc = (jax.nn.silu(c_ctx) @ w_mod[l] + b_mod[l]).reshape(N_MOD, D_MODEL)
        lp = {'w_in': w_in[l], 'ssm_c_re': ssm_c_re[l], 'ssm_c_im': ssm_c_im[l], 'ssm_d': ssm_d[l],
              'w_glu': w_glu[l], 'w_branch_a': w_branch_a[l], 'pool_w': pool_w[l],
              'pool_scale': pool_scale[l], 'w_branch_b': w_branch_b[l], 'w_out': w_out[l]}
        lam_bar, b_bar = _discretise(ssm_lambda_re[l], ssm_lambda_im[l], ssm_log_step[l], ssm_b_re[l], ssm_b_im[l])

        x = x + HALF * mx[2] * _swiglu(_pre(x, norm_g[l, 0], mx[0], mx[1]), ffn1_w_in[l], ffn1_w_out[l])
        ctx = ctx + HALF * mc[2] * _swiglu(_pre(ctx, norm_g[l, 0], mc[0], mc[1]), ffn1_w_in[l], ffn1_w_out[l])

        uc = _pre(ctx, norm_g[l, 1], mc[3], mc[4])
        if last:
            h_f, h_b = _ssm_states(uc @ w_in[l][:, :SSM_WIDTH], lam_bar, b_bar, None)
            h_ctx = (h_f[-1], h_b[0])
        else:
            m_c, h_ctx = _token_mixer(uc, lp, lam_bar, b_bar, None, False)
            ctx = ctx + mc[5] * m_c
            ctx = ctx + HALF * mc[8] * _swiglu(_pre(ctx, norm_g[l, 2], mc[6], mc[7]), ffn2_w_in[l], ffn2_w_out[l])

        m_x, _ = _token_mixer(_pre(x, norm_g[l, 1], mx[3], mx[4]), lp, lam_bar, b_bar, h_ctx, True)
        x = x + mx[5] * m_x

        x = x + HALF * mx[8] * _swiglu(_pre(x, norm_g[l, 2], mx[6], mx[7]), ffn2_w_in[l], ffn2_w_out[l])
    return _rmsnorm(x, final_g)
```

```python
import functools
import math

import jax
import jax.numpy as jnp
import numpy as np
from jax import lax
from jax.experimental import pallas as pl
from jax.experimental.pallas import tpu as pltpu

BF16 = jnp.bfloat16
F32 = jnp.float32

RMS_EPS = 1e-6
LAMBDA_RE_MAX = -1e-4
HALF = 0.5
N_MOD = 9
SSM_GROUP = 16
POOL_WINDOWS = (2, 4, 8, 16)
GRID_W = 64
SSM_CHUNK = 16

V7X_VMEM_BYTES = 64 * 1024 * 1024
VMEM_LIMIT = 60 * 1024 * 1024


def _cparams(sem):
    return pltpu.CompilerParams(dimension_semantics=sem, vmem_limit_bytes=VMEM_LIMIT)


def _tile(n, pref):
    t = min(n, pref)
    while n % t:
        t //= 2
    return t


ROW_CHUNK = 32


def _for_row_chunks(n, fn):
    def body(i, carry):
        fn(pl.ds(pl.multiple_of(i * ROW_CHUNK, ROW_CHUNK), ROW_CHUNK))
        return carry

    lax.fori_loop(0, n // ROW_CHUNK, body, 0)


def _dot(a, b):
    return jnp.dot(a, b, preferred_element_type=F32)


def _rms_mod(x, g, shift, scale):
    xn = x * lax.rsqrt(jnp.mean(x * x, axis=-1, keepdims=True) + RMS_EPS)
    return (xn * g) * (1.0 + scale) + shift


def _mod_kernel(c_ref, w_ref, b_ref, o_ref):
    c = c_ref[...]
    s = (c * jax.nn.sigmoid(c)).astype(BF16)
    o_ref[...] = _dot(s, w_ref[...].astype(BF16)) + b_ref[...]


def _mod(c8, w_mod, b_mod, tn):
    d, n = w_mod.shape
    return pl.pallas_call(
        _mod_kernel,
        out_shape=jax.ShapeDtypeStruct((c8.shape[0], n), F32),
        grid=(n // tn,),
        in_specs=[pl.BlockSpec(c8.shape, lambda j: (0, 0)),
                  pl.BlockSpec((d, tn), lambda j: (0, j)),
                  pl.BlockSpec((1, tn), lambda j: (0, j))],
        out_specs=pl.BlockSpec((c8.shape[0], tn), lambda j: (0, j)),
        compiler_params=_cparams(("arbitrary",)),
        name="mod",
    )(c8, w_mod, b_mod)


def _ffn_kernel(x_ref, g_ref, mod_ref, wg_ref, wu_ref, wo_ref, fg_ref, o_ref, pre_ref,
                *, gi, mi, nf, final_norm):
    f = pl.program_id(2)
    tm = pre_ref.shape[0]

    @pl.when(f == 0)
    def _():
        def rows(r):
            pre = _rms_mod(x_ref[0, r, :], g_ref[pl.ds(gi, 1), :], mod_ref[0, pl.ds(mi, 1), :],
                           mod_ref[0, pl.ds(mi + 1, 1), :])
            pre_ref[r, :] = pre.astype(BF16)
            o_ref[0, r, :] = jnp.zeros((ROW_CHUNK, o_ref.shape[2]), F32)

        _for_row_chunks(tm, rows)

    p = pre_ref[...]
    gate = _dot(p, wg_ref[...])
    up = _dot(p, wu_ref[...])
    act = (gate * jax.nn.sigmoid(gate) * up).astype(BF16)
    o_ref[0] += _dot(act, wo_ref[...])

    @pl.when(f == nf - 1)
    def _():
        def rows(r):
            y = x_ref[0, r, :] + (HALF * mod_ref[0, pl.ds(mi + 2, 1), :]) * o_ref[0, r, :]
            if final_norm:
                y = y * lax.rsqrt(jnp.mean(y * y, axis=-1, keepdims=True) + RMS_EPS) * fg_ref[...]
            o_ref[0, r, :] = y

        _for_row_chunks(tm, rows)


def _ffn(x, norm_g, mods, mod_row, w_in, w_out, final_g, *, gi, mi, tm, tf, final_norm):
    bsz, s, d = x.shape
    ff = w_out.shape[0]
    nf = ff // tf
    tm = _tile(s, tm)
    kern = functools.partial(_ffn_kernel, gi=gi, mi=mi, nf=nf, final_norm=final_norm)
    return pl.pallas_call(
        kern,
        out_shape=jax.ShapeDtypeStruct((bsz, s, d), F32),
        grid=(bsz, s // tm, nf),
        in_specs=[pl.BlockSpec((1, tm, d), lambda b, i, f: (b, i, 0)),
                  pl.BlockSpec(norm_g.shape, lambda b, i, f: (0, 0)),
                  pl.BlockSpec((1, N_MOD, d), lambda b, i, f: (mod_row(b), 0, 0)),
                  pl.BlockSpec((d, tf), lambda b, i, f: (0, f)),
                  pl.BlockSpec((d, tf), lambda b, i, f: (0, nf + f)),
                  pl.BlockSpec((tf, d), lambda b, i, f: (f, 0)),
                  pl.BlockSpec((1, d), lambda b, i, f: (0, 0))],
        out_specs=pl.BlockSpec((1, tm, d), lambda b, i, f: (b, i, 0)),
        scratch_shapes=[pltpu.VMEM((tm, d), BF16)],
        compiler_params=_cparams(("arbitrary", "arbitrary", "arbitrary")),
        name="ffn",
    )(x, norm_g, mods, w_in, w_in, w_out, final_g)


def _prenorm_kernel(x_ref, g_ref, mod_ref, o_ref, *, gi, mi):
    def rows(r):
        pre = _rms_mod(x_ref[0, r, :], g_ref[pl.ds(gi, 1), :], mod_ref[0, pl.ds(mi, 1), :],
                       mod_ref[0, pl.ds(mi + 1, 1), :])
        o_ref[0, r, :] = pre.astype(BF16)

    _for_row_chunks(x_ref.shape[1], rows)


def _prenorm(x, norm_g, mods, mod_row, *, gi, mi, tm):
    bsz, s, d = x.shape
    tm = _tile(s, tm)
    return pl.pallas_call(
        functools.partial(_prenorm_kernel, gi=gi, mi=mi),
        out_shape=jax.ShapeDtypeStruct((bsz, s, d), BF16),
        grid=(bsz, s // tm),
        in_specs=[pl.BlockSpec((1, tm, d), lambda b, i: (b, i, 0)),
                  pl.BlockSpec(norm_g.shape, lambda b, i: (0, 0)),
                  pl.BlockSpec((1, N_MOD, d), lambda b, i: (mod_row(b), 0, 0))],
        out_specs=pl.BlockSpec((1, tm, d), lambda b, i: (b, i, 0)),
        compiler_params=_cparams(("arbitrary", "arbitrary")),
        name="prenorm",
    )(x, norm_g, mods)


def _mm_kernel(a_ref, b_ref, o_ref):
    o_ref[0] = _dot(a_ref[0], b_ref[...]).astype(o_ref.dtype)


def _mm(a, w, out_dtype, *, tm, tn):
    bsz, s, k = a.shape
    n = w.shape[1]
    tm, tn = _tile(s, tm), _tile(n, tn)
    return pl.pallas_call(
        _mm_kernel,
        out_shape=jax.ShapeDtypeStruct((bsz, s, n), out_dtype),
        grid=(bsz, s // tm, n // tn),
        in_specs=[pl.BlockSpec((1, tm, k), lambda b, i, j: (b, i, 0)),
                  pl.BlockSpec((k, tn), lambda b, i, j: (0, j))],
        out_specs=pl.BlockSpec((1, tm, tn), lambda b, i, j: (b, i, j)),
        compiler_params=_cparams(("arbitrary", "arbitrary", "arbitrary")),
        name="mm",
    )(a, w)


def _outproj_kernel(a_ref, w_ref, x_ref, mod_ref, o_ref, *, mi):
    o_ref[0] = x_ref[0] + mod_ref[0, pl.ds(mi, 1), :] * _dot(a_ref[0], w_ref[...])


def _outproj(a, w, x, mods, mod_row, *, mi, tm, tn):
    bsz, s, k = a.shape
    n = w.shape[1]
    tm, tn = _tile(s, tm), _tile(n, tn)
    return pl.pallas_call(
        functools.partial(_outproj_kernel, mi=mi),
        out_shape=jax.ShapeDtypeStruct((bsz, s, n), F32),
        grid=(bsz, s // tm, n // tn),
        in_specs=[pl.BlockSpec((1, tm, k), lambda b, i, j: (b, i, 0)),
                  pl.BlockSpec((k, tn), lambda b, i, j: (0, j)),
                  pl.BlockSpec((1, tm, tn), lambda b, i, j: (b, i, j)),
                  pl.BlockSpec((1, N_MOD, tn), lambda b, i, j: (mod_row(b), 0, j))],
        out_specs=pl.BlockSpec((1, tm, tn), lambda b, i, j: (b, i, j)),
        compiler_params=_cparams(("arbitrary", "arbitrary", "arbitrary")),
        name="outproj",
    )(a, w, x, mods)


def _ssm_tables(lam_re, lam_im, log_step, b_re, b_im, c_re, c_im):
    t = SSM_CHUNK
    hp = lax.Precision.HIGHEST
    lr = jnp.minimum(lam_re.astype(F32), LAMBDA_RE_MAX)
    li = lam_im.astype(F32)
    step = jnp.exp(log_step.astype(F32))[..., None]
    m = jnp.arange(t + 1, dtype=F32)[:, None, None, None]
    mag = jnp.exp(m * (lr * step)[None])
    ang = m * (li * step)[None]
    pw_re, pw_im = mag * jnp.cos(ang), mag * jnp.sin(ang)
    nr, ni = pw_re[1] - 1.0, pw_im[1]
    den = lr * lr + li * li
    q_re, q_im = (nr * lr + ni * li) / den, (ni * lr - nr * li) / den
    bb_re = q_re[..., None] * b_re - q_im[..., None] * b_im
    bb_im = q_re[..., None] * b_im + q_im[..., None] * b_re
    cr, ci = c_re.astype(F32), c_im.astype(F32)

    cl_re = cr[:, :, None] * pw_re[:t].transpose(1, 2, 0, 3)[:, :, :, None, :] \
        - ci[:, :, None] * pw_im[:t].transpose(1, 2, 0, 3)[:, :, :, None, :]
    cl_im = cr[:, :, None] * pw_im[:t].transpose(1, 2, 0, 3)[:, :, :, None, :] \
        + ci[:, :, None] * pw_re[:t].transpose(1, 2, 0, 3)[:, :, :, None, :]
    kern = jnp.einsum('dgtkp,dgpq->dgtkq', cl_re, bb_re, precision=hp) \
        - jnp.einsum('dgtkp,dgpq->dgtkq', cl_im, bb_im, precision=hp)
    i_idx = jnp.arange(t)[:, None]
    j_idx = jnp.arange(t)[None, :]
    kf = kern[0][:, jnp.clip(j_idx - i_idx, 0, t - 1)] * (j_idx >= i_idx)[None, :, :, None, None]
    kb = kern[1][:, jnp.clip(i_idx - j_idx, 0, t - 1)] * (i_idx >= j_idx)[None, :, :, None, None]
    g, kk = kern.shape[1], kern.shape[3]
    toep = (kf + kb).transpose(0, 1, 4, 2, 3).reshape(g, t * kk, t * kk)

    pf_re, pf_im = pw_re[:t][::-1, 0], pw_im[:t][::-1, 0]
    pb_re, pb_im = pw_re[:t, 1], pw_im[:t, 1]

    def st(p_re, p_im, d):
        re = p_re[:, :, :, None] * bb_re[d][None] - p_im[:, :, :, None] * bb_im[d][None]
        im = p_re[:, :, :, None] * bb_im[d][None] + p_im[:, :, :, None] * bb_re[d][None]
        return re.transpose(1, 0, 3, 2), im.transpose(1, 0, 3, 2)

    wf_re, wf_im = st(pf_re, pf_im, 0)
    wb_re, wb_im = st(pb_re, pb_im, 1)
    p = lr.shape[-1]
    wst = jnp.concatenate([wf_re, wb_re, wf_im, wb_im], axis=-1).reshape(g, t * kk, 4 * p)

    vf_pw_re, vf_pw_im = pw_re[1:, 0], pw_im[1:, 0]
    vb_pw_re, vb_pw_im = pw_re[1:, 1][::-1], pw_im[1:, 1][::-1]

    def rd(p_re, p_im, d):
        d_re = cr[d][None] * p_re[:, :, None, :] - ci[d][None] * p_im[:, :, None, :]
        d_im = cr[d][None] * p_im[:, :, None, :] + ci[d][None] * p_re[:, :, None, :]
        return d_re.transpose(1, 3, 0, 2), -d_im.transpose(1, 3, 0, 2)

    vf_re, vf_im = rd(vf_pw_re, vf_pw_im, 0)
    vb_re, vb_im = rd(vb_pw_re, vb_pw_im, 1)
    vrd = jnp.concatenate([vf_re, vb_re, vf_im, vb_im], axis=1).reshape(g, 4 * p, t * kk)

    a_re = jnp.concatenate([pw_re[t, 0], pw_re[t, 1]], axis=-1)[:, None, :]
    a_im = jnp.concatenate([pw_im[t, 0], pw_im[t, 1]], axis=-1)[:, None, :]
    return toep.astype(BF16), wst.astype(BF16), vrd.astype(BF16), a_re, a_im


def _ssm_in_kernel(u_ref, w_ref, *o_refs, gb, n_intra):
    for g in range(gb):
        r = _dot(u_ref[g], w_ref[g])
        if n_intra:
            o_refs[0][g] = r[:, :n_intra]
            o_refs[1][g] = r[:, n_intra:]
        else:
            o_refs[0][g] = r


def _ssm_in(u, w, *, gb, n_intra):
    g, r, tk = u.shape
    n = w.shape[2]
    shapes, specs = [], []
    if n_intra:
        shapes.append(jax.ShapeDtypeStruct((g, r, n_intra), F32))
        specs.append(pl.BlockSpec((gb, r, n_intra), lambda i: (i, 0, 0)))
    shapes.append(jax.ShapeDtypeStruct((g, r, n - n_intra), F32))
    specs.append(pl.BlockSpec((gb, r, n - n_intra), lambda i: (i, 0, 0)))
    return pl.pallas_call(
        functools.partial(_ssm_in_kernel, gb=gb, n_intra=n_intra),
        out_shape=shapes,
        grid=(g // gb,),
        in_specs=[pl.BlockSpec((gb, r, tk), lambda i: (i, 0, 0)),
                  pl.BlockSpec((gb, tk, n), lambda i: (i, 0, 0))],
        out_specs=specs,
        compiler_params=_cparams(("arbitrary",)),
        name="ssm_in",
    )(u, w)


def _ssm_rec_kernel(s_ref, ar_ref, ai_ref, h0_ref, *refs, nc, p2, want_prev):
    if want_prev:
        hp_ref, hfin_ref, tmp_ref = refs
    else:
        hfin_ref, = refs
    ar = ar_ref[...]
    ai = ai_ref[...]
    shape = (s_ref.shape[0], s_ref.shape[2], p2)
    fwd = lax.broadcasted_iota(jnp.int32, shape, 2) < (p2 // 2)

    def body(c, carry):
        h_re, h_im = carry
        cb = nc - 1 - c
        if want_prev:
            hp_ref[:, c, :, :] = jnp.concatenate([h_re, h_im], axis=-1)
            tmp_ref[:, cb, :, :] = jnp.concatenate([h_re, h_im], axis=-1)
        s_f = s_ref[:, c, :, :]
        s_b = s_ref[:, cb, :, :]
        s_re = jnp.where(fwd, s_f[:, :, :p2], s_b[:, :, :p2])
        s_im = jnp.where(fwd, s_f[:, :, p2:], s_b[:, :, p2:])
        n_re = ar * h_re - ai * h_im + s_re
        n_im = ar * h_im + ai * h_re + s_im
        return n_re, n_im

    h0 = h0_ref[...]
    h_re, h_im = lax.fori_loop(0, nc, body, (h0[:, :, :p2], h0[:, :, p2:]))
    hfin_ref[...] = jnp.concatenate([h_re, h_im], axis=-1)
    if want_prev:
        fwd4 = lax.broadcasted_iota(jnp.int32, hp_ref.shape, 3) % p2 < (p2 // 2)
        hp_ref[...] = jnp.where(fwd4, hp_ref[...], tmp_ref[...])


def _ssm_rec(s, a_re, a_im, h0, *, gb, want_prev):
    g, nc, bsz, p4 = s.shape
    p2 = p4 // 2
    shapes = [jax.ShapeDtypeStruct((g, bsz, p4), F32)]
    specs = [pl.BlockSpec((gb, bsz, p4), lambda i: (i, 0, 0))]
    scratch = []
    if want_prev:
        shapes.insert(0, jax.ShapeDtypeStruct(s.shape, F32))
        specs.insert(0, pl.BlockSpec((gb, nc, bsz, p4), lambda i: (i, 0, 0, 0)))
        scratch.append(pltpu.VMEM((gb, nc, bsz, p4), F32))
    return pl.pallas_call(
        functools.partial(_ssm_rec_kernel, nc=nc, p2=p2, want_prev=want_prev),
        out_shape=shapes,
        grid=(g // gb,),
        in_specs=[pl.BlockSpec((gb, nc, bsz, p4), lambda i: (i, 0, 0, 0)),
                  pl.BlockSpec((gb, 1, p2), lambda i: (i, 0, 0)),
                  pl.BlockSpec((gb, 1, p2), lambda i: (i, 0, 0)),
                  pl.BlockSpec((gb, bsz, p4), lambda i: (i, 0, 0))],
        out_specs=specs,
        scratch_shapes=scratch,
        compiler_params=_cparams(("arbitrary",)),
        name="ssm_rec",
    )(s, a_re, a_im, h0)


def _ssm_out_kernel(hp_ref, v_ref, yi_ref, o_ref, *, gb):
    for g in range(gb):
        o_ref[g] = yi_ref[g] + _dot(hp_ref[g].astype(BF16), v_ref[g])


def _ssm_out(hp, v, yi, *, gb):
    g, r, p4 = hp.shape
    tk = v.shape[2]
    return pl.pallas_call(
        functools.partial(_ssm_out_kernel, gb=gb),
        out_shape=jax.ShapeDtypeStruct((g, r, tk), F32),
        grid=(g // gb,),
        in_specs=[pl.BlockSpec((gb, r, p4), lambda i: (i, 0, 0)),
                  pl.BlockSpec((gb, p4, tk), lambda i: (i, 0, 0)),
                  pl.BlockSpec((gb, r, tk), lambda i: (i, 0, 0))],
        out_specs=pl.BlockSpec((gb, r, tk), lambda i: (i, 0, 0)),
        compiler_params=_cparams(("arbitrary",)),
        name="ssm_out",
    )(hp, v, yi)


def _to_chunks(u_s, groups):
    bsz, n, w = u_s.shape
    k = w // groups
    nc = n // SSM_CHUNK
    u = u_s.astype(BF16).reshape(bsz, nc, SSM_CHUNK, groups, k).transpose(3, 1, 0, 2, 4)
    return u.reshape(groups, nc * bsz, SSM_CHUNK * k)


def _from_chunks(y, bsz):
    g, r, tk = y.shape
    k = tk // SSM_CHUNK
    nc = r // bsz
    y = y.reshape(g, nc, bsz, SSM_CHUNK, k).transpose(2, 1, 3, 0, 4)
    return y.reshape(bsz, nc * SSM_CHUNK, g * k)


def _glu_kernel(y_ref, u_ref, d_ref, w_ref, o_ref, *, ws):
    z = jax.nn.gelu(y_ref[0] + d_ref[...] * u_ref[0]).astype(BF16)
    a = _dot(z, w_ref[:, :ws])
    b = _dot(z, w_ref[:, ws:])
    o_ref[0] = (a * jax.nn.sigmoid(b)).astype(BF16)


def _glu(y, usp, d, w_glu, *, tm):
    bsz, s, ws = y.shape
    tm = _tile(s, tm)
    return pl.pallas_call(
        functools.partial(_glu_kernel, ws=ws),
        out_shape=jax.ShapeDtypeStruct((bsz, s, ws), BF16),
        grid=(bsz, s // tm),
        in_specs=[pl.BlockSpec((1, tm, ws), lambda b, i: (b, i, 0)),
                  pl.BlockSpec((1, tm, ws), lambda b, i: (b, i, 0)),
                  pl.BlockSpec((1, ws), lambda b, i: (0, 0)),
                  pl.BlockSpec(w_glu.shape, lambda b, i: (0, 0))],
        out_specs=pl.BlockSpec((1, tm, ws), lambda b, i: (b, i, 0)),
        compiler_params=_cparams(("arbitrary", "arbitrary")),
        name="glu",
    )(y, usp, d, w_glu)


POOL_TILE = 4 * GRID_W


def _pool_consts(n):
    t = np.arange(POOL_TILE)
    pcs, invs = [], []
    r = np.arange(n) // GRID_W
    c = np.arange(n) % GRID_W
    rows = n // GRID_W
    for w in POOL_WINDOWS:
        lo, hi = w // 2, w - w // 2
        same_row = (t[:, None] // GRID_W) == (t[None, :] // GRID_W)
        dc = (t[None, :] % GRID_W) - (t[:, None] % GRID_W)
        pcs.append((same_row & (dc >= -lo) & (dc < hi)).astype(np.float32))
        cnt_r = np.minimum(r + hi, rows) - np.maximum(r - lo, 0)
        cnt_c = np.minimum(c + hi, GRID_W) - np.maximum(c - lo, 0)
        invs.append((1.0 / (cnt_r * cnt_c)).astype(np.float32)[:, None])
    return jnp.asarray(np.stack(pcs), BF16), jnp.asarray(np.stack(invs), F32)


def _pool_kernel(v_ref, pc_ref, inv_ref, pw_ref, sc_ref, o_ref, cs_ref, *, n, pad):
    wi = pl.program_id(0)
    nt = n // POOL_TILE
    ch = v_ref.shape[2]
    zeros = jnp.zeros((pad, ch), F32)
    cs_ref[pl.ds(0, pad), :] = zeros
    cs_ref[pl.ds(pad + n, pad), :] = zeros
    pc = pc_ref[0]

    def col_body(i, _):
        off = pl.multiple_of(i * POOL_TILE, POOL_TILE)
        v = v_ref[0, pl.ds(off, POOL_TILE), :]
        hi = v.astype(BF16)
        lo = (v - hi.astype(F32)).astype(BF16)
        cs_ref[pl.ds(pad + off, POOL_TILE), :] = _dot(pc, hi) + _dot(pc, lo)
        return 0

    lax.fori_loop(0, nt, col_body, 0)

    for k, w in enumerate(POOL_WINDOWS):
        @pl.when(wi == k)
        def _(w=w):
            def row_body(i, _):
                off = pl.multiple_of(i * POOL_TILE, POOL_TILE)
                acc = cs_ref[pl.ds(pad + off - (w // 2) * GRID_W, POOL_TILE), :]
                for dlt in range(-(w // 2) + 1, w - w // 2):
                    acc = acc + cs_ref[pl.ds(pad + off + dlt * GRID_W, POOL_TILE), :]
                mixed = acc * inv_ref[0, pl.ds(off, POOL_TILE), :] - v_ref[0, pl.ds(off, POOL_TILE), :]
                y = _dot(mixed.astype(BF16), pw_ref[0]) * sc_ref[0]
                o_ref[0, pl.ds(off, POOL_TILE), :] = y.astype(BF16)
                return 0

            lax.fori_loop(0, nt, row_body, 0)


def _pool(usp, col0, pool_w, pool_scale, pc, inv):
    bsz, n, _ = usp.shape
    nw, ch, _ = pool_w.shape
    pad = (max(POOL_WINDOWS) // 2) * GRID_W
    cb0 = col0 // ch
    return pl.pallas_call(
        functools.partial(_pool_kernel, n=n, pad=pad),
        out_shape=jax.ShapeDtypeStruct((bsz, n, nw * ch), BF16),
        grid=(nw, bsz),
        in_specs=[pl.BlockSpec((1, n, ch), lambda w, b: (b, 0, cb0 + w)),
                  pl.BlockSpec((1, POOL_TILE, POOL_TILE), lambda w, b: (w, 0, 0)),
                  pl.BlockSpec((1, n, 1), lambda w, b: (w, 0, 0)),
                  pl.BlockSpec((1, ch, ch), lambda w, b: (w, 0, 0)),
                  pl.BlockSpec((1, 1, ch), lambda w, b: (w, 0, 0))],
        out_specs=pl.BlockSpec((1, n, ch), lambda w, b: (b, 0, w)),
        scratch_shapes=[pltpu.VMEM((n + 2 * pad, ch), F32)],
        compiler_params=_cparams(("arbitrary", "arbitrary")),
        name="pool",
    )(usp, pc, inv, pool_w, pool_scale)


def _merge_kernel(a_ref, p_ref, wa_ref, wb_ref, ga_ref, gb_ref, o_ref):
    ya = _dot(a_ref[0], wa_ref[...])
    yb = _dot(p_ref[0], wb_ref[...])
    m = jax.nn.sigmoid(ga_ref[0].astype(F32)) * ya + jax.nn.sigmoid(gb_ref[0].astype(F32)) * yb
    o_ref[0] = m.astype(BF16)


def _merge(ag, yp, wa, wb, gates, *, tm, tn):
    bsz, s, ka = ag.shape
    kb = yp.shape[2]
    d = wa.shape[1]
    tm, tn = _tile(s, tm), _tile(d, tn)
    nj = d // tn
    return pl.pallas_call(
        _merge_kernel,
        out_shape=jax.ShapeDtypeStruct((bsz, s, d), BF16),
        grid=(bsz, s // tm, nj),
        in_specs=[pl.BlockSpec((1, tm, ka), lambda b, i, j: (b, i, 0)),
                  pl.BlockSpec((1, tm, kb), lambda b, i, j: (b, i, 0)),
                  pl.BlockSpec((ka, tn), lambda b, i, j: (0, j)),
                  pl.BlockSpec((kb, tn), lambda b, i, j: (0, j)),
                  pl.BlockSpec((1, tm, tn), lambda b, i, j: (b, i, j)),
                  pl.BlockSpec((1, tm, tn), lambda b, i, j: (b, i, nj + j))],
        out_specs=pl.BlockSpec((1, tm, tn), lambda b, i, j: (b, i, j)),
        compiler_params=_cparams(("arbitrary", "arbitrary", "arbitrary")),
        name="merge",
    )(ag, yp, wa, wb, gates, gates)


def kernel(x, c, ctx, c_ctx, w_mod, b_mod, norm_g, final_g, ffn1_w_in, ffn1_w_out, ffn2_w_in, ffn2_w_out,
           w_in, ssm_lambda_re, ssm_lambda_im, ssm_log_step, ssm_b_re, ssm_b_im, ssm_c_re, ssm_c_im, ssm_d,
           w_glu, w_branch_a, pool_w, pool_scale, w_branch_b, w_out):
    bsz, seq, d = x.shape
    assert w_mod.shape[0] == 1, "single-layer problem"
    ssm_w = ssm_d.shape[1]
    pool_width = pool_scale.shape[1]
    groups = ssm_w // SSM_GROUP
    nw = len(POOL_WINDOWS)
    pch = pool_width // nw
    p = ssm_lambda_re.shape[-1]
    tk = SSM_CHUNK * SSM_GROUP
    assert seq % POOL_TILE == 0 and ctx.shape[1] % SSM_CHUNK == 0

    f1_in, f1_out = ffn1_w_in[0].astype(BF16), ffn1_w_out[0].astype(BF16)
    f2_in, f2_out = ffn2_w_in[0].astype(BF16), ffn2_w_out[0].astype(BF16)
    w_sp = w_in[0][:, :ssm_w + pool_width].astype(BF16)
    w_s = w_in[0][:, :ssm_w].astype(BF16)
    w_gates = w_in[0][:, ssm_w + pool_width:].astype(BF16)
    wglu = w_glu[0].astype(BF16)
    wba, wbb = w_branch_a[0].astype(BF16), w_branch_b[0].astype(BF16)
    wo = w_out[0].astype(BF16)
    pw = pool_w[0].astype(BF16)
    psc = pool_scale[0].reshape(nw, 1, pch)
    ng = norm_g[0]
    fg = final_g.reshape(1, d)

    rows = -(-(bsz + 1) // 8) * 8
    c8 = jnp.zeros((rows, d), F32).at[:bsz].set(c).at[bsz].set(c_ctx)
    mods = _mod(c8, w_mod[0], b_mod, tn=512 if w_mod.shape[2] % 512 == 0 else w_mod.shape[2]).reshape(rows, N_MOD, d)
    lat = lambda b: b
    con = lambda b: bsz

    toep, wst, vrd, a_re, a_im = _ssm_tables(ssm_lambda_re[0], ssm_lambda_im[0], ssm_log_step[0],
                                             ssm_b_re[0], ssm_b_im[0], ssm_c_re[0], ssm_c_im[0])
    gb = min(4, groups)

    ctx1 = _ffn(ctx, ng, mods, con, f1_in, f1_out, fg, gi=0, mi=0, tm=512, tf=256, final_norm=False)
    uc = _prenorm(ctx1, ng, mods, con, gi=1, mi=3, tm=512)
    us_c = _mm(uc, w_s, F32, tm=1024, tn=512)
    s_c, = _ssm_in(_to_chunks(us_c, groups), wst, gb=gb, n_intra=0)
    nc_c = ctx.shape[1] // SSM_CHUNK
    h_ctx, = _ssm_rec(s_c.reshape(groups, nc_c, bsz, 4 * p), a_re, a_im,
                      jnp.zeros((groups, bsz, 4 * p), F32), gb=gb, want_prev=False)

    x1 = _ffn(x, ng, mods, lat, f1_in, f1_out, fg, gi=0, mi=0, tm=512, tf=256, final_norm=False)
    u = _prenorm(x1, ng, mods, lat, gi=1, mi=3, tm=512)
    usp = _mm(u, w_sp, F32, tm=1024, tn=512)
    gates = _mm(u, w_gates, BF16, tm=1024, tn=512)

    nc = seq // SSM_CHUNK
    yi, s_x = _ssm_in(_to_chunks(usp[:, :, :ssm_w], groups), jnp.concatenate([toep, wst], axis=-1),
                      gb=gb, n_intra=tk)
    hp, _ = _ssm_rec(s_x.reshape(groups, nc, bsz, 4 * p), a_re, a_im, h_ctx, gb=gb, want_prev=True)
    y_ssm = _from_chunks(_ssm_out(hp.reshape(groups, nc * bsz, 4 * p), vrd, yi, gb=gb), bsz)
    ag = _glu(y_ssm, usp, ssm_d, wglu, tm=512)

    pc, inv = _pool_consts(seq)
    yp = _pool(usp, ssm_w, pw, psc, pc, inv)

    merged = _merge(ag, yp, wba, wbb, gates, tm=1024, tn=512)
    x2 = _outproj(merged, wo, x1, mods, lat, mi=5, tm=1024, tn=512)
    return _ffn(x2, ng, mods, lat, f2_in, f2_out, fg, gi=2, mi=6, tm=512, tf=256, final_norm=True)
```

```python
import functools
import math

import jax
import jax.numpy as jnp
import numpy as np
from jax import lax
from jax.experimental import pallas as pl
from jax.experimental.pallas import tpu as pltpu

BF16 = jnp.bfloat16
F32 = jnp.float32

RMS_EPS = 1e-6
LAMBDA_RE_MAX = -1e-4
HALF = 0.5
N_MOD = 9
SSM_GROUP = 16
POOL_WINDOWS = (2, 4, 8, 16)
GRID_W = 64
SSM_CHUNK = 16

V7X_VMEM_BYTES = 64 * 1024 * 1024
VMEM_LIMIT = 60 * 1024 * 1024


def _cparams(sem):
    return pltpu.CompilerParams(dimension_semantics=sem, vmem_limit_bytes=VMEM_LIMIT)


def _tile(n, pref):
    t = min(n, pref)
    while n % t:
        t //= 2
    return t


ROW_CHUNK = 32


def _for_row_chunks(n, fn):
    def body(i, carry):
        fn(pl.multiple_of(i * ROW_CHUNK, ROW_CHUNK))
        return carry

    lax.fori_loop(0, n // ROW_CHUNK, body, 0)


def _dot(a, b):
    return jnp.dot(a, b, preferred_element_type=F32)


def _rms_mod(x, g, shift, scale):
    xn = x * lax.rsqrt(jnp.mean(x * x, axis=-1, keepdims=True) + RMS_EPS)
    return (xn * g) * (1.0 + scale) + shift


def _mod_kernel(c_ref, w_ref, b_ref, o_ref):
    c = c_ref[...]
    s = (c * jax.nn.sigmoid(c)).astype(BF16)
    o_ref[...] = _dot(s, w_ref[...].astype(BF16)) + b_ref[...]


def _mod(c8, w_mod, b_mod, tn):
    d, n = w_mod.shape
    return pl.pallas_call(
        _mod_kernel,
        out_shape=jax.ShapeDtypeStruct((c8.shape[0], n), F32),
        grid=(n // tn,),
        in_specs=[pl.BlockSpec(c8.shape, lambda j: (0, 0)),
                  pl.BlockSpec((d, tn), lambda j: (0, j)),
                  pl.BlockSpec((1, tn), lambda j: (0, j))],
        out_specs=pl.BlockSpec((c8.shape[0], tn), lambda j: (0, j)),
        compiler_params=_cparams(("arbitrary",)),
        name="mod",
    )(c8, w_mod, b_mod)


FFN_STAGE_ROWS = 128


def _ffn_kernel(x_ref, g_ref, mod_ref, wg_ref, wu_ref, wo_ref, fg_ref, *refs,
                gi, mi, nf, nstage, final_norm, next_pre):
    if next_pre is None:
        o_ref, pre_ref, acc_ref = refs
    else:
        o_ref, u_ref, pre_ref, acc_ref = refs
    f = pl.program_id(2)
    rc = x_ref.shape[1]

    @pl.when(f < nstage)
    def _():
        base = pl.multiple_of(f * rc, rc)

        def rows(r0):
            pre = _rms_mod(x_ref[0, pl.ds(r0, ROW_CHUNK), :], g_ref[pl.ds(gi, 1), :],
                           mod_ref[0, pl.ds(mi, 1), :], mod_ref[0, pl.ds(mi + 1, 1), :])
            pre_ref[pl.ds(base + r0, ROW_CHUNK), :] = pre.astype(BF16)
            acc_ref[pl.ds(base + r0, ROW_CHUNK), :] = jnp.zeros((ROW_CHUNK, acc_ref.shape[1]), F32)

        _for_row_chunks(rc, rows)

    @pl.when((f >= nstage) & (f < nstage + nf))
    def _():
        p = pre_ref[...]
        gate = _dot(p, wg_ref[...])
        up = _dot(p, wu_ref[...])
        act = (gate * jax.nn.sigmoid(gate) * up).astype(BF16)
        acc_ref[...] += _dot(act, wo_ref[...])

    @pl.when(f >= nstage + nf)
    def _():
        base = pl.multiple_of((f - (nstage + nf)) * rc, rc)

        def rows(r0):
            r = pl.ds(r0, ROW_CHUNK)
            y = x_ref[0, r, :] + (HALF * mod_ref[0, pl.ds(mi + 2, 1), :]) * acc_ref[pl.ds(base + r0, ROW_CHUNK), :]
            if final_norm:
                y = y * lax.rsqrt(jnp.mean(y * y, axis=-1, keepdims=True) + RMS_EPS) * fg_ref[...]
            o_ref[0, r, :] = y
            if next_pre is not None:
                gi2, mi2 = next_pre
                u_ref[0, r, :] = _rms_mod(y, g_ref[pl.ds(gi2, 1), :], mod_ref[0, pl.ds(mi2, 1), :],
                                          mod_ref[0, pl.ds(mi2 + 1, 1), :]).astype(BF16)

        _for_row_chunks(rc, rows)


def _ffn(x, norm_g, mods, mod_row, w_in, w_out, final_g, *, gi, mi, tm, tf, final_norm=False, next_pre=None):
    bsz, s, d = x.shape
    ff = w_out.shape[0]
    nf = ff // tf
    tm = _tile(s, tm)
    rc = _tile(tm, FFN_STAGE_ROWS)
    nstage = tm // rc
    kern = functools.partial(_ffn_kernel, gi=gi, mi=mi, nf=nf, nstage=nstage, final_norm=final_norm,
                             next_pre=next_pre)

    def hid(f):
        return jnp.clip(f - nstage, 0, nf - 1)

    def x_blk(b, i, f):
        return b, i * nstage + jnp.where(f < nstage, f, jnp.clip(f - (nstage + nf), 0, nstage - 1)), 0

    def o_blk(b, i, f):
        return b, i * nstage + jnp.clip(f - (nstage + nf), 0, nstage - 1), 0

    out_shape = [jax.ShapeDtypeStruct((bsz, s, d), F32)]
    out_specs = [pl.BlockSpec((1, rc, d), o_blk)]
    if next_pre is not None:
        out_shape.append(jax.ShapeDtypeStruct((bsz, s, d), BF16))
        out_specs.append(pl.BlockSpec((1, rc, d), o_blk))
    return pl.pallas_call(
        kern,
        out_shape=out_shape,
        grid=(bsz, s // tm, nf + 2 * nstage),
        in_specs=[pl.BlockSpec((1, rc, d), x_blk),
                  pl.BlockSpec(norm_g.shape, lambda b, i, f: (0, 0)),
                  pl.BlockSpec((1, N_MOD, d), lambda b, i, f: (mod_row(b), 0, 0)),
                  pl.BlockSpec((d, tf), lambda b, i, f: (0, hid(f))),
                  pl.BlockSpec((d, tf), lambda b, i, f: (0, nf + hid(f))),
                  pl.BlockSpec((tf, d), lambda b, i, f: (hid(f), 0)),
                  pl.BlockSpec((1, d), lambda b, i, f: (0, 0))],
        out_specs=out_specs,
        scratch_shapes=[pltpu.VMEM((tm, d), BF16), pltpu.VMEM((tm, d), F32)],
        compiler_params=_cparams(("arbitrary", "arbitrary", "arbitrary")),
        name="ffn",
    )(x, norm_g, mods, w_in, w_in, w_out, final_g)


def _mm_kernel(a_ref, b_ref, o_ref):
    o_ref[0] = _dot(a_ref[0], b_ref[...]).astype(o_ref.dtype)


def _mm(a, w, out_dtype, *, tm, tn):
    bsz, s, k = a.shape
    n = w.shape[1]
    tm, tn = _tile(s, tm), _tile(n, tn)
    return pl.pallas_call(
        _mm_kernel,
        out_shape=jax.ShapeDtypeStruct((bsz, s, n), out_dtype),
        grid=(bsz, s // tm, n // tn),
        in_specs=[pl.BlockSpec((1, tm, k), lambda b, i, j: (b, i, 0)),
                  pl.BlockSpec((k, tn), lambda b, i, j: (0, j))],
        out_specs=pl.BlockSpec((1, tm, tn), lambda b, i, j: (b, i, j)),
        compiler_params=_cparams(("arbitrary", "arbitrary", "arbitrary")),
        name="mm",
    )(a, w)


def _outproj_kernel(a_ref, w_ref, x_ref, mod_ref, o_ref, *, mi):
    o_ref[0] = x_ref[0] + mod_ref[0, pl.ds(mi, 1), :] * _dot(a_ref[0], w_ref[...])


def _outproj(a, w, x, mods, mod_row, *, mi, tm, tn):
    bsz, s, k = a.shape
    n = w.shape[1]
    tm, tn = _tile(s, tm), _tile(n, tn)
    return pl.pallas_call(
        functools.partial(_outproj_kernel, mi=mi),
        out_shape=jax.ShapeDtypeStruct((bsz, s, n), F32),
        grid=(bsz, s // tm, n // tn),
        in_specs=[pl.BlockSpec((1, tm, k), lambda b, i, j: (b, i, 0)),
                  pl.BlockSpec((k, tn), lambda b, i, j: (0, j)),
                  pl.BlockSpec((1, tm, tn), lambda b, i, j: (b, i, j)),
                  pl.BlockSpec((1, N_MOD, tn), lambda b, i, j: (mod_row(b), 0, j))],
        out_specs=pl.BlockSpec((1, tm, tn), lambda b, i, j: (b, i, j)),
        compiler_params=_cparams(("arbitrary", "arbitrary", "arbitrary")),
        name="outproj",
    )(a, w, x, mods)


def _ssm_tables(lam_re, lam_im, log_step, b_re, b_im, c_re, c_im):
    t = SSM_CHUNK
    hp = lax.Precision.HIGHEST
    lr = jnp.minimum(lam_re.astype(F32), LAMBDA_RE_MAX)
    li = lam_im.astype(F32)
    step = jnp.exp(log_step.astype(F32))[..., None]
    m = jnp.arange(t + 1, dtype=F32)[:, None, None, None]
    mag = jnp.exp(m * (lr * step)[None])
    ang = m * (li * step)[None]
    pw_re, pw_im = mag * jnp.cos(ang), mag * jnp.sin(ang)
    nr, ni = pw_re[1] - 1.0, pw_im[1]
    den = lr * lr + li * li
    q_re, q_im = (nr * lr + ni * li) / den, (ni * lr - nr * li) / den
    bb_re = q_re[..., None] * b_re - q_im[..., None] * b_im
    bb_im = q_re[..., None] * b_im + q_im[..., None] * b_re
    cr, ci = c_re.astype(F32), c_im.astype(F32)

    cl_re = cr[:, :, None] * pw_re[:t].transpose(1, 2, 0, 3)[:, :, :, None, :] \
        - ci[:, :, None] * pw_im[:t].transpose(1, 2, 0, 3)[:, :, :, None, :]
    cl_im = cr[:, :, None] * pw_im[:t].transpose(1, 2, 0, 3)[:, :, :, None, :] \
        + ci[:, :, None] * pw_re[:t].transpose(1, 2, 0, 3)[:, :, :, None, :]
    kern = jnp.einsum('dgtkp,dgpq->dgtkq', cl_re, bb_re, precision=hp) \
        - jnp.einsum('dgtkp,dgpq->dgtkq', cl_im, bb_im, precision=hp)
    i_idx = jnp.arange(t)[:, None]
    j_idx = jnp.arange(t)[None, :]
    kf = kern[0][:, jnp.clip(j_idx - i_idx, 0, t - 1)] * (j_idx >= i_idx)[None, :, :, None, None]
    kb = kern[1][:, jnp.clip(i_idx - j_idx, 0, t - 1)] * (i_idx >= j_idx)[None, :, :, None, None]
    g, kk = kern.shape[1], kern.shape[3]
    toep = (kf + kb).transpose(0, 1, 4, 2, 3).reshape(g, t * kk, t * kk)

    pf_re, pf_im = pw_re[:t][::-1, 0], pw_im[:t][::-1, 0]
    pb_re, pb_im = pw_re[:t, 1], pw_im[:t, 1]

    def st(p_re, p_im, d):
        re = p_re[:, :, :, None] * bb_re[d][None] - p_im[:, :, :, None] * bb_im[d][None]
        im = p_re[:, :, :, None] * bb_im[d][None] + p_im[:, :, :, None] * bb_re[d][None]
        return re.transpose(1, 0, 3, 2), im.transpose(1, 0, 3, 2)

    wf_re, wf_im = st(pf_re, pf_im, 0)
    wb_re, wb_im = st(pb_re, pb_im, 1)
    p = lr.shape[-1]
    wst = jnp.concatenate([wf_re, wb_re, wf_im, wb_im], axis=-1).reshape(g, t * kk, 4 * p)

    vf_pw_re, vf_pw_im = pw_re[1:, 0], pw_im[1:, 0]
    vb_pw_re, vb_pw_im = pw_re[1:, 1][::-1], pw_im[1:, 1][::-1]

    def rd(p_re, p_im, d):
        d_re = cr[d][None] * p_re[:, :, None, :] - ci[d][None] * p_im[:, :, None, :]
        d_im = cr[d][None] * p_im[:, :, None, :] + ci[d][None] * p_re[:, :, None, :]
        return d_re.transpose(1, 3, 0, 2), -d_im.transpose(1, 3, 0, 2)

    vf_re, vf_im = rd(vf_pw_re, vf_pw_im, 0)
    vb_re, vb_im = rd(vb_pw_re, vb_pw_im, 1)
    vrd = jnp.concatenate([vf_re, vb_re, vf_im, vb_im], axis=1).reshape(g, 4 * p, t * kk)

    a_re = jnp.concatenate([pw_re[t, 0], pw_re[t, 1]], axis=-1)[:, None, :]
    a_im = jnp.concatenate([pw_im[t, 0], pw_im[t, 1]], axis=-1)[:, None, :]
    return toep.astype(BF16), wst.astype(BF16), vrd.astype(BF16), a_re, a_im


def _ssm_in_kernel(u_ref, w_ref, *o_refs, gb, n_intra):
    for g in range(gb):
        r = _dot(u_ref[g], w_ref[g])
        if n_intra:
            o_refs[0][g] = r[:, :n_intra]
            o_refs[1][g] = r[:, n_intra:]
        else:
            o_refs[0][g] = r


def _ssm_in(u, w, *, gb, n_intra):
    g, r, tk = u.shape
    n = w.shape[2]
    shapes, specs = [], []
    if n_intra:
        shapes.append(jax.ShapeDtypeStruct((g, r, n_intra), F32))
        specs.append(pl.BlockSpec((gb, r, n_intra), lambda i: (i, 0, 0)))
    shapes.append(jax.ShapeDtypeStruct((g, r, n - n_intra), F32))
    specs.append(pl.BlockSpec((gb, r, n - n_intra), lambda i: (i, 0, 0)))
    return pl.pallas_call(
        functools.partial(_ssm_in_kernel, gb=gb, n_intra=n_intra),
        out_shape=shapes,
        grid=(g // gb,),
        in_specs=[pl.BlockSpec((gb, r, tk), lambda i: (i, 0, 0)),
                  pl.BlockSpec((gb, tk, n), lambda i: (i, 0, 0))],
        out_specs=specs,
        compiler_params=_cparams(("arbitrary",)),
        name="ssm_in",
    )(u, w)


def _ssm_rec_kernel(s_ref, ar_ref, ai_ref, h0_ref, *refs, nc, p2, want_prev):
    if want_prev:
        hp_ref, hfin_ref, tmp_ref = refs
    else:
        hfin_ref, = refs
    ar = ar_ref[...]
    ai = ai_ref[...]
    shape = (s_ref.shape[0], s_ref.shape[2], p2)
    fwd = lax.broadcasted_iota(jnp.int32, shape, 2) < (p2 // 2)

    def body(c, carry):
        h_re, h_im = carry
        cb = nc - 1 - c
        if want_prev:
            hp_ref[:, c, :, :] = jnp.concatenate([h_re, h_im], axis=-1)
            tmp_ref[:, cb, :, :] = jnp.concatenate([h_re, h_im], axis=-1)
        s_f = s_ref[:, c, :, :]
        s_b = s_ref[:, cb, :, :]
        s_re = jnp.where(fwd, s_f[:, :, :p2], s_b[:, :, :p2])
        s_im = jnp.where(fwd, s_f[:, :, p2:], s_b[:, :, p2:])
        n_re = ar * h_re - ai * h_im + s_re
        n_im = ar * h_im + ai * h_re + s_im
        return n_re, n_im

    h0 = h0_ref[...]
    h_re, h_im = lax.fori_loop(0, nc, body, (h0[:, :, :p2], h0[:, :, p2:]))
    hfin_ref[...] = jnp.concatenate([h_re, h_im], axis=-1)
    if want_prev:
        fwd4 = lax.broadcasted_iota(jnp.int32, hp_ref.shape, 3) % p2 < (p2 // 2)
        hp_ref[...] = jnp.where(fwd4, hp_ref[...], tmp_ref[...])


def _ssm_rec(s, a_re, a_im, h0, *, gb, want_prev):
    g, nc, bsz, p4 = s.shape
    p2 = p4 // 2
    shapes = [jax.ShapeDtypeStruct((g, bsz, p4), F32)]
    specs = [pl.BlockSpec((gb, bsz, p4), lambda i: (i, 0, 0))]
    scratch = []
    if want_prev:
        shapes.insert(0, jax.ShapeDtypeStruct(s.shape, F32))
        specs.insert(0, pl.BlockSpec((gb, nc, bsz, p4), lambda i: (i, 0, 0, 0)))
        scratch.append(pltpu.VMEM((gb, nc, bsz, p4), F32))
    return pl.pallas_call(
        functools.partial(_ssm_rec_kernel, nc=nc, p2=p2, want_prev=want_prev),
        out_shape=shapes,
        grid=(g // gb,),
        in_specs=[pl.BlockSpec((gb, nc, bsz, p4), lambda i: (i, 0, 0, 0)),
                  pl.BlockSpec((gb, 1, p2), lambda i: (i, 0, 0)),
                  pl.BlockSpec((gb, 1, p2), lambda i: (i, 0, 0)),
                  pl.BlockSpec((gb, bsz, p4), lambda i: (i, 0, 0))],
        out_specs=specs,
        scratch_shapes=scratch,
        compiler_params=_cparams(("arbitrary",)),
        name="ssm_rec",
    )(s, a_re, a_im, h0)


def _ssm_out_kernel(hp_ref, v_ref, yi_ref, o_ref, *, gb):
    for g in range(gb):
        o_ref[g] = yi_ref[g] + _dot(hp_ref[g].astype(BF16), v_ref[g])


def _ssm_out(hp, v, yi, *, gb):
    g, r, p4 = hp.shape
    tk = v.shape[2]
    return pl.pallas_call(
        functools.partial(_ssm_out_kernel, gb=gb),
        out_shape=jax.ShapeDtypeStruct((g, r, tk), F32),
        grid=(g // gb,),
        in_specs=[pl.BlockSpec((gb, r, p4), lambda i: (i, 0, 0)),
                  pl.BlockSpec((gb, p4, tk), lambda i: (i, 0, 0)),
                  pl.BlockSpec((gb, r, tk), lambda i: (i, 0, 0))],
        out_specs=pl.BlockSpec((gb, r, tk), lambda i: (i, 0, 0)),
        compiler_params=_cparams(("arbitrary",)),
        name="ssm_out",
    )(hp, v, yi)


def _to_chunks(u_s, groups):
    bsz, n, w = u_s.shape
    k = w // groups
    nc = n // SSM_CHUNK
    u = u_s.astype(BF16).reshape(bsz, nc, SSM_CHUNK, groups, k).transpose(3, 1, 0, 2, 4)
    return u.reshape(groups, nc * bsz, SSM_CHUNK * k)


def _from_chunks(y, bsz):
    g, r, tk = y.shape
    k = tk // SSM_CHUNK
    nc = r // bsz
    y = y.reshape(g, nc, bsz, SSM_CHUNK, k).transpose(2, 1, 3, 0, 4)
    return y.reshape(bsz, nc * SSM_CHUNK, g * k)


def _glu_kernel(y_ref, u_ref, d_ref, w_ref, o_ref, *, ws):
    z = jax.nn.gelu(y_ref[0] + d_ref[...] * u_ref[0]).astype(BF16)
    a = _dot(z, w_ref[:, :ws])
    b = _dot(z, w_ref[:, ws:])
    o_ref[0] = (a * jax.nn.sigmoid(b)).astype(BF16)


def _glu(y, usp, d, w_glu, *, tm):
    bsz, s, ws = y.shape
    tm = _tile(s, tm)
    return pl.pallas_call(
        functools.partial(_glu_kernel, ws=ws),
        out_shape=jax.ShapeDtypeStruct((bsz, s, ws), BF16),
        grid=(bsz, s // tm),
        in_specs=[pl.BlockSpec((1, tm, ws), lambda b, i: (b, i, 0)),
                  pl.BlockSpec((1, tm, ws), lambda b, i: (b, i, 0)),
                  pl.BlockSpec((1, ws), lambda b, i: (0, 0)),
                  pl.BlockSpec(w_glu.shape, lambda b, i: (0, 0))],
        out_specs=pl.BlockSpec((1, tm, ws), lambda b, i: (b, i, 0)),
        compiler_params=_cparams(("arbitrary", "arbitrary")),
        name="glu",
    )(y, usp, d, w_glu)


POOL_TILE = 4 * GRID_W


def _pool_consts(n):
    t = np.arange(POOL_TILE)
    pcs, invs = [], []
    r = np.arange(n) // GRID_W
    c = np.arange(n) % GRID_W
    rows = n // GRID_W
    for w in POOL_WINDOWS:
        lo, hi = w // 2, w - w // 2
        same_row = (t[:, None] // GRID_W) == (t[None, :] // GRID_W)
        dc = (t[None, :] % GRID_W) - (t[:, None] % GRID_W)
        pcs.append((same_row & (dc >= -lo) & (dc < hi)).astype(np.float32))
        cnt_r = np.minimum(r + hi, rows) - np.maximum(r - lo, 0)
        cnt_c = np.minimum(c + hi, GRID_W) - np.maximum(c - lo, 0)
        invs.append((1.0 / (cnt_r * cnt_c)).astype(np.float32)[:, None])
    return jnp.asarray(np.stack(pcs), BF16), jnp.asarray(np.stack(invs), F32)


def _pool_kernel(v_ref, pc_ref, inv_ref, pw_ref, sc_ref, o_ref, cs_ref, *, n, pad):
    wi = pl.program_id(0)
    nt = n // POOL_TILE
    ch = v_ref.shape[2]
    zeros = jnp.zeros((pad, ch), F32)
    cs_ref[pl.ds(0, pad), :] = zeros
    cs_ref[pl.ds(pad + n, pad), :] = zeros
    pc = pc_ref[0]

    def col_body(i, _):
        off = pl.multiple_of(i * POOL_TILE, POOL_TILE)
        v = v_ref[0, pl.ds(off, POOL_TILE), :]
        hi = v.astype(BF16)
        lo = (v - hi.astype(F32)).astype(BF16)
        cs_ref[pl.ds(pad + off, POOL_TILE), :] = _dot(pc, hi) + _dot(pc, lo)
        return 0

    lax.fori_loop(0, nt, col_body, 0)

    for k, w in enumerate(POOL_WINDOWS):
        @pl.when(wi == k)
        def _(w=w):
            def row_body(i, _):
                off = pl.multiple_of(i * POOL_TILE, POOL_TILE)
                acc = cs_ref[pl.ds(pad + off - (w // 2) * GRID_W, POOL_TILE), :]
                for dlt in range(-(w // 2) + 1, w - w // 2):
                    acc = acc + cs_ref[pl.ds(pad + off + dlt * GRID_W, POOL_TILE), :]
                mixed = acc * inv_ref[0, pl.ds(off, POOL_TILE), :] - v_ref[0, pl.ds(off, POOL_TILE), :]
                y = _dot(mixed.astype(BF16), pw_ref[0]) * sc_ref[0]
                o_ref[0, pl.ds(off, POOL_TILE), :] = y.astype(BF16)
                return 0

            lax.fori_loop(0, nt, row_body, 0)


def _pool(usp, col0, pool_w, pool_scale, pc, inv):
    bsz, n, _ = usp.shape
    nw, ch, _ = pool_w.shape
    pad = (max(POOL_WINDOWS) // 2) * GRID_W
    cb0 = col0 // ch
    return pl.pallas_call(
        functools.partial(_pool_kernel, n=n, pad=pad),
        out_shape=jax.ShapeDtypeStruct((bsz, n, nw * ch), BF16),
        grid=(nw, bsz),
        in_specs=[pl.BlockSpec((1, n, ch), lambda w, b: (b, 0, cb0 + w)),
                  pl.BlockSpec((1, POOL_TILE, POOL_TILE), lambda w, b: (w, 0, 0)),
                  pl.BlockSpec((1, n, 1), lambda w, b: (w, 0, 0)),
                  pl.BlockSpec((1, ch, ch), lambda w, b: (w, 0, 0)),
                  pl.BlockSpec((1, 1, ch), lambda w, b: (w, 0, 0))],
        out_specs=pl.BlockSpec((1, n, ch), lambda w, b: (b, 0, w)),
        scratch_shapes=[pltpu.VMEM((n + 2 * pad, ch), F32)],
        compiler_params=_cparams(("arbitrary", "arbitrary")),
        name="pool",
    )(usp, pc, inv, pool_w, pool_scale)


def _merge_kernel(a_ref, p_ref, wa_ref, wb_ref, ga_ref, gb_ref, o_ref):
    ya = _dot(a_ref[0], wa_ref[...])
    yb = _dot(p_ref[0], wb_ref[...])
    m = jax.nn.sigmoid(ga_ref[0].astype(F32)) * ya + jax.nn.sigmoid(gb_ref[0].astype(F32)) * yb
    o_ref[0] = m.astype(BF16)


def _merge(ag, yp, wa, wb, gates, *, tm, tn):
    bsz, s, ka = ag.shape
    kb = yp.shape[2]
    d = wa.shape[1]
    tm, tn = _tile(s, tm), _tile(d, tn)
    nj = d // tn
    return pl.pallas_call(
        _merge_kernel,
        out_shape=jax.ShapeDtypeStruct((bsz, s, d), BF16),
        grid=(bsz, s // tm, nj),
        in_specs=[pl.BlockSpec((1, tm, ka), lambda b, i, j: (b, i, 0)),
                  pl.BlockSpec((1, tm, kb), lambda b, i, j: (b, i, 0)),
                  pl.BlockSpec((ka, tn), lambda b, i, j: (0, j)),
                  pl.BlockSpec((kb, tn), lambda b, i, j: (0, j)),
                  pl.BlockSpec((1, tm, tn), lambda b, i, j: (b, i, j)),
                  pl.BlockSpec((1, tm, tn), lambda b, i, j: (b, i, nj + j))],
        out_specs=pl.BlockSpec((1, tm, tn), lambda b, i, j: (b, i, j)),
        compiler_params=_cparams(("arbitrary", "arbitrary", "arbitrary")),
        name="merge",
    )(ag, yp, wa, wb, gates, gates)


def kernel(x, c, ctx, c_ctx, w_mod, b_mod, norm_g, final_g, ffn1_w_in, ffn1_w_out, ffn2_w_in, ffn2_w_out,
           w_in, ssm_lambda_re, ssm_lambda_im, ssm_log_step, ssm_b_re, ssm_b_im, ssm_c_re, ssm_c_im, ssm_d,
           w_glu, w_branch_a, pool_w, pool_scale, w_branch_b, w_out):
    bsz, seq, d = x.shape
    assert w_mod.shape[0] == 1, "single-layer problem"
    ssm_w = ssm_d.shape[1]
    pool_width = pool_scale.shape[1]
    groups = ssm_w // SSM_GROUP
    nw = len(POOL_WINDOWS)
    pch = pool_width // nw
    p = ssm_lambda_re.shape[-1]
    tk = SSM_CHUNK * SSM_GROUP
    assert seq % POOL_TILE == 0 and ctx.shape[1] % SSM_CHUNK == 0

    f1_in, f1_out = ffn1_w_in[0].astype(BF16), ffn1_w_out[0].astype(BF16)
    f2_in, f2_out = ffn2_w_in[0].astype(BF16), ffn2_w_out[0].astype(BF16)
    w_sp = w_in[0][:, :ssm_w + pool_width].astype(BF16)
    w_s = w_in[0][:, :ssm_w].astype(BF16)
    w_gates = w_in[0][:, ssm_w + pool_width:].astype(BF16)
    wglu = w_glu[0].astype(BF16)
    wba, wbb = w_branch_a[0].astype(BF16), w_branch_b[0].astype(BF16)
    wo = w_out[0].astype(BF16)
    pw = pool_w[0].astype(BF16)
    psc = pool_scale[0].reshape(nw, 1, pch)
    ng = norm_g[0]
    fg = final_g.reshape(1, d)

    rows = -(-(bsz + 1) // 8) * 8
    c8 = jnp.zeros((rows, d), F32).at[:bsz].set(c).at[bsz].set(c_ctx)
    mods = _mod(c8, w_mod[0], b_mod, tn=512 if w_mod.shape[2] % 512 == 0 else w_mod.shape[2]).reshape(rows, N_MOD, d)
    lat = lambda b: b
    con = lambda b: bsz

    toep, wst, vrd, a_re, a_im = _ssm_tables(ssm_lambda_re[0], ssm_lambda_im[0], ssm_log_step[0],
                                             ssm_b_re[0], ssm_b_im[0], ssm_c_re[0], ssm_c_im[0])
    gb = min(4, groups)

    nctx = ctx.shape[1]
    _, uc = _ffn(ctx.reshape(1, bsz * nctx, d), ng, mods, con, f1_in, f1_out, fg, gi=0, mi=0, tm=1024, tf=256,
                 next_pre=(1, 3))
    us_c = _mm(uc, w_s, F32, tm=1024, tn=512).reshape(bsz, nctx, ssm_w)
    s_c, = _ssm_in(_to_chunks(us_c, groups), wst, gb=gb, n_intra=0)
    nc_c = nctx // SSM_CHUNK
    h_ctx, = _ssm_rec(s_c.reshape(groups, nc_c, bsz, 4 * p), a_re, a_im,
                      jnp.zeros((groups, bsz, 4 * p), F32), gb=gb, want_prev=False)

    x1, u = _ffn(x, ng, mods, lat, f1_in, f1_out, fg, gi=0, mi=0, tm=1024, tf=256, next_pre=(1, 3))
    usp = _mm(u, w_sp, F32, tm=1024, tn=512)
    gates = _mm(u, w_gates, BF16, tm=1024, tn=512)

    nc = seq // SSM_CHUNK
    yi, s_x = _ssm_in(_to_chunks(usp[:, :, :ssm_w], groups), jnp.concatenate([toep, wst], axis=-1),
                      gb=gb, n_intra=tk)
    hp, _ = _ssm_rec(s_x.reshape(groups, nc, bsz, 4 * p), a_re, a_im, h_ctx, gb=gb, want_prev=True)
    y_ssm = _from_chunks(_ssm_out(hp.reshape(groups, nc * bsz, 4 * p), vrd, yi, gb=gb), bsz)
    ag = _glu(y_ssm, usp, ssm_d, wglu, tm=512)

    pc, inv = _pool_consts(seq)
    yp = _pool(usp, ssm_w, pw, psc, pc, inv)

    merged = _merge(ag, yp, wba, wbb, gates, tm=1024, tn=512)
    x2 = _outproj(merged, wo, x1, mods, lat, mi=5, tm=1024, tn=512)
    out, = _ffn(x2, ng, mods, lat, f2_in, f2_out, fg, gi=2, mi=6, tm=1024, tf=256, final_norm=True)
    return out
```

```python
import functools
import math

import jax
import jax.numpy as jnp
import numpy as np
from jax import lax
from jax.experimental import pallas as pl
from jax.experimental.pallas import tpu as pltpu

BF16 = jnp.bfloat16
F32 = jnp.float32

RMS_EPS = 1e-6
LAMBDA_RE_MAX = -1e-4
HALF = 0.5
N_MOD = 9
SSM_GROUP = 16
POOL_WINDOWS = (2, 4, 8, 16)
GRID_W = 64
SSM_CHUNK = 16

V7X_VMEM_BYTES = 64 * 1024 * 1024
VMEM_LIMIT = 60 * 1024 * 1024


def _cparams(sem):
    return pltpu.CompilerParams(dimension_semantics=sem, vmem_limit_bytes=VMEM_LIMIT)


def _tile(n, pref):
    t = min(n, pref)
    while n % t:
        t //= 2
    return t


ROW_CHUNK = 32


def _for_row_chunks(n, fn):
    def body(i, carry):
        fn(pl.multiple_of(i * ROW_CHUNK, ROW_CHUNK))
        return carry

    lax.fori_loop(0, n // ROW_CHUNK, body, 0)


def _dot(a, b):
    return jnp.dot(a, b, preferred_element_type=F32)


def _rms_mod(x, g, shift, scale):
    xn = x * lax.rsqrt(jnp.mean(x * x, axis=-1, keepdims=True) + RMS_EPS)
    return (xn * g) * (1.0 + scale) + shift


def _mod_kernel(c_ref, w_ref, b_ref, o_ref):
    c = c_ref[...]
    s = (c * jax.nn.sigmoid(c)).astype(BF16)
    o_ref[...] = _dot(s, w_ref[...].astype(BF16)) + b_ref[...]


def _mod(c8, w_mod, b_mod, tn):
    d, n = w_mod.shape
    return pl.pallas_call(
        _mod_kernel,
        out_shape=jax.ShapeDtypeStruct((c8.shape[0], n), F32),
        grid=(n // tn,),
        in_specs=[pl.BlockSpec(c8.shape, lambda j: (0, 0)),
                  pl.BlockSpec((d, tn), lambda j: (0, j)),
                  pl.BlockSpec((1, tn), lambda j: (0, j))],
        out_specs=pl.BlockSpec((c8.shape[0], tn), lambda j: (0, j)),
        compiler_params=_cparams(("arbitrary",)),
        name="mod",
    )(c8, w_mod, b_mod)


FFN_STAGE_ROWS = 128


def _ffn_kernel(x_ref, g_ref, mod_ref, wg_ref, wu_ref, wo_ref, fg_ref, *refs,
                gi, mi, nf, nstage, final_norm, next_pre):
    if next_pre is None:
        o_ref, pre_ref, acc_ref = refs
    else:
        o_ref, u_ref, pre_ref, acc_ref = refs
    f = pl.program_id(2)
    rc = x_ref.shape[1]

    @pl.when(f < nstage)
    def _():
        base = pl.multiple_of(f * rc, rc)

        def rows(r0):
            pre = _rms_mod(x_ref[0, pl.ds(r0, ROW_CHUNK), :], g_ref[pl.ds(gi, 1), :],
                           mod_ref[0, pl.ds(mi, 1), :], mod_ref[0, pl.ds(mi + 1, 1), :])
            pre_ref[pl.ds(base + r0, ROW_CHUNK), :] = pre.astype(BF16)
            acc_ref[pl.ds(base + r0, ROW_CHUNK), :] = jnp.zeros((ROW_CHUNK, acc_ref.shape[1]), F32)

        _for_row_chunks(rc, rows)

    @pl.when((f >= nstage) & (f < nstage + nf))
    def _():
        p = pre_ref[...]
        gate = _dot(p, wg_ref[...])
        up = _dot(p, wu_ref[...])
        act = (gate * jax.nn.sigmoid(gate) * up).astype(BF16)
        acc_ref[...] += _dot(act, wo_ref[...])

    @pl.when(f >= nstage + nf)
    def _():
        base = pl.multiple_of((f - (nstage + nf)) * rc, rc)

        def rows(r0):
            r = pl.ds(r0, ROW_CHUNK)
            y = x_ref[0, r, :] + (HALF * mod_ref[0, pl.ds(mi + 2, 1), :]) * acc_ref[pl.ds(base + r0, ROW_CHUNK), :]
            if final_norm:
                y = y * lax.rsqrt(jnp.mean(y * y, axis=-1, keepdims=True) + RMS_EPS) * fg_ref[...]
            o_ref[0, r, :] = y
            if next_pre is not None:
                gi2, mi2 = next_pre
                u_ref[0, r, :] = _rms_mod(y, g_ref[pl.ds(gi2, 1), :], mod_ref[0, pl.ds(mi2, 1), :],
                                          mod_ref[0, pl.ds(mi2 + 1, 1), :]).astype(BF16)

        _for_row_chunks(rc, rows)


def _ffn(x, norm_g, mods, mod_row, w_in, w_out, final_g, *, gi, mi, tm, tf, final_norm=False, next_pre=None):
    bsz, s, d = x.shape
    ff = w_out.shape[0]
    nf = ff // tf
    tm = _tile(s, tm)
    rc = _tile(tm, FFN_STAGE_ROWS)
    nstage = tm // rc
    kern = functools.partial(_ffn_kernel, gi=gi, mi=mi, nf=nf, nstage=nstage, final_norm=final_norm,
                             next_pre=next_pre)

    def hid(f):
        return jnp.clip(f - nstage, 0, nf - 1)

    def x_blk(b, i, f):
        return b, i * nstage + jnp.where(f < nstage, f, jnp.clip(f - (nstage + nf), 0, nstage - 1)), 0

    def o_blk(b, i, f):
        return b, i * nstage + jnp.clip(f - (nstage + nf), 0, nstage - 1), 0

    out_shape = [jax.ShapeDtypeStruct((bsz, s, d), F32)]
    out_specs = [pl.BlockSpec((1, rc, d), o_blk)]
    if next_pre is not None:
        out_shape.append(jax.ShapeDtypeStruct((bsz, s, d), BF16))
        out_specs.append(pl.BlockSpec((1, rc, d), o_blk))
    return pl.pallas_call(
        kern,
        out_shape=out_shape,
        grid=(bsz, s // tm, nf + 2 * nstage),
        in_specs=[pl.BlockSpec((1, rc, d), x_blk),
                  pl.BlockSpec(norm_g.shape, lambda b, i, f: (0, 0)),
                  pl.BlockSpec((1, N_MOD, d), lambda b, i, f: (mod_row(b), 0, 0)),
                  pl.BlockSpec((d, tf), lambda b, i, f: (0, hid(f))),
                  pl.BlockSpec((d, tf), lambda b, i, f: (0, nf + hid(f))),
                  pl.BlockSpec((tf, d), lambda b, i, f: (hid(f), 0)),
                  pl.BlockSpec((1, d), lambda b, i, f: (0, 0))],
        out_specs=out_specs,
        scratch_shapes=[pltpu.VMEM((tm, d), BF16), pltpu.VMEM((tm, d), F32)],
        compiler_params=_cparams(("arbitrary", "arbitrary", "arbitrary")),
        name="ffn",
    )(x, norm_g, mods, w_in, w_in, w_out, final_g)


def _mm_kernel(a_ref, b_ref, o_ref):
    o_ref[0] = _dot(a_ref[0], b_ref[...]).astype(o_ref.dtype)


def _mm(a, w, out_dtype, *, tm, tn):
    bsz, s, k = a.shape
    n = w.shape[1]
    tm, tn = _tile(s, tm), _tile(n, tn)
    return pl.pallas_call(
        _mm_kernel,
        out_shape=jax.ShapeDtypeStruct((bsz, s, n), out_dtype),
        grid=(bsz, s // tm, n // tn),
        in_specs=[pl.BlockSpec((1, tm, k), lambda b, i, j: (b, i, 0)),
                  pl.BlockSpec((k, tn), lambda b, i, j: (0, j))],
        out_specs=pl.BlockSpec((1, tm, tn), lambda b, i, j: (b, i, j)),
        compiler_params=_cparams(("arbitrary", "arbitrary", "arbitrary")),
        name="mm",
    )(a, w)


def _outproj_kernel(a_ref, w_ref, x_ref, mod_ref, o_ref, *, mi):
    o_ref[0] = x_ref[0] + mod_ref[0, pl.ds(mi, 1), :] * _dot(a_ref[0], w_ref[...])


def _outproj(a, w, x, mods, mod_row, *, mi, tm, tn):
    bsz, s, k = a.shape
    n = w.shape[1]
    tm, tn = _tile(s, tm), _tile(n, tn)
    return pl.pallas_call(
        functools.partial(_outproj_kernel, mi=mi),
        out_shape=jax.ShapeDtypeStruct((bsz, s, n), F32),
        grid=(bsz, s // tm, n // tn),
        in_specs=[pl.BlockSpec((1, tm, k), lambda b, i, j: (b, i, 0)),
                  pl.BlockSpec((k, tn), lambda b, i, j: (0, j)),
                  pl.BlockSpec((1, tm, tn), lambda b, i, j: (b, i, j)),
                  pl.BlockSpec((1, N_MOD, tn), lambda b, i, j: (mod_row(b), 0, j))],
        out_specs=pl.BlockSpec((1, tm, tn), lambda b, i, j: (b, i, j)),
        compiler_params=_cparams(("arbitrary", "arbitrary", "arbitrary")),
        name="outproj",
    )(a, w, x, mods)


def _ssm_tables(lam_re, lam_im, log_step, b_re, b_im, c_re, c_im):
    t = SSM_CHUNK
    hp = lax.Precision.HIGHEST
    lr = jnp.minimum(lam_re.astype(F32), LAMBDA_RE_MAX)
    li = lam_im.astype(F32)
    step = jnp.exp(log_step.astype(F32))[..., None]
    m = jnp.arange(t + 1, dtype=F32)[:, None, None, None]
    mag = jnp.exp(m * (lr * step)[None])
    ang = m * (li * step)[None]
    pw_re, pw_im = mag * jnp.cos(ang), mag * jnp.sin(ang)
    nr, ni = pw_re[1] - 1.0, pw_im[1]
    den = lr * lr + li * li
    q_re, q_im = (nr * lr + ni * li) / den, (ni * lr - nr * li) / den
    bb_re = q_re[..., None] * b_re - q_im[..., None] * b_im
    bb_im = q_re[..., None] * b_im + q_im[..., None] * b_re
    cr, ci = c_re.astype(F32), c_im.astype(F32)

    cl_re = cr[:, :, None] * pw_re[:t].transpose(1, 2, 0, 3)[:, :, :, None, :] \
        - ci[:, :, None] * pw_im[:t].transpose(1, 2, 0, 3)[:, :, :, None, :]
    cl_im = cr[:, :, None] * pw_im[:t].transpose(1, 2, 0, 3)[:, :, :, None, :] \
        + ci[:, :, None] * pw_re[:t].transpose(1, 2, 0, 3)[:, :, :, None, :]
    kern = jnp.einsum('dgtkp,dgpq->dgtkq', cl_re, bb_re, precision=hp) \
        - jnp.einsum('dgtkp,dgpq->dgtkq', cl_im, bb_im, precision=hp)
    i_idx = jnp.arange(t)[:, None]
    j_idx = jnp.arange(t)[None, :]
    kf = kern[0][:, jnp.clip(j_idx - i_idx, 0, t - 1)] * (j_idx >= i_idx)[None, :, :, None, None]
    kb = kern[1][:, jnp.clip(i_idx - j_idx, 0, t - 1)] * (i_idx >= j_idx)[None, :, :, None, None]
    g, kk = kern.shape[1], kern.shape[3]
    toep = (kf + kb).transpose(0, 1, 4, 2, 3).reshape(g, t * kk, t * kk)

    pf_re, pf_im = pw_re[:t][::-1, 0], pw_im[:t][::-1, 0]
    pb_re, pb_im = pw_re[:t, 1], pw_im[:t, 1]

    def st(p_re, p_im, d):
        re = p_re[:, :, :, None] * bb_re[d][None] - p_im[:, :, :, None] * bb_im[d][None]
        im = p_re[:, :, :, None] * bb_im[d][None] + p_im[:, :, :, None] * bb_re[d][None]
        return re.transpose(1, 0, 3, 2), im.transpose(1, 0, 3, 2)

    wf_re, wf_im = st(pf_re, pf_im, 0)
    wb_re, wb_im = st(pb_re, pb_im, 1)
    p = lr.shape[-1]
    wst = jnp.concatenate([wf_re, wb_re, wf_im, wb_im], axis=-1).reshape(g, t * kk, 4 * p)

    vf_pw_re, vf_pw_im = pw_re[1:, 0], pw_im[1:, 0]
    vb_pw_re, vb_pw_im = pw_re[1:, 1][::-1], pw_im[1:, 1][::-1]

    def rd(p_re, p_im, d):
        d_re = cr[d][None] * p_re[:, :, None, :] - ci[d][None] * p_im[:, :, None, :]
        d_im = cr[d][None] * p_im[:, :, None, :] + ci[d][None] * p_re[:, :, None, :]
        return d_re.transpose(1, 3, 0, 2), -d_im.transpose(1, 3, 0, 2)

    vf_re, vf_im = rd(vf_pw_re, vf_pw_im, 0)
    vb_re, vb_im = rd(vb_pw_re, vb_pw_im, 1)
    vrd = jnp.concatenate([vf_re, vb_re, vf_im, vb_im], axis=1).reshape(g, 4 * p, t * kk)

    a_re = jnp.concatenate([pw_re[t, 0], pw_re[t, 1]], axis=-1)[:, None, :]
    a_im = jnp.concatenate([pw_im[t, 0], pw_im[t, 1]], axis=-1)[:, None, :]
    return toep.astype(BF16), wst.astype(BF16), vrd.astype(BF16), a_re, a_im


LANES = 128
SSM_GPB = LANES // SSM_GROUP


def _ssm_block_tables(toep, wst, vrd, a_re, a_im):
    g, tk, _ = toep.shape
    t, k, gpb = SSM_CHUNK, SSM_GROUP, SSM_GPB
    nb = g // gpb
    p2 = wst.shape[2] // 2
    eye = jnp.eye(gpb, dtype=toep.dtype)
    toep_b = jnp.einsum('bgikjl,gh->bigkjhl', toep.reshape(nb, gpb, t, k, t, k), eye)
    toep_b = toep_b.reshape(nb, t * gpb * k, t * gpb * k)
    wst_b = jnp.einsum('bgikrq,gh->bigkrhq', wst.reshape(nb, gpb, t, k, 2, p2), eye)
    wst_b = wst_b.reshape(nb, t * gpb * k, 2 * gpb * p2)
    vrd_b = jnp.einsum('bgrqjl,gh->brgqjhl', vrd.reshape(nb, gpb, 2, p2, t, k), eye)
    vrd_b = vrd_b.reshape(nb, 2 * gpb * p2, t * gpb * k)
    return toep_b, wst_b, vrd_b, a_re.reshape(nb, 1, gpb * p2), a_im.reshape(nb, 1, gpb * p2)


def _ssm_kernel(u_ref, ws_ref, ar_ref, ai_ref, h0_ref, *refs, nc, want_y):
    t = SSM_CHUNK
    if want_y:
        wt_ref, v_ref, d_ref, z_ref, hfin_ref, s_ref, yi_ref, hp_ref, tmp_ref, y_ref = refs
    else:
        hfin_ref, s_ref = refs
    a = jnp.concatenate([u_ref[0, pl.ds(j, nc, stride=t), :] for j in range(t)], axis=-1).astype(BF16)
    s_ref[...] = _dot(a, ws_ref[0])
    if want_y:
        yi_ref[...] = _dot(a, wt_ref[0])

    hw = ar_ref.shape[2]
    ar = ar_ref[0]
    ai = ai_ref[0]
    p2 = hw // SSM_GPB
    fwd = lax.broadcasted_iota(jnp.int32, (1, hw), 1) % p2 < (p2 // 2)

    def body(c, carry):
        h_re, h_im = carry
        cb = nc - 1 - c
        if want_y:
            h = jnp.concatenate([h_re, h_im], axis=-1)
            hp_ref[pl.ds(c, 1), :] = h
            tmp_ref[pl.ds(cb, 1), :] = h
        s_f = s_ref[pl.ds(c, 1), :]
        s_b = s_ref[pl.ds(cb, 1), :]
        s_re = jnp.where(fwd, s_f[:, :hw], s_b[:, :hw])
        s_im = jnp.where(fwd, s_f[:, hw:], s_b[:, hw:])
        return ar * h_re - ai * h_im + s_re, ar * h_im + ai * h_re + s_im

    h0 = h0_ref[0, 0]
    h_re, h_im = lax.fori_loop(0, nc, body, (h0[:, :hw], h0[:, hw:]))
    hfin_ref[0, 0] = jnp.concatenate([h_re, h_im], axis=-1)

    if want_y:
        fwd2 = lax.broadcasted_iota(jnp.int32, hp_ref.shape, 1) % p2 < (p2 // 2)
        hp = jnp.where(fwd2, hp_ref[...], tmp_ref[...]).astype(BF16)
        y = yi_ref[...] + _dot(hp, v_ref[0])
        for j in range(t):
            y_ref[pl.ds(j, nc, stride=t), :] = y[:, j * LANES:(j + 1) * LANES]

        def rows(r0):
            r = pl.ds(r0, ROW_CHUNK)
            z_ref[0, r, :] = jax.nn.gelu(y_ref[r, :] + d_ref[...] * u_ref[0, r, :]).astype(BF16)

        _for_row_chunks(nc * t, rows)


def _ssm(u, tables, h0, d, *, want_y):
    toep_b, wst_b, vrd_b, ar, ai = tables
    bsz, n, _ = u.shape
    nb, tl, sw = wst_b.shape
    nc = n // SSM_CHUNK
    single = pl.Buffered(1)
    shapes = [jax.ShapeDtypeStruct((bsz, nb, 1, sw), F32)]
    specs = [pl.BlockSpec((1, 1, 1, sw), lambda k, b: (b, k, 0, 0))]
    scratch = [pltpu.VMEM((nc, sw), F32)]
    in_specs = [pl.BlockSpec((1, n, LANES), lambda k, b: (b, 0, k)),
                pl.BlockSpec((1, tl, sw), lambda k, b: (k, 0, 0), pipeline_mode=single),
                pl.BlockSpec((1, 1, sw // 2), lambda k, b: (k, 0, 0)),
                pl.BlockSpec((1, 1, sw // 2), lambda k, b: (k, 0, 0)),
                pl.BlockSpec((1, 1, 1, sw), lambda k, b: (b, k, 0, 0))]
    args = [u, wst_b, ar, ai, h0]
    if want_y:
        in_specs += [pl.BlockSpec((1, tl, tl), lambda k, b: (k, 0, 0), pipeline_mode=single),
                     pl.BlockSpec((1, sw, tl), lambda k, b: (k, 0, 0), pipeline_mode=single),
                     pl.BlockSpec((1, LANES), lambda k, b: (0, k))]
        args += [toep_b, vrd_b, d]
        shapes.insert(0, jax.ShapeDtypeStruct((bsz, n, nb * LANES), BF16))
        specs.insert(0, pl.BlockSpec((1, n, LANES), lambda k, b: (b, 0, k)))
        scratch += [pltpu.VMEM((nc, tl), F32), pltpu.VMEM((nc, sw), F32), pltpu.VMEM((nc, sw), F32),
                    pltpu.VMEM((n, LANES), F32)]
    return pl.pallas_call(
        functools.partial(_ssm_kernel, nc=nc, want_y=want_y),
        out_shape=shapes,
        grid=(nb, bsz),
        in_specs=in_specs,
        out_specs=specs,
        scratch_shapes=scratch,
        compiler_params=_cparams(("arbitrary", "arbitrary")),
        name="ssm",
    )(*args)


def _glu_kernel(z_ref, w_ref, o_ref, *, ws):
    z = z_ref[0]
    a = _dot(z, w_ref[:, :ws])
    b = _dot(z, w_ref[:, ws:])
    o_ref[0] = (a * jax.nn.sigmoid(b)).astype(BF16)


def _glu(z, w_glu, *, tm):
    bsz, s, ws = z.shape
    tm = _tile(s, tm)
    return pl.pallas_call(
        functools.partial(_glu_kernel, ws=ws),
        out_shape=jax.ShapeDtypeStruct((bsz, s, ws), BF16),
        grid=(bsz, s // tm),
        in_specs=[pl.BlockSpec((1, tm, ws), lambda b, i: (b, i, 0)),
                  pl.BlockSpec(w_glu.shape, lambda b, i: (0, 0))],
        out_specs=pl.BlockSpec((1, tm, ws), lambda b, i: (b, i, 0)),
        compiler_params=_cparams(("arbitrary", "arbitrary")),
        name="glu",
    )(z, w_glu)


POOL_TILE = 4 * GRID_W


def _pool_consts(n):
    t = np.arange(POOL_TILE)
    pcs, invs = [], []
    r = np.arange(n) // GRID_W
    c = np.arange(n) % GRID_W
    rows = n // GRID_W
    for w in POOL_WINDOWS:
        lo, hi = w // 2, w - w // 2
        same_row = (t[:, None] // GRID_W) == (t[None, :] // GRID_W)
        dc = (t[None, :] % GRID_W) - (t[:, None] % GRID_W)
        pcs.append((same_row & (dc >= -lo) & (dc < hi)).astype(np.float32))
        cnt_r = np.minimum(r + hi, rows) - np.maximum(r - lo, 0)
        cnt_c = np.minimum(c + hi, GRID_W) - np.maximum(c - lo, 0)
        invs.append((1.0 / (cnt_r * cnt_c)).astype(np.float32)[:, None])
    return jnp.asarray(np.stack(pcs), BF16), jnp.asarray(np.stack(invs), F32)


def _pool_kernel(v_ref, pc_ref, inv_ref, pw_ref, sc_ref, o_ref, cs_ref, *, n, pad):
    wi = pl.program_id(0)
    nt = n // POOL_TILE
    ch = v_ref.shape[2]
    zeros = jnp.zeros((pad, ch), F32)
    cs_ref[pl.ds(0, pad), :] = zeros
    cs_ref[pl.ds(pad + n, pad), :] = zeros
    pc = pc_ref[0]

    def col_body(i, _):
        off = pl.multiple_of(i * POOL_TILE, POOL_TILE)
        v = v_ref[0, pl.ds(off, POOL_TILE), :]
        hi = v.astype(BF16)
        lo = (v - hi.astype(F32)).astype(BF16)
        cs_ref[pl.ds(pad + off, POOL_TILE), :] = _dot(pc, hi) + _dot(pc, lo)
        return 0

    lax.fori_loop(0, nt, col_body, 0)

    for k, w in enumerate(POOL_WINDOWS):
        @pl.when(wi == k)
        def _(w=w):
            def row_body(i, _):
                off = pl.multiple_of(i * POOL_TILE, POOL_TILE)
                acc = cs_ref[pl.ds(pad + off - (w // 2) * GRID_W, POOL_TILE), :]
                for dlt in range(-(w // 2) + 1, w - w // 2):
                    acc = acc + cs_ref[pl.ds(pad + off + dlt * GRID_W, POOL_TILE), :]
                mixed = acc * inv_ref[0, pl.ds(off, POOL_TILE), :] - v_ref[0, pl.ds(off, POOL_TILE), :]
                y = _dot(mixed.astype(BF16), pw_ref[0]) * sc_ref[0]
                o_ref[0, pl.ds(off, POOL_TILE), :] = y.astype(BF16)
                return 0

            lax.fori_loop(0, nt, row_body, 0)


def _pool(usp, col0, pool_w, pool_scale, pc, inv):
    bsz, n, _ = usp.shape
    nw, ch, _ = pool_w.shape
    pad = (max(POOL_WINDOWS) // 2) * GRID_W
    cb0 = col0 // ch
    return pl.pallas_call(
        functools.partial(_pool_kernel, n=n, pad=pad),
        out_shape=jax.ShapeDtypeStruct((bsz, n, nw * ch), BF16),
        grid=(nw, bsz),
        in_specs=[pl.BlockSpec((1, n, ch), lambda w, b: (b, 0, cb0 + w)),
                  pl.BlockSpec((1, POOL_TILE, POOL_TILE), lambda w, b: (w, 0, 0)),
                  pl.BlockSpec((1, n, 1), lambda w, b: (w, 0, 0)),
                  pl.BlockSpec((1, ch, ch), lambda w, b: (w, 0, 0)),
                  pl.BlockSpec((1, 1, ch), lambda w, b: (w, 0, 0))],
        out_specs=pl.BlockSpec((1, n, ch), lambda w, b: (b, 0, w)),
        scratch_shapes=[pltpu.VMEM((n + 2 * pad, ch), F32)],
        compiler_params=_cparams(("arbitrary", "arbitrary")),
        name="pool",
    )(usp, pc, inv, pool_w, pool_scale)


def _merge_kernel(a_ref, p_ref, wa_ref, wb_ref, ga_ref, gb_ref, o_ref):
    ya = _dot(a_ref[0], wa_ref[...])
    yb = _dot(p_ref[0], wb_ref[...])
    m = jax.nn.sigmoid(ga_ref[0].astype(F32)) * ya + jax.nn.sigmoid(gb_ref[0].astype(F32)) * yb
    o_ref[0] = m.astype(BF16)


def _merge(ag, yp, wa, wb, gates, *, tm, tn):
    bsz, s, ka = ag.shape
    kb = yp.shape[2]
    d = wa.shape[1]
    tm, tn = _tile(s, tm), _tile(d, tn)
    nj = d // tn
    return pl.pallas_call(
        _merge_kernel,
        out_shape=jax.ShapeDtypeStruct((bsz, s, d), BF16),
        grid=(bsz, s // tm, nj),
        in_specs=[pl.BlockSpec((1, tm, ka), lambda b, i, j: (b, i, 0)),
                  pl.BlockSpec((1, tm, kb), lambda b, i, j: (b, i, 0)),
                  pl.BlockSpec((ka, tn), lambda b, i, j: (0, j)),
                  pl.BlockSpec((kb, tn), lambda b, i, j: (0, j)),
                  pl.BlockSpec((1, tm, tn), lambda b, i, j: (b, i, j)),
                  pl.BlockSpec((1, tm, tn), lambda b, i, j: (b, i, nj + j))],
        out_specs=pl.BlockSpec((1, tm, tn), lambda b, i, j: (b, i, j)),
        compiler_params=_cparams(("arbitrary", "arbitrary", "arbitrary")),
        name="merge",
    )(ag, yp, wa, wb, gates, gates)


def kernel(x, c, ctx, c_ctx, w_mod, b_mod, norm_g, final_g, ffn1_w_in, ffn1_w_out, ffn2_w_in, ffn2_w_out,
           w_in, ssm_lambda_re, ssm_lambda_im, ssm_log_step, ssm_b_re, ssm_b_im, ssm_c_re, ssm_c_im, ssm_d,
           w_glu, w_branch_a, pool_w, pool_scale, w_branch_b, w_out):
    bsz, seq, d = x.shape
    assert w_mod.shape[0] == 1, "single-layer problem"
    ssm_w = ssm_d.shape[1]
    pool_width = pool_scale.shape[1]
    nw = len(POOL_WINDOWS)
    pch = pool_width // nw
    p = ssm_lambda_re.shape[-1]
    assert seq % POOL_TILE == 0 and ctx.shape[1] % SSM_CHUNK == 0 and ssm_w % LANES == 0

    f1_in, f1_out = ffn1_w_in[0].astype(BF16), ffn1_w_out[0].astype(BF16)
    f2_in, f2_out = ffn2_w_in[0].astype(BF16), ffn2_w_out[0].astype(BF16)
    w_sp = w_in[0][:, :ssm_w + pool_width].astype(BF16)
    w_s = w_in[0][:, :ssm_w].astype(BF16)
    w_gates = w_in[0][:, ssm_w + pool_width:].astype(BF16)
    wglu = w_glu[0].astype(BF16)
    wba, wbb = w_branch_a[0].astype(BF16), w_branch_b[0].astype(BF16)
    wo = w_out[0].astype(BF16)
    pw = pool_w[0].astype(BF16)
    psc = pool_scale[0].reshape(nw, 1, pch)
    ng = norm_g[0]
    fg = final_g.reshape(1, d)

    rows = -(-(bsz + 1) // 8) * 8
    c8 = jnp.zeros((rows, d), F32).at[:bsz].set(c).at[bsz].set(c_ctx)
    mods = _mod(c8, w_mod[0], b_mod, tn=512 if w_mod.shape[2] % 512 == 0 else w_mod.shape[2]).reshape(rows, N_MOD, d)
    lat = lambda b: b
    con = lambda b: bsz

    tables = _ssm_block_tables(*_ssm_tables(ssm_lambda_re[0], ssm_lambda_im[0], ssm_log_step[0],
                                            ssm_b_re[0], ssm_b_im[0], ssm_c_re[0], ssm_c_im[0]))

    nctx = ctx.shape[1]
    _, uc = _ffn(ctx.reshape(1, bsz * nctx, d), ng, mods, con, f1_in, f1_out, fg, gi=0, mi=0, tm=1024, tf=256,
                 next_pre=(1, 3))
    us_c = _mm(uc, w_s, F32, tm=1024, tn=512).reshape(bsz, nctx, ssm_w)
    h_ctx, = _ssm(us_c, tables, jnp.zeros((bsz, ssm_w // LANES, 1, 4 * p * SSM_GPB), F32), ssm_d, want_y=False)

    x1, u = _ffn(x, ng, mods, lat, f1_in, f1_out, fg, gi=0, mi=0, tm=1024, tf=256, next_pre=(1, 3))
    usp = _mm(u, w_sp, F32, tm=1024, tn=512)
    gates = _mm(u, w_gates, BF16, tm=1024, tn=512)

    z, _ = _ssm(usp, tables, h_ctx, ssm_d, want_y=True)
    ag = _glu(z, wglu, tm=512)

    pc, inv = _pool_consts(seq)
    yp = _pool(usp, ssm_w, pw, psc, pc, inv)

    merged = _merge(ag, yp, wba, wbb, gates, tm=1024, tn=512)
    x2 = _outproj(merged, wo, x1, mods, lat, mi=5, tm=1024, tn=512)
    out, = _ffn(x2, ng, mods, lat, f2_in, f2_out, fg, gi=2, mi=6, tm=1024, tf=256, final_norm=True)
    return out
```

```python
import functools
import math

import jax
import jax.numpy as jnp
import numpy as np
from jax import lax
from jax.experimental import pallas as pl
from jax.experimental.pallas import tpu as pltpu

BF16 = jnp.bfloat16
F32 = jnp.float32

RMS_EPS = 1e-6
LAMBDA_RE_MAX = -1e-4
HALF = 0.5
N_MOD = 9
SSM_GROUP = 16
POOL_WINDOWS = (2, 4, 8, 16)
GRID_W = 64
SSM_CHUNK = 16

V7X_VMEM_BYTES = 64 * 1024 * 1024
VMEM_LIMIT = 60 * 1024 * 1024


def _cparams(sem):
    return pltpu.CompilerParams(dimension_semantics=sem, vmem_limit_bytes=VMEM_LIMIT)


def _tile(n, pref):
    t = min(n, pref)
    while n % t:
        t //= 2
    return t


ROW_CHUNK = 32


def _for_row_chunks(n, fn):
    def body(i, carry):
        fn(pl.multiple_of(i * ROW_CHUNK, ROW_CHUNK))
        return carry

    lax.fori_loop(0, n // ROW_CHUNK, body, 0)


def _dot(a, b):
    return jnp.dot(a, b, preferred_element_type=F32)


def _rms_mod(x, g, shift, scale):
    xn = x * lax.rsqrt(jnp.mean(x * x, axis=-1, keepdims=True) + RMS_EPS)
    return (xn * g) * (1.0 + scale) + shift


def _mod_kernel(c_ref, w_ref, b_ref, o_ref):
    c = c_ref[...]
    s = (c * jax.nn.sigmoid(c)).astype(BF16)
    o_ref[...] = _dot(s, w_ref[...].astype(BF16)) + b_ref[...]


def _mod(c8, w_mod, b_mod, tn):
    d, n = w_mod.shape
    return pl.pallas_call(
        _mod_kernel,
        out_shape=jax.ShapeDtypeStruct((c8.shape[0], n), F32),
        grid=(n // tn,),
        in_specs=[pl.BlockSpec(c8.shape, lambda j: (0, 0)),
                  pl.BlockSpec((d, tn), lambda j: (0, j)),
                  pl.BlockSpec((1, tn), lambda j: (0, j))],
        out_specs=pl.BlockSpec((c8.shape[0], tn), lambda j: (0, j)),
        compiler_params=_cparams(("arbitrary",)),
        name="mod",
    )(c8, w_mod, b_mod)


FFN_STAGE_ROWS = 128


def _ffn_kernel(x_ref, g_ref, mod_ref, wg_ref, wu_ref, wo_ref, fg_ref, *refs,
                gi, mi, nf, nstage, final_norm, next_pre):
    if next_pre is None:
        o_ref, pre_ref, acc_ref = refs
    else:
        o_ref, u_ref, pre_ref, acc_ref = refs
    f = pl.program_id(2)
    rc = x_ref.shape[1]

    @pl.when(f < nstage)
    def _():
        base = pl.multiple_of(f * rc, rc)

        def rows(r0):
            pre = _rms_mod(x_ref[0, pl.ds(r0, ROW_CHUNK), :], g_ref[pl.ds(gi, 1), :],
                           mod_ref[0, pl.ds(mi, 1), :], mod_ref[0, pl.ds(mi + 1, 1), :])
            pre_ref[pl.ds(base + r0, ROW_CHUNK), :] = pre.astype(BF16)
            acc_ref[pl.ds(base + r0, ROW_CHUNK), :] = jnp.zeros((ROW_CHUNK, acc_ref.shape[1]), F32)

        _for_row_chunks(rc, rows)

    @pl.when((f >= nstage) & (f < nstage + nf))
    def _():
        p = pre_ref[...]
        gate = _dot(p, wg_ref[...])
        up = _dot(p, wu_ref[...])
        act = (gate * jax.nn.sigmoid(gate) * up).astype(BF16)
        acc_ref[...] += _dot(act, wo_ref[...])

    @pl.when(f >= nstage + nf)
    def _():
        base = pl.multiple_of((f - (nstage + nf)) * rc, rc)

        def rows(r0):
            r = pl.ds(r0, ROW_CHUNK)
            y = x_ref[0, r, :] + (HALF * mod_ref[0, pl.ds(mi + 2, 1), :]) * acc_ref[pl.ds(base + r0, ROW_CHUNK), :]
            if final_norm:
                y = y * lax.rsqrt(jnp.mean(y * y, axis=-1, keepdims=True) + RMS_EPS) * fg_ref[...]
            o_ref[0, r, :] = y
            if next_pre is not None:
                gi2, mi2 = next_pre
                u_ref[0, r, :] = _rms_mod(y, g_ref[pl.ds(gi2, 1), :], mod_ref[0, pl.ds(mi2, 1), :],
                                          mod_ref[0, pl.ds(mi2 + 1, 1), :]).astype(BF16)

        _for_row_chunks(rc, rows)


def _ffn(x, norm_g, mods, mod_row, w_in, w_out, final_g, *, gi, mi, tm, tf, final_norm=False, next_pre=None):
    bsz, s, d = x.shape
    ff = w_out.shape[0]
    nf = ff // tf
    tm = _tile(s, tm)
    rc = _tile(tm, FFN_STAGE_ROWS)
    nstage = tm // rc
    kern = functools.partial(_ffn_kernel, gi=gi, mi=mi, nf=nf, nstage=nstage, final_norm=final_norm,
                             next_pre=next_pre)

    def hid(f):
        return jnp.clip(f - nstage, 0, nf - 1)

    def x_blk(b, i, f):
        return b, i * nstage + jnp.where(f < nstage, f, jnp.clip(f - (nstage + nf), 0, nstage - 1)), 0

    def o_blk(b, i, f):
        return b, i * nstage + jnp.clip(f - (nstage + nf), 0, nstage - 1), 0

    out_shape = [jax.ShapeDtypeStruct((bsz, s, d), F32)]
    out_specs = [pl.BlockSpec((1, rc, d), o_blk)]
    if next_pre is not None:
        out_shape.append(jax.ShapeDtypeStruct((bsz, s, d), BF16))
        out_specs.append(pl.BlockSpec((1, rc, d), o_blk))
    return pl.pallas_call(
        kern,
        out_shape=out_shape,
        grid=(bsz, s // tm, nf + 2 * nstage),
        in_specs=[pl.BlockSpec((1, rc, d), x_blk),
                  pl.BlockSpec(norm_g.shape, lambda b, i, f: (0, 0)),
                  pl.BlockSpec((1, N_MOD, d), lambda b, i, f: (mod_row(b), 0, 0)),
                  pl.BlockSpec((d, tf), lambda b, i, f: (0, hid(f))),
                  pl.BlockSpec((d, tf), lambda b, i, f: (0, nf + hid(f))),
                  pl.BlockSpec((tf, d), lambda b, i, f: (hid(f), 0)),
                  pl.BlockSpec((1, d), lambda b, i, f: (0, 0))],
        out_specs=out_specs,
        scratch_shapes=[pltpu.VMEM((tm, d), BF16), pltpu.VMEM((tm, d), F32)],
        compiler_params=_cparams(("arbitrary", "arbitrary", "arbitrary")),
        name="ffn",
    )(x, norm_g, mods, w_in, w_in, w_out, final_g)


def _mm_kernel(a_ref, b_ref, o_ref):
    o_ref[0] = _dot(a_ref[0], b_ref[...]).astype(o_ref.dtype)


def _mm(a, w, out_dtype, *, tm, tn):
    bsz, s, k = a.shape
    n = w.shape[1]
    tm, tn = _tile(s, tm), _tile(n, tn)
    return pl.pallas_call(
        _mm_kernel,
        out_shape=jax.ShapeDtypeStruct((bsz, s, n), out_dtype),
        grid=(bsz, s // tm, n // tn),
        in_specs=[pl.BlockSpec((1, tm, k), lambda b, i, j: (b, i, 0)),
                  pl.BlockSpec((k, tn), lambda b, i, j: (0, j))],
        out_specs=pl.BlockSpec((1, tm, tn), lambda b, i, j: (b, i, j)),
        compiler_params=_cparams(("arbitrary", "arbitrary", "arbitrary")),
        name="mm",
    )(a, w)


def _outproj_kernel(a_ref, w_ref, x_ref, mod_ref, o_ref, *, mi):
    o_ref[0] = x_ref[0] + mod_ref[0, pl.ds(mi, 1), :] * _dot(a_ref[0], w_ref[...])


def _outproj(a, w, x, mods, mod_row, *, mi, tm, tn):
    bsz, s, k = a.shape
    n = w.shape[1]
    tm, tn = _tile(s, tm), _tile(n, tn)
    return pl.pallas_call(
        functools.partial(_outproj_kernel, mi=mi),
        out_shape=jax.ShapeDtypeStruct((bsz, s, n), F32),
        grid=(bsz, s // tm, n // tn),
        in_specs=[pl.BlockSpec((1, tm, k), lambda b, i, j: (b, i, 0)),
                  pl.BlockSpec((k, tn), lambda b, i, j: (0, j)),
                  pl.BlockSpec((1, tm, tn), lambda b, i, j: (b, i, j)),
                  pl.BlockSpec((1, N_MOD, tn), lambda b, i, j: (mod_row(b), 0, j))],
        out_specs=pl.BlockSpec((1, tm, tn), lambda b, i, j: (b, i, j)),
        compiler_params=_cparams(("arbitrary", "arbitrary", "arbitrary")),
        name="outproj",
    )(a, w, x, mods)


def _ssm_tables(lam_re, lam_im, log_step, b_re, b_im, c_re, c_im):
    t = SSM_CHUNK
    hp = lax.Precision.HIGHEST
    lr = jnp.minimum(lam_re.astype(F32), LAMBDA_RE_MAX)
    li = lam_im.astype(F32)
    step = jnp.exp(log_step.astype(F32))[..., None]
    m = jnp.arange(t + 1, dtype=F32)[:, None, None, None]
    mag = jnp.exp(m * (lr * step)[None])
    ang = m * (li * step)[None]
    pw_re, pw_im = mag * jnp.cos(ang), mag * jnp.sin(ang)
    nr, ni = pw_re[1] - 1.0, pw_im[1]
    den = lr * lr + li * li
    q_re, q_im = (nr * lr + ni * li) / den, (ni * lr - nr * li) / den
    bb_re = q_re[..., None] * b_re - q_im[..., None] * b_im
    bb_im = q_re[..., None] * b_im + q_im[..., None] * b_re
    cr, ci = c_re.astype(F32), c_im.astype(F32)

    cl_re = cr[:, :, None] * pw_re[:t].transpose(1, 2, 0, 3)[:, :, :, None, :] \
        - ci[:, :, None] * pw_im[:t].transpose(1, 2, 0, 3)[:, :, :, None, :]
    cl_im = cr[:, :, None] * pw_im[:t].transpose(1, 2, 0, 3)[:, :, :, None, :] \
        + ci[:, :, None] * pw_re[:t].transpose(1, 2, 0, 3)[:, :, :, None, :]
    kern = jnp.einsum('dgtkp,dgpq->dgtkq', cl_re, bb_re, precision=hp) \
        - jnp.einsum('dgtkp,dgpq->dgtkq', cl_im, bb_im, precision=hp)
    g, kk = kern.shape[1], kern.shape[3]
    zero = (jnp.arange(t) == 0).astype(F32)[None, :, None, None]
    kpos = kern[0] + zero * kern[1]
    kneg = kern[1] + zero * kern[0]
    rpos = kpos.transpose(0, 3, 1, 2).reshape(g, kk, t * kk)
    rneg = kneg.transpose(0, 3, 1, 2).reshape(g, kk, t * kk)

    pf_re, pf_im = pw_re[:t][::-1, 0], pw_im[:t][::-1, 0]
    pb_re, pb_im = pw_re[:t, 1], pw_im[:t, 1]

    def st(p_re, p_im, d):
        re = p_re[:, :, :, None] * bb_re[d][None] - p_im[:, :, :, None] * bb_im[d][None]
        im = p_re[:, :, :, None] * bb_im[d][None] + p_im[:, :, :, None] * bb_re[d][None]
        return re.transpose(1, 0, 3, 2), im.transpose(1, 0, 3, 2)

    wf_re, wf_im = st(pf_re, pf_im, 0)
    wb_re, wb_im = st(pb_re, pb_im, 1)
    p = lr.shape[-1]
    wst = jnp.concatenate([wf_re, wb_re, wf_im, wb_im], axis=-1).reshape(g, t * kk, 4 * p)

    vf_pw_re, vf_pw_im = pw_re[1:, 0], pw_im[1:, 0]
    vb_pw_re, vb_pw_im = pw_re[1:, 1][::-1], pw_im[1:, 1][::-1]

    def rd(p_re, p_im, d):
        d_re = cr[d][None] * p_re[:, :, None, :] - ci[d][None] * p_im[:, :, None, :]
        d_im = cr[d][None] * p_im[:, :, None, :] + ci[d][None] * p_re[:, :, None, :]
        return d_re.transpose(1, 3, 0, 2), -d_im.transpose(1, 3, 0, 2)

    vf_re, vf_im = rd(vf_pw_re, vf_pw_im, 0)
    vb_re, vb_im = rd(vb_pw_re, vb_pw_im, 1)
    vrd = jnp.concatenate([vf_re, vb_re, vf_im, vb_im], axis=1).reshape(g, 4 * p, t * kk)

    a_re = jnp.concatenate([pw_re[t, 0], pw_re[t, 1]], axis=-1)[:, None, :]
    a_im = jnp.concatenate([pw_im[t, 0], pw_im[t, 1]], axis=-1)[:, None, :]
    return rpos.astype(BF16), rneg.astype(BF16), wst.astype(BF16), vrd.astype(BF16), a_re, a_im


LANES = 128
SSM_GPB = LANES // SSM_GROUP


def _ssm_block_tables(rpos, rneg, wst, vrd, a_re, a_im):
    rpos, rneg, wst, vrd = lax.optimization_barrier((rpos, rneg, wst, vrd))
    g, _, tk = rpos.shape
    t, k, gpb = SSM_CHUNK, SSM_GROUP, SSM_GPB
    nb = g // gpb
    p2 = wst.shape[2] // 2
    tl, hw = t * gpb * k, gpb * p2
    rpos = rpos.reshape(nb, gpb * k, tk)
    rneg = rneg.reshape(nb, gpb * k, tk)
    rw = wst.reshape(nb, gpb, t, k, 2 * p2).transpose(0, 2, 1, 3, 4).reshape(nb, t, gpb * k, 2 * p2)
    rv = vrd.reshape(nb, gpb, 2, p2, tk).transpose(0, 2, 1, 3, 4).reshape(nb, 2, hw, tk)
    jj, kk = np.arange(tk) // k, np.arange(tk) % k
    sel = (jj[None, :, None] == np.arange(t)[:, None, None]) & (kk[None, :, None] == (np.arange(LANES) % k)[None, None, :])
    sel = jnp.asarray(sel, BF16)
    toep_b, wst_b, vrd_b = pl.pallas_call(
        functools.partial(_ssm_expand_kernel, t=t, k=k, p2=p2),
        out_shape=[jax.ShapeDtypeStruct((nb, tl, tl), BF16), jax.ShapeDtypeStruct((nb, tl, 2 * hw), BF16),
                   jax.ShapeDtypeStruct((nb, 2 * hw, tl), BF16)],
        grid=(nb, t),
        in_specs=[pl.BlockSpec((1, gpb * k, tk), lambda b, i: (b, 0, 0)),
                  pl.BlockSpec((1, gpb * k, tk), lambda b, i: (b, 0, 0)),
                  pl.BlockSpec((1, 1, gpb * k, 2 * p2), lambda b, i: (b, i, 0, 0)),
                  pl.BlockSpec((1, 2, hw, tk), lambda b, i: (b, 0, 0, 0)),
                  pl.BlockSpec(sel.shape, lambda b, i: (0, 0, 0))],
        out_specs=[pl.BlockSpec((1, gpb * k, tl), lambda b, i: (b, i, 0)),
                   pl.BlockSpec((1, gpb * k, 2 * hw), lambda b, i: (b, i, 0)),
                   pl.BlockSpec((1, 2 * hw, gpb * k), lambda b, i: (b, 0, i))],
        compiler_params=_cparams(("arbitrary", "arbitrary")),
        name="ssm_expand",
    )(rpos, rneg, rw, rv, sel)
    return toep_b, wst_b, vrd_b, a_re.reshape(nb, 1, hw), a_im.reshape(nb, 1, hw)


def _ssm_expand_kernel(rpos_ref, rneg_ref, rw_ref, rv_ref, sel_ref, toep_ref, wst_ref, vrd_ref, *, t, k, p2):
    i = pl.program_id(1)
    lb = rpos_ref.shape[1]
    hw = rv_ref.shape[2]
    gpb = lb // k
    same_g = (lax.broadcasted_iota(jnp.int32, (lb, lb), 0) // k) == (lax.broadcasted_iota(jnp.int32, (lb, lb), 1) // k)
    for j in range(t):
        pos = _dot(rpos_ref[0], sel_ref[jnp.clip(j - i, 0, t - 1)])
        neg = _dot(rneg_ref[0], sel_ref[jnp.clip(i - j, 0, t - 1)])
        blk = jnp.where(i <= j, pos, neg)
        toep_ref[0, :, j * lb:(j + 1) * lb] = jnp.where(same_g, blk, 0.0).astype(BF16)
    w = rw_ref[0, 0]
    same_gw = (lax.broadcasted_iota(jnp.int32, (lb, hw), 0) // k) == (lax.broadcasted_iota(jnp.int32, (lb, hw), 1) // p2)
    for r in range(2):
        wr = jnp.concatenate([w[:, r * p2:(r + 1) * p2]] * gpb, axis=-1)
        wst_ref[0, :, r * hw:(r + 1) * hw] = jnp.where(same_gw, wr, jnp.zeros_like(wr))
    same_gv = (lax.broadcasted_iota(jnp.int32, (hw, lb), 0) // p2) == (lax.broadcasted_iota(jnp.int32, (hw, lb), 1) // k)
    for r in range(2):
        v = _dot(rv_ref[0, r], sel_ref[i])
        vrd_ref[0, r * hw:(r + 1) * hw, :] = jnp.where(same_gv, v, 0.0).astype(BF16)


def _ssm_kernel(u_ref, ws_ref, ar_ref, ai_ref, h0_ref, *refs, nc, want_y):
    t = SSM_CHUNK
    if want_y:
        wt_ref, v_ref, d_ref, z_ref, hfin_ref, s_ref, yi_ref, hp_ref, tmp_ref, y_ref = refs
    else:
        hfin_ref, s_ref = refs
    a = jnp.concatenate([u_ref[0, pl.ds(j, nc, stride=t), :] for j in range(t)], axis=-1).astype(BF16)
    s_ref[...] = _dot(a, ws_ref[0])
    if want_y:
        yi_ref[...] = _dot(a, wt_ref[0])

    hw = ar_ref.shape[2]
    ar = ar_ref[0]
    ai = ai_ref[0]
    p2 = hw // SSM_GPB
    fwd = lax.broadcasted_iota(jnp.int32, (1, hw), 1) % p2 < (p2 // 2)

    def body(c, carry):
        h_re, h_im = carry
        cb = nc - 1 - c
        if want_y:
            h = jnp.concatenate([h_re, h_im], axis=-1)
            hp_ref[pl.ds(c, 1), :] = h
            tmp_ref[pl.ds(cb, 1), :] = h
        s_f = s_ref[pl.ds(c, 1), :]
        s_b = s_ref[pl.ds(cb, 1), :]
        s_re = jnp.where(fwd, s_f[:, :hw], s_b[:, :hw])
        s_im = jnp.where(fwd, s_f[:, hw:], s_b[:, hw:])
        return ar * h_re - ai * h_im + s_re, ar * h_im + ai * h_re + s_im

    h0 = h0_ref[0, 0]
    h_re, h_im = lax.fori_loop(0, nc, body, (h0[:, :hw], h0[:, hw:]))
    hfin_ref[0, 0] = jnp.concatenate([h_re, h_im], axis=-1)

    if want_y:
        fwd2 = lax.broadcasted_iota(jnp.int32, hp_ref.shape, 1) % p2 < (p2 // 2)
        hp = jnp.where(fwd2, hp_ref[...], tmp_ref[...]).astype(BF16)
        y = yi_ref[...] + _dot(hp, v_ref[0])
        for j in range(t):
            y_ref[pl.ds(j, nc, stride=t), :] = y[:, j * LANES:(j + 1) * LANES]

        def rows(r0):
            r = pl.ds(r0, ROW_CHUNK)
            z_ref[0, r, :] = jax.nn.gelu(y_ref[r, :] + d_ref[...] * u_ref[0, r, :]).astype(BF16)

        _for_row_chunks(nc * t, rows)


def _ssm(u, tables, h0, d, *, want_y):
    toep_b, wst_b, vrd_b, ar, ai = tables
    bsz, n, _ = u.shape
    nb, tl, sw = wst_b.shape
    nc = n // SSM_CHUNK
    single = pl.Buffered(1)
    shapes = [jax.ShapeDtypeStruct((bsz, nb, 1, sw), F32)]
    specs = [pl.BlockSpec((1, 1, 1, sw), lambda k, b: (b, k, 0, 0))]
    scratch = [pltpu.VMEM((nc, sw), F32)]
    in_specs = [pl.BlockSpec((1, n, LANES), lambda k, b: (b, 0, k)),
                pl.BlockSpec((1, tl, sw), lambda k, b: (k, 0, 0), pipeline_mode=single),
                pl.BlockSpec((1, 1, sw // 2), lambda k, b: (k, 0, 0)),
                pl.BlockSpec((1, 1, sw // 2), lambda k, b: (k, 0, 0)),
                pl.BlockSpec((1, 1, 1, sw), lambda k, b: (b, k, 0, 0))]
    args = [u, wst_b, ar, ai, h0]
    if want_y:
        in_specs += [pl.BlockSpec((1, tl, tl), lambda k, b: (k, 0, 0), pipeline_mode=single),
                     pl.BlockSpec((1, sw, tl), lambda k, b: (k, 0, 0), pipeline_mode=single),
                     pl.BlockSpec((1, LANES), lambda k, b: (0, k))]
        args += [toep_b, vrd_b, d]
        shapes.insert(0, jax.ShapeDtypeStruct((bsz, n, nb * LANES), BF16))
        specs.insert(0, pl.BlockSpec((1, n, LANES), lambda k, b: (b, 0, k)))
        scratch += [pltpu.VMEM((nc, tl), F32), pltpu.VMEM((nc, sw), F32), pltpu.VMEM((nc, sw), F32),
                    pltpu.VMEM((n, LANES), F32)]
    return pl.pallas_call(
        functools.partial(_ssm_kernel, nc=nc, want_y=want_y),
        out_shape=shapes,
        grid=(nb, bsz),
        in_specs=in_specs,
        out_specs=specs,
        scratch_shapes=scratch,
        compiler_params=_cparams(("arbitrary", "arbitrary")),
        name="ssm",
    )(*args)


def _glu_kernel(z_ref, w_ref, o_ref, *, ws):
    z = z_ref[0]
    a = _dot(z, w_ref[:, :ws])
    b = _dot(z, w_ref[:, ws:])
    o_ref[0] = (a * jax.nn.sigmoid(b)).astype(BF16)


def _glu(z, w_glu, *, tm):
    bsz, s, ws = z.shape
    tm = _tile(s, tm)
    return pl.pallas_call(
        functools.partial(_glu_kernel, ws=ws),
        out_shape=jax.ShapeDtypeStruct((bsz, s, ws), BF16),
        grid=(bsz, s // tm),
        in_specs=[pl.BlockSpec((1, tm, ws), lambda b, i: (b, i, 0)),
                  pl.BlockSpec(w_glu.shape, lambda b, i: (0, 0))],
        out_specs=pl.BlockSpec((1, tm, ws), lambda b, i: (b, i, 0)),
        compiler_params=_cparams(("arbitrary", "arbitrary")),
        name="glu",
    )(z, w_glu)


POOL_TILE = 4 * GRID_W


def _pool_consts(n):
    t = np.arange(POOL_TILE)
    pcs, invs = [], []
    r = np.arange(n) // GRID_W
    c = np.arange(n) % GRID_W
    rows = n // GRID_W
    for w in POOL_WINDOWS:
        lo, hi = w // 2, w - w // 2
        same_row = (t[:, None] // GRID_W) == (t[None, :] // GRID_W)
        dc = (t[None, :] % GRID_W) - (t[:, None] % GRID_W)
        pcs.append((same_row & (dc >= -lo) & (dc < hi)).astype(np.float32))
        cnt_r = np.minimum(r + hi, rows) - np.maximum(r - lo, 0)
        cnt_c = np.minimum(c + hi, GRID_W) - np.maximum(c - lo, 0)
        invs.append((1.0 / (cnt_r * cnt_c)).astype(np.float32)[:, None])
    return jnp.asarray(np.stack(pcs), BF16), jnp.asarray(np.stack(invs), F32)


def _pool_kernel(v_ref, pc_ref, inv_ref, pw_ref, sc_ref, o_ref, cs_ref, *, n, pad):
    wi = pl.program_id(0)
    nt = n // POOL_TILE
    ch = v_ref.shape[2]
    zeros = jnp.zeros((pad, ch), F32)
    cs_ref[pl.ds(0, pad), :] = zeros
    cs_ref[pl.ds(pad + n, pad), :] = zeros
    pc = pc_ref[0]

    def col_body(i, _):
        off = pl.multiple_of(i * POOL_TILE, POOL_TILE)
        v = v_ref[0, pl.ds(off, POOL_TILE), :]
        hi = v.astype(BF16)
        lo = (v - hi.astype(F32)).astype(BF16)
        cs_ref[pl.ds(pad + off, POOL_TILE), :] = _dot(pc, hi) + _dot(pc, lo)
        return 0

    lax.fori_loop(0, nt, col_body, 0)

    for k, w in enumerate(POOL_WINDOWS):
        @pl.when(wi == k)
        def _(w=w):
            def row_body(i, _):
                off = pl.multiple_of(i * POOL_TILE, POOL_TILE)
                acc = cs_ref[pl.ds(pad + off - (w // 2) * GRID_W, POOL_TILE), :]
                for dlt in range(-(w // 2) + 1, w - w // 2):
                    acc = acc + cs_ref[pl.ds(pad + off + dlt * GRID_W, POOL_TILE), :]
                mixed = acc * inv_ref[0, pl.ds(off, POOL_TILE), :] - v_ref[0, pl.ds(off, POOL_TILE), :]
                y = _dot(mixed.astype(BF16), pw_ref[0]) * sc_ref[0]
                o_ref[0, pl.ds(off, POOL_TILE), :] = y.astype(BF16)
                return 0

            lax.fori_loop(0, nt, row_body, 0)


def _pool(usp, col0, pool_w, pool_scale, pc, inv):
    bsz, n, _ = usp.shape
    nw, ch, _ = pool_w.shape
    pad = (max(POOL_WINDOWS) // 2) * GRID_W
    cb0 = col0 // ch
    return pl.pallas_call(
        functools.partial(_pool_kernel, n=n, pad=pad),
        out_shape=jax.ShapeDtypeStruct((bsz, n, nw * ch), BF16),
        grid=(nw, bsz),
        in_specs=[pl.BlockSpec((1, n, ch), lambda w, b: (b, 0, cb0 + w)),
                  pl.BlockSpec((1, POOL_TILE, POOL_TILE), lambda w, b: (w, 0, 0)),
                  pl.BlockSpec((1, n, 1), lambda w, b: (w, 0, 0)),
                  pl.BlockSpec((1, ch, ch), lambda w, b: (w, 0, 0)),
                  pl.BlockSpec((1, 1, ch), lambda w, b: (w, 0, 0))],
        out_specs=pl.BlockSpec((1, n, ch), lambda w, b: (b, 0, w)),
        scratch_shapes=[pltpu.VMEM((n + 2 * pad, ch), F32)],
        compiler_params=_cparams(("arbitrary", "arbitrary")),
        name="pool",
    )(usp, pc, inv, pool_w, pool_scale)


def _merge_kernel(a_ref, p_ref, wa_ref, wb_ref, ga_ref, gb_ref, o_ref):
    ya = _dot(a_ref[0], wa_ref[...])
    yb = _dot(p_ref[0], wb_ref[...])
    m = jax.nn.sigmoid(ga_ref[0].astype(F32)) * ya + jax.nn.sigmoid(gb_ref[0].astype(F32)) * yb
    o_ref[0] = m.astype(BF16)


def _merge(ag, yp, wa, wb, gates, *, tm, tn):
    bsz, s, ka = ag.shape
    kb = yp.shape[2]
    d = wa.shape[1]
    tm, tn = _tile(s, tm), _tile(d, tn)
    nj = d // tn
    return pl.pallas_call(
        _merge_kernel,
        out_shape=jax.ShapeDtypeStruct((bsz, s, d), BF16),
        grid=(bsz, s // tm, nj),
        in_specs=[pl.BlockSpec((1, tm, ka), lambda b, i, j: (b, i, 0)),
                  pl.BlockSpec((1, tm, kb), lambda b, i, j: (b, i, 0)),
                  pl.BlockSpec((ka, tn), lambda b, i, j: (0, j)),
                  pl.BlockSpec((kb, tn), lambda b, i, j: (0, j)),
                  pl.BlockSpec((1, tm, tn), lambda b, i, j: (b, i, j)),
                  pl.BlockSpec((1, tm, tn), lambda b, i, j: (b, i, nj + j))],
        out_specs=pl.BlockSpec((1, tm, tn), lambda b, i, j: (b, i, j)),
        compiler_params=_cparams(("arbitrary", "arbitrary", "arbitrary")),
        name="merge",
    )(ag, yp, wa, wb, gates, gates)


def kernel(x, c, ctx, c_ctx, w_mod, b_mod, norm_g, final_g, ffn1_w_in, ffn1_w_out, ffn2_w_in, ffn2_w_out,
           w_in, ssm_lambda_re, ssm_lambda_im, ssm_log_step, ssm_b_re, ssm_b_im, ssm_c_re, ssm_c_im, ssm_d,
           w_glu, w_branch_a, pool_w, pool_scale, w_branch_b, w_out):
    bsz, seq, d = x.shape
    assert w_mod.shape[0] == 1, "single-layer problem"
    ssm_w = ssm_d.shape[1]
    pool_width = pool_scale.shape[1]
    nw = len(POOL_WINDOWS)
    pch = pool_width // nw
    p = ssm_lambda_re.shape[-1]
    assert seq % POOL_TILE == 0 and ctx.shape[1] % SSM_CHUNK == 0 and ssm_w % LANES == 0

    f1_in, f1_out = ffn1_w_in[0].astype(BF16), ffn1_w_out[0].astype(BF16)
    f2_in, f2_out = ffn2_w_in[0].astype(BF16), ffn2_w_out[0].astype(BF16)
    w_sp = w_in[0][:, :ssm_w + pool_width].astype(BF16)
    w_s = w_in[0][:, :ssm_w].astype(BF16)
    w_gates = w_in[0][:, ssm_w + pool_width:].astype(BF16)
    wglu = w_glu[0].astype(BF16)
    wba, wbb = w_branch_a[0].astype(BF16), w_branch_b[0].astype(BF16)
    wo = w_out[0].astype(BF16)
    pw = pool_w[0].astype(BF16)
    psc = pool_scale[0].reshape(nw, 1, pch)
    ng = norm_g[0]
    fg = final_g.reshape(1, d)

    rows = -(-(bsz + 1) // 8) * 8
    c8 = jnp.zeros((rows, d), F32).at[:bsz].set(c).at[bsz].set(c_ctx)
    mods = _mod(c8, w_mod[0], b_mod, tn=512 if w_mod.shape[2] % 512 == 0 else w_mod.shape[2]).reshape(rows, N_MOD, d)
    lat = lambda b: b
    con = lambda b: bsz

    tables = _ssm_block_tables(*_ssm_tables(ssm_lambda_re[0], ssm_lambda_im[0], ssm_log_step[0],
                                            ssm_b_re[0], ssm_b_im[0], ssm_c_re[0], ssm_c_im[0]))

    nctx = ctx.shape[1]
    _, uc = _ffn(ctx.reshape(1, bsz * nctx, d), ng, mods, con, f1_in, f1_out, fg, gi=0, mi=0, tm=1024, tf=256,
                 next_pre=(1, 3))
    us_c = _mm(uc, w_s, F32, tm=1024, tn=512).reshape(bsz, nctx, ssm_w)
    h_ctx, = _ssm(us_c, tables, jnp.zeros((bsz, ssm_w // LANES, 1, 4 * p * SSM_GPB), F32), ssm_d, want_y=False)

    x1, u = _ffn(x, ng, mods, lat, f1_in, f1_out, fg, gi=0, mi=0, tm=1024, tf=256, next_pre=(1, 3))
    usp = _mm(u, w_sp, F32, tm=1024, tn=512)
    gates = _mm(u, w_gates, BF16, tm=1024, tn=512)

    z, _ = _ssm(usp, tables, h_ctx, ssm_d, want_y=True)
    ag = _glu(z, wglu, tm=512)

    pc, inv = _pool_consts(seq)
    yp = _pool(usp, ssm_w, pw, psc, pc, inv)

    merged = _merge(ag, yp, wba, wbb, gates, tm=1024, tn=512)
    x2 = _outproj(merged, wo, x1, mods, lat, mi=5, tm=1024, tn=512)
    out, = _ffn(x2, ng, mods, lat, f2_in, f2_out, fg, gi=2, mi=6, tm=1024, tf=256, final_norm=True)
    return out
```

```python
import functools
import math

import jax
import jax.numpy as jnp
import numpy as np
from jax import lax
from jax.experimental import pallas as pl
from jax.experimental.pallas import tpu as pltpu

BF16 = jnp.bfloat16
F32 = jnp.float32

RMS_EPS = 1e-6
LAMBDA_RE_MAX = -1e-4
HALF = 0.5
N_MOD = 9
SSM_GROUP = 16
POOL_WINDOWS = (2, 4, 8, 16)
GRID_W = 64
SSM_CHUNK = 16

LANES = 128
BF16_ROWS = 16
V7X_VMEM_BYTES = 64 * 1024 * 1024
VMEM_LIMIT = 60 * 1024 * 1024


def _cparams(sem):
    return pltpu.CompilerParams(dimension_semantics=sem, vmem_limit_bytes=VMEM_LIMIT)


def _tile(n, pref):
    t = min(n, pref)
    while n % t:
        t //= 2
    return t


ROW_UNROLL = 1
ROW_CHUNK = 64


def _for_row_chunks(n, fn):
    def body(i, carry):
        fn(pl.multiple_of(i * ROW_CHUNK, ROW_CHUNK))
        return carry

    lax.fori_loop(0, n // ROW_CHUNK, body, 0, unroll=ROW_UNROLL)


def _dot(a, b):
    return jnp.dot(a, b, preferred_element_type=F32)


def _rms_mod(x, g, shift, scale):
    xn = x * lax.rsqrt(jnp.mean(x * x, axis=-1, keepdims=True) + RMS_EPS)
    return (xn * g) * (1.0 + scale) + shift


def _mod_kernel(c_ref, w_ref, b_ref, o_ref):
    c = c_ref[...]
    s = (c * jax.nn.sigmoid(c)).astype(BF16)
    o_ref[...] = _dot(s, w_ref[...].astype(BF16)) + b_ref[...]


def _mod(c8, w_mod, b_mod, tn):
    d, n = w_mod.shape
    return pl.pallas_call(
        _mod_kernel,
        out_shape=jax.ShapeDtypeStruct((c8.shape[0], n), F32),
        grid=(n // tn,),
        in_specs=[pl.BlockSpec(c8.shape, lambda j: (0, 0)),
                  pl.BlockSpec((d, tn), lambda j: (0, j)),
                  pl.BlockSpec((1, tn), lambda j: (0, j))],
        out_specs=pl.BlockSpec((c8.shape[0], tn), lambda j: (0, j)),
        compiler_params=_cparams(("arbitrary",)),
        name="mod",
    )(c8, w_mod, b_mod)


FFN_STAGE_ROWS = 128


def _ffn_kernel(x_ref, g_ref, mod_ref, wg_ref, wu_ref, wo_ref, fg_ref, *refs,
                gi, mi, nf, nstage, final_norm, next_pre, cast_blocks):
    nc = len(cast_blocks)
    src_refs, refs = refs[:nc], refs[nc:]
    if next_pre is None:
        o_ref, refs = refs[0], refs[1:]
    else:
        o_ref, u_ref, refs = refs[0], refs[1], refs[2:]
    dst_refs, (pre_ref, acc_ref) = refs[:nc], refs[nc:]
    f = pl.program_id(2)
    rc = x_ref.shape[1]

    @pl.when(f < nstage)
    def _():
        base = pl.multiple_of(f * rc, rc)

        def rows(r0):
            pre = _rms_mod(x_ref[0, pl.ds(r0, ROW_CHUNK), :], g_ref[pl.ds(gi, 1), :],
                           mod_ref[0, pl.ds(mi, 1), :], mod_ref[0, pl.ds(mi + 1, 1), :])
            pre_ref[pl.ds(base + r0, ROW_CHUNK), :] = pre.astype(BF16)
            acc_ref[pl.ds(base + r0, ROW_CHUNK), :] = jnp.zeros((ROW_CHUNK, acc_ref.shape[1]), F32)

        _for_row_chunks(rc, rows)

    @pl.when((f >= nstage) & (f < nstage + nf))
    def _():
        p = pre_ref[...]
        gate = _dot(p, wg_ref[...])
        up = _dot(p, wu_ref[...])
        act = (gate * jax.nn.sigmoid(gate) * up).astype(BF16)
        acc_ref[...] += _dot(act, wo_ref[...])
        step = (pl.program_id(0) * pl.num_programs(1) + pl.program_id(1)) * nf + (f - nstage)
        for src, dst, nblk in zip(src_refs, dst_refs, cast_blocks):
            @pl.when(step < nblk)
            def _(src=src, dst=dst):
                dst[...] = src[...].astype(BF16)

    @pl.when(f >= nstage + nf)
    def _():
        base = pl.multiple_of((f - (nstage + nf)) * rc, rc)

        def rows(r0):
            r = pl.ds(r0, ROW_CHUNK)
            y = x_ref[0, r, :] + (HALF * mod_ref[0, pl.ds(mi + 2, 1), :]) * acc_ref[pl.ds(base + r0, ROW_CHUNK), :]
            if final_norm:
                y = y * lax.rsqrt(jnp.mean(y * y, axis=-1, keepdims=True) + RMS_EPS) * fg_ref[...]
            o_ref[0, r, :] = y
            if next_pre is not None:
                gi2, mi2 = next_pre
                u_ref[0, r, :] = _rms_mod(y, g_ref[pl.ds(gi2, 1), :], mod_ref[0, pl.ds(mi2, 1), :],
                                          mod_ref[0, pl.ds(mi2 + 1, 1), :]).astype(BF16)

        _for_row_chunks(rc, rows)


def _cast_plan(arr, nsteps):
    r, c = arr.shape
    for parts in (8, 4, 2, 1):
        if c % (parts * LANES):
            continue
        br = BF16_ROWS
        while br <= r:
            if r % br == 0 and (r // br) * parts <= nsteps:
                return br, c // parts
            br *= 2
    return None


def _ffn(x, norm_g, mods, mod_row, w_in, w_out, final_g, *, gi, mi, tm, tf, final_norm=False, next_pre=None,
         casts=()):
    bsz, s, d = x.shape
    ff = w_out.shape[0]
    nf = ff // tf
    tm = _tile(s, tm)
    rc = _tile(tm, FFN_STAGE_ROWS)
    nstage = tm // rc
    ni = s // tm

    def hid(f):
        return jnp.clip(f - nstage, 0, nf - 1)

    cast_blocks, cast_in, cast_out, cast_shapes = [], [], [], []
    for arr, (br, bc) in casts:
        nrb, ncb = arr.shape[0] // br, arr.shape[1] // bc
        assert nrb * br == arr.shape[0] and ncb * bc == arr.shape[1] and nrb * ncb <= bsz * ni * nf

        def blk(b, i, f, nrb=nrb, ncb=ncb):
            step = jnp.minimum((b * ni + i) * nf + hid(f), nrb * ncb - 1)
            return step // ncb, step % ncb

        cast_blocks.append(nrb * ncb)
        cast_in.append(pl.BlockSpec((br, bc), blk))
        cast_out.append(pl.BlockSpec((br, bc), blk))
        cast_shapes.append(jax.ShapeDtypeStruct(arr.shape, BF16))
    kern = functools.partial(_ffn_kernel, gi=gi, mi=mi, nf=nf, nstage=nstage, final_norm=final_norm,
                             next_pre=next_pre, cast_blocks=tuple(cast_blocks))

    def x_blk(b, i, f):
        return b, i * nstage + jnp.where(f < nstage, f, jnp.clip(f - (nstage + nf), 0, nstage - 1)), 0

    def o_blk(b, i, f):
        return b, i * nstage + jnp.clip(f - (nstage + nf), 0, nstage - 1), 0

    out_shape = [jax.ShapeDtypeStruct((bsz, s, d), F32)]
    out_specs = [pl.BlockSpec((1, rc, d), o_blk)]
    if next_pre is not None:
        out_shape.append(jax.ShapeDtypeStruct((bsz, s, d), BF16))
        out_specs.append(pl.BlockSpec((1, rc, d), o_blk))
    return pl.pallas_call(
        kern,
        out_shape=out_shape + cast_shapes,
        grid=(bsz, ni, nf + 2 * nstage),
        in_specs=[pl.BlockSpec((1, rc, d), x_blk),
                  pl.BlockSpec(norm_g.shape, lambda b, i, f: (0, 0)),
                  pl.BlockSpec((1, N_MOD, d), lambda b, i, f: (mod_row(b), 0, 0)),
                  pl.BlockSpec((d, tf), lambda b, i, f: (0, hid(f))),
                  pl.BlockSpec((d, tf), lambda b, i, f: (0, nf + hid(f))),
                  pl.BlockSpec((tf, d), lambda b, i, f: (hid(f), 0)),
                  pl.BlockSpec((1, d), lambda b, i, f: (0, 0))] + cast_in,
        out_specs=out_specs + cast_out,
        scratch_shapes=[pltpu.VMEM((tm, d), BF16), pltpu.VMEM((tm, d), F32)],
        compiler_params=_cparams(("arbitrary", "arbitrary", "arbitrary")),
        name="ffn",
    )(x, norm_g, mods, w_in, w_in, w_out, final_g, *[arr for arr, _ in casts])


def _mm_kernel(a_ref, b_ref, o_ref):
    o_ref[0] = _dot(a_ref[0], b_ref[...]).astype(o_ref.dtype)


def _mm(a, w, out_dtype, *, tm, tn, col0=0, ncols=None):
    bsz, s, k = a.shape
    n = w.shape[1] - col0 if ncols is None else ncols
    tm, tn = _tile(s, tm), _tile(math.gcd(n, col0) if col0 else n, tn)
    j0 = col0 // tn
    return pl.pallas_call(
        _mm_kernel,
        out_shape=jax.ShapeDtypeStruct((bsz, s, n), out_dtype),
        grid=(bsz, s // tm, n // tn),
        in_specs=[pl.BlockSpec((1, tm, k), lambda b, i, j: (b, i, 0)),
                  pl.BlockSpec((k, tn), lambda b, i, j: (0, j0 + j))],
        out_specs=pl.BlockSpec((1, tm, tn), lambda b, i, j: (b, i, j)),
        compiler_params=_cparams(("arbitrary", "arbitrary", "arbitrary")),
        name="mm",
    )(a, w)


def _outproj_kernel(a_ref, w_ref, x_ref, mod_ref, o_ref, *, mi):
    o_ref[0] = x_ref[0] + mod_ref[0, pl.ds(mi, 1), :] * _dot(a_ref[0], w_ref[...])


def _outproj(a, w, x, mods, mod_row, *, mi, tm, tn):
    bsz, s, k = a.shape
    n = w.shape[1]
    tm, tn = _tile(s, tm), _tile(n, tn)
    return pl.pallas_call(
        functools.partial(_outproj_kernel, mi=mi),
        out_shape=jax.ShapeDtypeStruct((bsz, s, n), F32),
        grid=(bsz, s // tm, n // tn),
        in_specs=[pl.BlockSpec((1, tm, k), lambda b, i, j: (b, i, 0)),
                  pl.BlockSpec((k, tn), lambda b, i, j: (0, j)),
                  pl.BlockSpec((1, tm, tn), lambda b, i, j: (b, i, j)),
                  pl.BlockSpec((1, N_MOD, tn), lambda b, i, j: (mod_row(b), 0, j))],
        out_specs=pl.BlockSpec((1, tm, tn), lambda b, i, j: (b, i, j)),
        compiler_params=_cparams(("arbitrary", "arbitrary", "arbitrary")),
        name="outproj",
    )(a, w, x, mods)


def _ssm_tables(lam_re, lam_im, log_step, b_re, b_im, c_re, c_im):
    t = SSM_CHUNK
    hp = lax.Precision.HIGHEST
    lr = jnp.minimum(lam_re.astype(F32), LAMBDA_RE_MAX)
    li = lam_im.astype(F32)
    step = jnp.exp(log_step.astype(F32))[..., None]
    m = jnp.arange(t + 1, dtype=F32)[:, None, None, None]
    mag = jnp.exp(m * (lr * step)[None])
    ang = m * (li * step)[None]
    pw_re, pw_im = mag * jnp.cos(ang), mag * jnp.sin(ang)
    nr, ni = pw_re[1] - 1.0, pw_im[1]
    den = lr * lr + li * li
    q_re, q_im = (nr * lr + ni * li) / den, (ni * lr - nr * li) / den
    bb_re = q_re[..., None] * b_re - q_im[..., None] * b_im
    bb_im = q_re[..., None] * b_im + q_im[..., None] * b_re
    cr, ci = c_re.astype(F32), c_im.astype(F32)

    cl_re = cr[:, :, None] * pw_re[:t].transpose(1, 2, 0, 3)[:, :, :, None, :] \
        - ci[:, :, None] * pw_im[:t].transpose(1, 2, 0, 3)[:, :, :, None, :]
    cl_im = cr[:, :, None] * pw_im[:t].transpose(1, 2, 0, 3)[:, :, :, None, :] \
        + ci[:, :, None] * pw_re[:t].transpose(1, 2, 0, 3)[:, :, :, None, :]
    kern = jnp.einsum('dgtkp,dgpq->dgtkq', cl_re, bb_re, precision=hp) \
        - jnp.einsum('dgtkp,dgpq->dgtkq', cl_im, bb_im, precision=hp)
    g, kk = kern.shape[1], kern.shape[3]
    zero = (jnp.arange(t) == 0).astype(F32)[None, :, None, None]
    kpos = kern[0] + zero * kern[1]
    kneg = kern[1] + zero * kern[0]
    rpos = kpos.transpose(0, 3, 1, 2).reshape(g, kk, t * kk)
    rneg = kneg.transpose(0, 3, 1, 2).reshape(g, kk, t * kk)

    pf_re, pf_im = pw_re[:t][::-1, 0], pw_im[:t][::-1, 0]
    pb_re, pb_im = pw_re[:t, 1], pw_im[:t, 1]

    def st(p_re, p_im, d):
        re = p_re[:, :, :, None] * bb_re[d][None] - p_im[:, :, :, None] * bb_im[d][None]
        im = p_re[:, :, :, None] * bb_im[d][None] + p_im[:, :, :, None] * bb_re[d][None]
        return re.transpose(1, 0, 3, 2), im.transpose(1, 0, 3, 2)

    wf_re, wf_im = st(pf_re, pf_im, 0)
    wb_re, wb_im = st(pb_re, pb_im, 1)
    p = lr.shape[-1]
    wst = jnp.concatenate([wf_re, wb_re, wf_im, wb_im], axis=-1).reshape(g, t * kk, 4 * p)

    vf_pw_re, vf_pw_im = pw_re[1:, 0], pw_im[1:, 0]
    vb_pw_re, vb_pw_im = pw_re[1:, 1][::-1], pw_im[1:, 1][::-1]

    def rd(p_re, p_im, d):
        d_re = cr[d][None] * p_re[:, :, None, :] - ci[d][None] * p_im[:, :, None, :]
        d_im = cr[d][None] * p_im[:, :, None, :] + ci[d][None] * p_re[:, :, None, :]
        return d_re.transpose(1, 3, 0, 2), -d_im.transpose(1, 3, 0, 2)

    vf_re, vf_im = rd(vf_pw_re, vf_pw_im, 0)
    vb_re, vb_im = rd(vb_pw_re, vb_pw_im, 1)
    vrd = jnp.concatenate([vf_re, vb_re, vf_im, vb_im], axis=1).reshape(g, 4 * p, t * kk)

    a_re = jnp.concatenate([pw_re[t, 0], pw_re[t, 1]], axis=-1)[:, None, :]
    a_im = jnp.concatenate([pw_im[t, 0], pw_im[t, 1]], axis=-1)[:, None, :]
    return rpos.astype(BF16), rneg.astype(BF16), wst.astype(BF16), vrd.astype(BF16), a_re, a_im


SSM_GPB = LANES // SSM_GROUP


def _ssm_block_tables(rpos, rneg, wst, vrd, a_re, a_im):
    rpos, rneg, wst, vrd = lax.optimization_barrier((rpos, rneg, wst, vrd))
    g, _, tk = rpos.shape
    t, k, gpb = SSM_CHUNK, SSM_GROUP, SSM_GPB
    nb = g // gpb
    p2 = wst.shape[2] // 2
    tl, hw = t * gpb * k, gpb * p2
    rpos = rpos.reshape(nb, gpb * k, tk)
    rneg = rneg.reshape(nb, gpb * k, tk)
    rw = wst.reshape(nb, gpb, t, k, 2 * p2).transpose(0, 2, 1, 3, 4).reshape(nb, t, gpb * k, 2 * p2)
    rv = vrd.reshape(nb, gpb, 2, p2, tk).transpose(0, 2, 1, 3, 4).reshape(nb, 2, hw, tk)
    jj, kk = np.arange(tk) // k, np.arange(tk) % k
    sel = (jj[None, :, None] == np.arange(t)[:, None, None]) & (kk[None, :, None] == (np.arange(LANES) % k)[None, None, :])
    sel = jnp.asarray(sel, BF16)
    toep_b, wst_b, vrd_b = pl.pallas_call(
        functools.partial(_ssm_expand_kernel, t=t, k=k, p2=p2),
        out_shape=[jax.ShapeDtypeStruct((nb, tl, tl), BF16), jax.ShapeDtypeStruct((nb, tl, 2 * hw), BF16),
                   jax.ShapeDtypeStruct((nb, 2 * hw, tl), BF16)],
        grid=(nb, t),
        in_specs=[pl.BlockSpec((1, gpb * k, tk), lambda b, i: (b, 0, 0)),
                  pl.BlockSpec((1, gpb * k, tk), lambda b, i: (b, 0, 0)),
                  pl.BlockSpec((1, 1, gpb * k, 2 * p2), lambda b, i: (b, i, 0, 0)),
                  pl.BlockSpec((1, 2, hw, tk), lambda b, i: (b, 0, 0, 0)),
                  pl.BlockSpec(sel.shape, lambda b, i: (0, 0, 0))],
        out_specs=[pl.BlockSpec((1, gpb * k, tl), lambda b, i: (b, i, 0)),
                   pl.BlockSpec((1, gpb * k, 2 * hw), lambda b, i: (b, i, 0)),
                   pl.BlockSpec((1, 2 * hw, gpb * k), lambda b, i: (b, 0, i))],
        compiler_params=_cparams(("arbitrary", "arbitrary")),
        name="ssm_expand",
    )(rpos, rneg, rw, rv, sel)
    return toep_b, wst_b, vrd_b, a_re.reshape(nb, 1, hw), a_im.reshape(nb, 1, hw)


def _ssm_expand_kernel(rpos_ref, rneg_ref, rw_ref, rv_ref, sel_ref, toep_ref, wst_ref, vrd_ref, *, t, k, p2):
    i = pl.program_id(1)
    lb = rpos_ref.shape[1]
    hw = rv_ref.shape[2]
    gpb = lb // k
    same_g = (lax.broadcasted_iota(jnp.int32, (lb, lb), 0) // k) == (lax.broadcasted_iota(jnp.int32, (lb, lb), 1) // k)
    for j in range(t):
        pos = _dot(rpos_ref[0], sel_ref[jnp.clip(j - i, 0, t - 1)])
        neg = _dot(rneg_ref[0], sel_ref[jnp.clip(i - j, 0, t - 1)])
        blk = jnp.where(i <= j, pos, neg)
        toep_ref[0, :, j * lb:(j + 1) * lb] = jnp.where(same_g, blk, 0.0).astype(BF16)
    w = rw_ref[0, 0]
    same_gw = (lax.broadcasted_iota(jnp.int32, (lb, hw), 0) // k) == (lax.broadcasted_iota(jnp.int32, (lb, hw), 1) // p2)
    for r in range(2):
        wr = jnp.concatenate([w[:, r * p2:(r + 1) * p2]] * gpb, axis=-1)
        wst_ref[0, :, r * hw:(r + 1) * hw] = jnp.where(same_gw, wr, jnp.zeros_like(wr))
    same_gv = (lax.broadcasted_iota(jnp.int32, (hw, lb), 0) // p2) == (lax.broadcasted_iota(jnp.int32, (hw, lb), 1) // k)
    for r in range(2):
        v = _dot(rv_ref[0, r], sel_ref[i])
        vrd_ref[0, r * hw:(r + 1) * hw, :] = jnp.where(same_gv, v, 0.0).astype(BF16)


def _ssm_kernel(u_ref, ws_ref, ar_ref, ai_ref, h0_ref, *refs, nc, want_y):
    t = SSM_CHUNK
    if want_y:
        wt_ref, v_ref, d_ref, z_ref, hfin_ref, s_ref, yi_ref, hp_ref, tmp_ref, y_ref = refs
    else:
        hfin_ref, s_ref = refs
    a = jnp.concatenate([u_ref[0, pl.ds(j, nc, stride=t), :] for j in range(t)], axis=-1).astype(BF16)
    s_ref[...] = _dot(a, ws_ref[0])
    if want_y:
        yi_ref[...] = _dot(a, wt_ref[0])

    hw = ar_ref.shape[2]
    ar = ar_ref[0]
    ai = ai_ref[0]
    p2 = hw // SSM_GPB
    fwd = lax.broadcasted_iota(jnp.int32, (1, hw), 1) % p2 < (p2 // 2)

    def body(c, carry):
        h_re, h_im = carry
        cb = nc - 1 - c
        if want_y:
            h = jnp.concatenate([h_re, h_im], axis=-1)
            hp_ref[pl.ds(c, 1), :] = h
            tmp_ref[pl.ds(cb, 1), :] = h
        s_f = s_ref[pl.ds(c, 1), :]
        s_b = s_ref[pl.ds(cb, 1), :]
        s_re = jnp.where(fwd, s_f[:, :hw], s_b[:, :hw])
        s_im = jnp.where(fwd, s_f[:, hw:], s_b[:, hw:])
        return ar * h_re - ai * h_im + s_re, ar * h_im + ai * h_re + s_im

    h0 = h0_ref[0, 0]
    h_re, h_im = lax.fori_loop(0, nc, body, (h0[:, :hw], h0[:, hw:]))
    hfin_ref[0, 0] = jnp.concatenate([h_re, h_im], axis=-1)

    if want_y:
        fwd2 = lax.broadcasted_iota(jnp.int32, hp_ref.shape, 1) % p2 < (p2 // 2)
        hp = jnp.where(fwd2, hp_ref[...], tmp_ref[...]).astype(BF16)
        y = yi_ref[...] + _dot(hp, v_ref[0])
        for j in range(t):
            y_ref[pl.ds(j, nc, stride=t), :] = y[:, j * LANES:(j + 1) * LANES]

        def rows(r0):
            r = pl.ds(r0, ROW_CHUNK)
            z_ref[0, r, :] = jax.nn.gelu(y_ref[r, :] + d_ref[...] * u_ref[0, r, :]).astype(BF16)

        _for_row_chunks(nc * t, rows)


def _ssm(u, tables, h0, d, *, want_y):
    toep_b, wst_b, vrd_b, ar, ai = tables
    bsz, n, _ = u.shape
    nb, tl, sw = wst_b.shape
    nc = n // SSM_CHUNK
    single = pl.Buffered(1)
    shapes = [jax.ShapeDtypeStruct((bsz, nb, 1, sw), F32)]
    specs = [pl.BlockSpec((1, 1, 1, sw), lambda k, b: (b, k, 0, 0))]
    scratch = [pltpu.VMEM((nc, sw), F32)]
    in_specs = [pl.BlockSpec((1, n, LANES), lambda k, b: (b, 0, k)),
                pl.BlockSpec((1, tl, sw), lambda k, b: (k, 0, 0), pipeline_mode=single),
                pl.BlockSpec((1, 1, sw // 2), lambda k, b: (k, 0, 0)),
                pl.BlockSpec((1, 1, sw // 2), lambda k, b: (k, 0, 0)),
                pl.BlockSpec((1, 1, 1, sw), lambda k, b: (b, k, 0, 0))]
    args = [u, wst_b, ar, ai, h0]
    if want_y:
        in_specs += [pl.BlockSpec((1, tl, tl), lambda k, b: (k, 0, 0), pipeline_mode=single),
                     pl.BlockSpec((1, sw, tl), lambda k, b: (k, 0, 0), pipeline_mode=single),
                     pl.BlockSpec((1, LANES), lambda k, b: (0, k))]
        args += [toep_b, vrd_b, d]
        shapes.insert(0, jax.ShapeDtypeStruct((bsz, n, nb * LANES), BF16))
        specs.insert(0, pl.BlockSpec((1, n, LANES), lambda k, b: (b, 0, k)))
        scratch += [pltpu.VMEM((nc, tl), F32), pltpu.VMEM((nc, sw), F32), pltpu.VMEM((nc, sw), F32),
                    pltpu.VMEM((n, LANES), F32)]
    return pl.pallas_call(
        functools.partial(_ssm_kernel, nc=nc, want_y=want_y),
        out_shape=shapes,
        grid=(nb, bsz),
        in_specs=in_specs,
        out_specs=specs,
        scratch_shapes=scratch,
        compiler_params=_cparams(("arbitrary", "arbitrary")),
        name="ssm",
    )(*args)


def _glu_kernel(z_ref, w_ref, o_ref, *, ws):
    z = z_ref[0]
    a = _dot(z, w_ref[:, :ws])
    b = _dot(z, w_ref[:, ws:])
    o_ref[0] = (a * jax.nn.sigmoid(b)).astype(BF16)


def _glu(z, w_glu, *, tm):
    bsz, s, ws = z.shape
    tm = _tile(s, tm)
    return pl.pallas_call(
        functools.partial(_glu_kernel, ws=ws),
        out_shape=jax.ShapeDtypeStruct((bsz, s, ws), BF16),
        grid=(bsz, s // tm),
        in_specs=[pl.BlockSpec((1, tm, ws), lambda b, i: (b, i, 0)),
                  pl.BlockSpec(w_glu.shape, lambda b, i: (0, 0))],
        out_specs=pl.BlockSpec((1, tm, ws), lambda b, i: (b, i, 0)),
        compiler_params=_cparams(("arbitrary", "arbitrary")),
        name="glu",
    )(z, w_glu)


POOL_TILE = 4 * GRID_W


def _pool_consts(n):
    t = np.arange(POOL_TILE)
    pcs, invs = [], []
    r = np.arange(n) // GRID_W
    c = np.arange(n) % GRID_W
    rows = n // GRID_W
    for w in POOL_WINDOWS:
        lo, hi = w // 2, w - w // 2
        same_row = (t[:, None] // GRID_W) == (t[None, :] // GRID_W)
        dc = (t[None, :] % GRID_W) - (t[:, None] % GRID_W)
        pcs.append((same_row & (dc >= -lo) & (dc < hi)).astype(np.float32))
        cnt_r = np.minimum(r + hi, rows) - np.maximum(r - lo, 0)
        cnt_c = np.minimum(c + hi, GRID_W) - np.maximum(c - lo, 0)
        invs.append((1.0 / (cnt_r * cnt_c)).astype(np.float32)[:, None])
    return jnp.asarray(np.stack(pcs), BF16), jnp.asarray(np.stack(invs), F32)


def _pool_kernel(v_ref, pc_ref, inv_ref, pw_ref, sc_ref, o_ref, cs_ref, *, n, pad):
    wi = pl.program_id(0)
    nt = n // POOL_TILE
    ch = v_ref.shape[2]
    zeros = jnp.zeros((pad, ch), F32)
    cs_ref[pl.ds(0, pad), :] = zeros
    cs_ref[pl.ds(pad + n, pad), :] = zeros
    pc = pc_ref[0]

    def col_body(i, _):
        off = pl.multiple_of(i * POOL_TILE, POOL_TILE)
        v = v_ref[0, pl.ds(off, POOL_TILE), :]
        hi = v.astype(BF16)
        lo = (v - hi.astype(F32)).astype(BF16)
        cs_ref[pl.ds(pad + off, POOL_TILE), :] = _dot(pc, hi) + _dot(pc, lo)
        return 0

    lax.fori_loop(0, nt, col_body, 0)

    for k, w in enumerate(POOL_WINDOWS):
        @pl.when(wi == k)
        def _(w=w):
            def row_body(i, _):
                off = pl.multiple_of(i * POOL_TILE, POOL_TILE)
                acc = cs_ref[pl.ds(pad + off - (w // 2) * GRID_W, POOL_TILE), :]
                for dlt in range(-(w // 2) + 1, w - w // 2):
                    acc = acc + cs_ref[pl.ds(pad + off + dlt * GRID_W, POOL_TILE), :]
                mixed = acc * inv_ref[0, pl.ds(off, POOL_TILE), :] - v_ref[0, pl.ds(off, POOL_TILE), :]
                y = _dot(mixed.astype(BF16), pw_ref[0]) * sc_ref[0]
                o_ref[0, pl.ds(off, POOL_TILE), :] = y.astype(BF16)
                return 0

            lax.fori_loop(0, nt, row_body, 0)


def _pool(usp, col0, pool_w, pool_scale, pc, inv):
    bsz, n, _ = usp.shape
    nw, ch, _ = pool_w.shape
    pad = (max(POOL_WINDOWS) // 2) * GRID_W
    cb0 = col0 // ch
    return pl.pallas_call(
        functools.partial(_pool_kernel, n=n, pad=pad),
        out_shape=jax.ShapeDtypeStruct((bsz, n, nw * ch), BF16),
        grid=(nw, bsz),
        in_specs=[pl.BlockSpec((1, n, ch), lambda w, b: (b, 0, cb0 + w)),
                  pl.BlockSpec((1, POOL_TILE, POOL_TILE), lambda w, b: (w, 0, 0)),
                  pl.BlockSpec((1, n, 1), lambda w, b: (w, 0, 0)),
                  pl.BlockSpec((1, ch, ch), lambda w, b: (w, 0, 0)),
                  pl.BlockSpec((1, 1, ch), lambda w, b: (w, 0, 0))],
        out_specs=pl.BlockSpec((1, n, ch), lambda w, b: (b, 0, w)),
        scratch_shapes=[pltpu.VMEM((n + 2 * pad, ch), F32)],
        compiler_params=_cparams(("arbitrary", "arbitrary")),
        name="pool",
    )(usp, pc, inv, pool_w, pool_scale)


def _merge_kernel(a_ref, p_ref, wa_ref, wb_ref, ga_ref, gb_ref, o_ref):
    ya = _dot(a_ref[0], wa_ref[...])
    yb = _dot(p_ref[0], wb_ref[...])
    m = jax.nn.sigmoid(ga_ref[0].astype(F32)) * ya + jax.nn.sigmoid(gb_ref[0].astype(F32)) * yb
    o_ref[0] = m.astype(BF16)


def _merge(ag, yp, wa, wb, gates, *, tm, tn):
    bsz, s, ka = ag.shape
    kb = yp.shape[2]
    d = wa.shape[1]
    tm, tn = _tile(s, tm), _tile(d, tn)
    nj = d // tn
    return pl.pallas_call(
        _merge_kernel,
        out_shape=jax.ShapeDtypeStruct((bsz, s, d), BF16),
        grid=(bsz, s // tm, nj),
        in_specs=[pl.BlockSpec((1, tm, ka), lambda b, i, j: (b, i, 0)),
                  pl.BlockSpec((1, tm, kb), lambda b, i, j: (b, i, 0)),
                  pl.BlockSpec((ka, tn), lambda b, i, j: (0, j)),
                  pl.BlockSpec((kb, tn), lambda b, i, j: (0, j)),
                  pl.BlockSpec((1, tm, tn), lambda b, i, j: (b, i, j)),
                  pl.BlockSpec((1, tm, tn), lambda b, i, j: (b, i, nj + j))],
        out_specs=pl.BlockSpec((1, tm, tn), lambda b, i, j: (b, i, j)),
        compiler_params=_cparams(("arbitrary", "arbitrary", "arbitrary")),
        name="merge",
    )(ag, yp, wa, wb, gates, gates)


def kernel(x, c, ctx, c_ctx, w_mod, b_mod, norm_g, final_g, ffn1_w_in, ffn1_w_out, ffn2_w_in, ffn2_w_out,
           w_in, ssm_lambda_re, ssm_lambda_im, ssm_log_step, ssm_b_re, ssm_b_im, ssm_c_re, ssm_c_im, ssm_d,
           w_glu, w_branch_a, pool_w, pool_scale, w_branch_b, w_out):
    bsz, seq, d = x.shape
    assert w_mod.shape[0] == 1, "single-layer problem"
    ssm_w = ssm_d.shape[1]
    pool_width = pool_scale.shape[1]
    nw = len(POOL_WINDOWS)
    pch = pool_width // nw
    p = ssm_lambda_re.shape[-1]
    assert seq % POOL_TILE == 0 and ctx.shape[1] % SSM_CHUNK == 0 and ssm_w % LANES == 0

    f1_in, f1_out = ffn1_w_in[0].astype(BF16), ffn1_w_out[0].astype(BF16)
    w_s = w_in[0][:, :ssm_w].astype(BF16)
    late = [ffn2_w_in[0], ffn2_w_out[0], w_in[0], w_out[0], w_branch_b[0], w_branch_a[0], w_glu[0],
            pool_w[0].reshape(nw * pch, pch)]
    psc = pool_scale[0].reshape(nw, 1, pch)
    ng = norm_g[0]
    fg = final_g.reshape(1, d)

    rows = -(-(bsz + 1) // 8) * 8
    c8 = jnp.zeros((rows, d), F32).at[:bsz].set(c).at[bsz].set(c_ctx)
    mods = _mod(c8, w_mod[0], b_mod, tn=512 if w_mod.shape[2] % 512 == 0 else w_mod.shape[2]).reshape(rows, N_MOD, d)
    lat = lambda b: b
    con = lambda b: bsz

    tables = _ssm_block_tables(*_ssm_tables(ssm_lambda_re[0], ssm_lambda_im[0], ssm_log_step[0],
                                            ssm_b_re[0], ssm_b_im[0], ssm_c_re[0], ssm_c_im[0]))

    nctx = ctx.shape[1]
    _, uc = _ffn(ctx.reshape(1, bsz * nctx, d), ng, mods, con, f1_in, f1_out, fg, gi=0, mi=0, tm=1024, tf=256,
                 next_pre=(1, 3))
    us_c = _mm(uc, w_s, F32, tm=1024, tn=512).reshape(bsz, nctx, ssm_w)
    h_ctx, = _ssm(us_c, tables, jnp.zeros((bsz, ssm_w // LANES, 1, 4 * p * SSM_GPB), F32), ssm_d, want_y=False)

    tf = 256
    nsteps = bsz * (seq // _tile(seq, 1024)) * (f1_out.shape[0] // tf)
    plans = [_cast_plan(a, nsteps) for a in late]
    x1, u, *cast = _ffn(x, ng, mods, lat, f1_in, f1_out, fg, gi=0, mi=0, tm=1024, tf=tf, next_pre=(1, 3),
                        casts=[(a, pln) for a, pln in zip(late, plans) if pln is not None])
    cast = iter(cast)
    f2_in, f2_out, w_all, wo, wbb, wba, wglu, pw = [
        a.astype(BF16) if pln is None else next(cast) for a, pln in zip(late, plans)]
    pw = pw.reshape(nw, pch, pch)
    usp = _mm(u, w_all, F32, tm=1024, tn=512, ncols=ssm_w + pool_width)
    gates = _mm(u, w_all, BF16, tm=1024, tn=512, col0=ssm_w + pool_width)

    z, _ = _ssm(usp, tables, h_ctx, ssm_d, want_y=True)
    ag = _glu(z, wglu, tm=512)

    pc, inv = _pool_consts(seq)
    yp = _pool(usp, ssm_w, pw, psc, pc, inv)

    merged = _merge(ag, yp, wba, wbb, gates, tm=1024, tn=512)
    x2 = _outproj(merged, wo, x1, mods, lat, mi=5, tm=1024, tn=512)
    out, = _ffn(x2, ng, mods, lat, f2_in, f2_out, fg, gi=2, mi=6, tm=1024, tf=256, final_norm=True)
    return out
```

```python
import functools
import math

import jax
import jax.numpy as jnp
import numpy as np
from jax import lax
from jax.experimental import pallas as pl
from jax.experimental.pallas import tpu as pltpu

BF16 = jnp.bfloat16
F32 = jnp.float32

RMS_EPS = 1e-6
LAMBDA_RE_MAX = -1e-4
HALF = 0.5
N_MOD = 9
SSM_GROUP = 16
POOL_WINDOWS = (2, 4, 8, 16)
GRID_W = 64
SSM_CHUNK = 16

LANES = 128
BF16_ROWS = 16
V7X_VMEM_BYTES = 64 * 1024 * 1024
VMEM_LIMIT = 60 * 1024 * 1024


def _cparams(sem):
    return pltpu.CompilerParams(dimension_semantics=sem, vmem_limit_bytes=VMEM_LIMIT)


def _tile(n, pref):
    t = min(n, pref)
    while n % t:
        t //= 2
    return t


ROW_UNROLL = 1
ROW_CHUNK = 64


def _for_row_chunks(n, fn):
    def body(i, carry):
        fn(pl.multiple_of(i * ROW_CHUNK, ROW_CHUNK))
        return carry

    lax.fori_loop(0, n // ROW_CHUNK, body, 0, unroll=ROW_UNROLL)


NORM_ROWS = 16
NORM_UNROLL = 4


def _for_rows(n, step, fn):
    def body(i, carry):
        fn(pl.multiple_of(i * step, step))
        return carry

    lax.fori_loop(0, n // step, body, 0, unroll=NORM_UNROLL)


def _dot(a, b):
    return jnp.dot(a, b, preferred_element_type=F32)


def _mod_kernel(c_ref, w_ref, b_ref, o_ref):
    c = c_ref[...]
    s = (c * jax.nn.sigmoid(c)).astype(BF16)
    o_ref[...] = _dot(s, w_ref[...].astype(BF16)) + b_ref[...]


def _mod(c8, w_mod, b_mod, tn):
    d, n = w_mod.shape
    return pl.pallas_call(
        _mod_kernel,
        out_shape=jax.ShapeDtypeStruct((c8.shape[0], n), F32),
        grid=(n // tn,),
        in_specs=[pl.BlockSpec(c8.shape, lambda j: (0, 0)),
                  pl.BlockSpec((d, tn), lambda j: (0, j)),
                  pl.BlockSpec((1, tn), lambda j: (0, j))],
        out_specs=pl.BlockSpec((c8.shape[0], tn), lambda j: (0, j)),
        compiler_params=_cparams(("arbitrary",)),
        name="mod",
    )(c8, w_mod, b_mod)


FFN_STAGE_ROWS = 128


def _ffn_kernel(x_ref, g_ref, mod_ref, wg_ref, wu_ref, wo_ref, fg_ref, *refs,
                gi, mi, nf, nstage, final_norm, next_pre, cast_blocks):
    nc = len(cast_blocks)
    src_refs, refs = refs[:nc], refs[nc:]
    if next_pre is None:
        o_ref, refs = refs[0], refs[1:]
    else:
        o_ref, u_ref, refs = refs[0], refs[1], refs[2:]
    dst_refs, (pre_ref, acc_ref, vec_ref, r_ref) = refs[:nc], refs[nc:]
    f = pl.program_id(2)
    rc, d = x_ref.shape[1], x_ref.shape[2]
    nr = NORM_ROWS

    def bcast(v):
        return jnp.broadcast_to(v, (nr, d))

    def inv_rms(y):
        ss = jnp.sum(y * y, axis=-1, keepdims=True)
        return jnp.broadcast_to(lax.rsqrt(ss * (1.0 / d) + RMS_EPS), (nr, LANES))

    def lanes(r):
        return jnp.concatenate([r] * (d // LANES), axis=-1)

    @pl.when(f == 0)
    def _():
        vec_ref[0] = bcast(g_ref[pl.ds(gi, 1), :] * (1.0 + mod_ref[0, pl.ds(mi + 1, 1), :]))
        vec_ref[1] = bcast(mod_ref[0, pl.ds(mi, 1), :])
        vec_ref[2] = bcast(HALF * mod_ref[0, pl.ds(mi + 2, 1), :])
        if final_norm:
            vec_ref[3] = bcast(fg_ref[...])
        if next_pre is not None:
            gi2, mi2 = next_pre
            vec_ref[3] = bcast(g_ref[pl.ds(gi2, 1), :] * (1.0 + mod_ref[0, pl.ds(mi2 + 1, 1), :]))
            vec_ref[4] = bcast(mod_ref[0, pl.ds(mi2, 1), :])

    @pl.when(f < nstage)
    def _():
        base = pl.multiple_of(f * rc, rc)

        def stats(r0):
            r_ref[pl.ds(r0, nr), :] = inv_rms(x_ref[0, pl.ds(r0, nr), :])

        def apply(r0):
            pre = x_ref[0, pl.ds(r0, nr), :] * lanes(r_ref[pl.ds(r0, nr), :]) * vec_ref[0] + vec_ref[1]
            pre_ref[pl.ds(base + r0, nr), :] = pre.astype(BF16)
            acc_ref[pl.ds(base + r0, nr), :] = jnp.zeros((nr, d), F32)

        _for_rows(rc, nr, stats)
        _for_rows(rc, nr, apply)

    @pl.when((f >= nstage) & (f < nstage + nf))
    def _():
        p = pre_ref[...]
        gate = _dot(p, wg_ref[...])
        up = _dot(p, wu_ref[...])
        act = (gate * jax.nn.sigmoid(gate) * up).astype(BF16)
        acc_ref[...] += _dot(act, wo_ref[...])
        step = (pl.program_id(0) * pl.num_programs(1) + pl.program_id(1)) * nf + (f - nstage)
        for src, dst, nblk in zip(src_refs, dst_refs, cast_blocks):
            @pl.when(step < nblk)
            def _(src=src, dst=dst):
                dst[...] = src[...].astype(BF16)

    @pl.when(f >= nstage + nf)
    def _():
        base = pl.multiple_of((f - (nstage + nf)) * rc, rc)

        renorm = final_norm or next_pre is not None

        def residual(r0):
            r = pl.ds(r0, nr)
            y = x_ref[0, r, :] + vec_ref[2] * acc_ref[pl.ds(base + r0, nr), :]
            o_ref[0, r, :] = y
            if renorm:
                r_ref[r, :] = inv_rms(y)

        def norm(r0):
            r = pl.ds(r0, nr)
            yn = o_ref[0, r, :] * lanes(r_ref[r, :])
            if final_norm:
                o_ref[0, r, :] = yn * vec_ref[3]
            if next_pre is not None:
                u_ref[0, r, :] = (yn * vec_ref[3] + vec_ref[4]).astype(BF16)

        _for_rows(rc, nr, residual)
        if renorm:
            _for_rows(rc, nr, norm)


def _cast_plan(arr, nsteps):
    r, c = arr.shape
    for parts in (8, 4, 2, 1):
        if c % (parts * LANES):
            continue
        br = BF16_ROWS
        while br <= r:
            if r % br == 0 and (r // br) * parts <= nsteps:
                return br, c // parts
            br *= 2
    return None


def _ffn(x, norm_g, mods, mod_row, w_in, w_out, final_g, *, gi, mi, tm, tf, final_norm=False, next_pre=None,
         casts=()):
    bsz, s, d = x.shape
    ff = w_out.shape[0]
    nf = ff // tf
    tm = _tile(s, tm)
    rc = _tile(tm, FFN_STAGE_ROWS)
    nstage = tm // rc
    ni = s // tm

    def hid(f):
        return jnp.clip(f - nstage, 0, nf - 1)

    cast_blocks, cast_in, cast_out, cast_shapes = [], [], [], []
    for arr, (br, bc) in casts:
        nrb, ncb = arr.shape[0] // br, arr.shape[1] // bc
        assert nrb * br == arr.shape[0] and ncb * bc == arr.shape[1] and nrb * ncb <= bsz * ni * nf

        def blk(b, i, f, nrb=nrb, ncb=ncb):
            step = jnp.minimum((b * ni + i) * nf + hid(f), nrb * ncb - 1)
            return step // ncb, step % ncb

        cast_blocks.append(nrb * ncb)
        cast_in.append(pl.BlockSpec((br, bc), blk))
        cast_out.append(pl.BlockSpec((br, bc), blk))
        cast_shapes.append(jax.ShapeDtypeStruct(arr.shape, BF16))
    kern = functools.partial(_ffn_kernel, gi=gi, mi=mi, nf=nf, nstage=nstage, final_norm=final_norm,
                             next_pre=next_pre, cast_blocks=tuple(cast_blocks))

    def x_blk(b, i, f):
        return b, i * nstage + jnp.where(f < nstage, f, jnp.clip(f - (nstage + nf), 0, nstage - 1)), 0

    def o_blk(b, i, f):
        return b, i * nstage + jnp.clip(f - (nstage + nf), 0, nstage - 1), 0

    out_shape = [jax.ShapeDtypeStruct((bsz, s, d), F32)]
    out_specs = [pl.BlockSpec((1, rc, d), o_blk)]
    if next_pre is not None:
        out_shape.append(jax.ShapeDtypeStruct((bsz, s, d), BF16))
        out_specs.append(pl.BlockSpec((1, rc, d), o_blk))
    return pl.pallas_call(
        kern,
        out_shape=out_shape + cast_shapes,
        grid=(bsz, ni, nf + 2 * nstage),
        in_specs=[pl.BlockSpec((1, rc, d), x_blk),
                  pl.BlockSpec(norm_g.shape, lambda b, i, f: (0, 0)),
                  pl.BlockSpec((1, N_MOD, d), lambda b, i, f: (mod_row(b), 0, 0)),
                  pl.BlockSpec((d, tf), lambda b, i, f: (0, hid(f))),
                  pl.BlockSpec((d, tf), lambda b, i, f: (0, nf + hid(f))),
                  pl.BlockSpec((tf, d), lambda b, i, f: (hid(f), 0)),
                  pl.BlockSpec((1, d), lambda b, i, f: (0, 0))] + cast_in,
        out_specs=out_specs + cast_out,
        scratch_shapes=[pltpu.VMEM((tm, d), BF16), pltpu.VMEM((tm, d), F32),
                        pltpu.VMEM((5, NORM_ROWS, d), F32), pltpu.VMEM((rc, LANES), F32)],
        compiler_params=_cparams(("arbitrary", "arbitrary", "arbitrary")),
        name="ffn",
    )(x, norm_g, mods, w_in, w_in, w_out, final_g, *[arr for arr, _ in casts])


def _mm_kernel(a_ref, b_ref, o_ref):
    o_ref[0] = _dot(a_ref[0], b_ref[...]).astype(o_ref.dtype)


def _mm(a, w, out_dtype, *, tm, tn, col0=0, ncols=None):
    bsz, s, k = a.shape
    n = w.shape[1] - col0 if ncols is None else ncols
    tm, tn = _tile(s, tm), _tile(math.gcd(n, col0) if col0 else n, tn)
    j0 = col0 // tn
    return pl.pallas_call(
        _mm_kernel,
        out_shape=jax.ShapeDtypeStruct((bsz, s, n), out_dtype),
        grid=(bsz, s // tm, n // tn),
        in_specs=[pl.BlockSpec((1, tm, k), lambda b, i, j: (b, i, 0)),
                  pl.BlockSpec((k, tn), lambda b, i, j: (0, j0 + j))],
        out_specs=pl.BlockSpec((1, tm, tn), lambda b, i, j: (b, i, j)),
        compiler_params=_cparams(("arbitrary", "arbitrary", "arbitrary")),
        name="mm",
    )(a, w)


def _outproj_kernel(a_ref, w_ref, x_ref, mod_ref, o_ref, *, mi):
    o_ref[0] = x_ref[0] + mod_ref[0, pl.ds(mi, 1), :] * _dot(a_ref[0], w_ref[...])


def _outproj(a, w, x, mods, mod_row, *, mi, tm, tn):
    bsz, s, k = a.shape
    n = w.shape[1]
    tm, tn = _tile(s, tm), _tile(n, tn)
    return pl.pallas_call(
        functools.partial(_outproj_kernel, mi=mi),
        out_shape=jax.ShapeDtypeStruct((bsz, s, n), F32),
        grid=(bsz, s // tm, n // tn),
        in_specs=[pl.BlockSpec((1, tm, k), lambda b, i, j: (b, i, 0)),
                  pl.BlockSpec((k, tn), lambda b, i, j: (0, j)),
                  pl.BlockSpec((1, tm, tn), lambda b, i, j: (b, i, j)),
                  pl.BlockSpec((1, N_MOD, tn), lambda b, i, j: (mod_row(b), 0, j))],
        out_specs=pl.BlockSpec((1, tm, tn), lambda b, i, j: (b, i, j)),
        compiler_params=_cparams(("arbitrary", "arbitrary", "arbitrary")),
        name="outproj",
    )(a, w, x, mods)


def _ssm_tables(lam_re, lam_im, log_step, b_re, b_im, c_re, c_im):
    t = SSM_CHUNK
    hp = lax.Precision.HIGHEST
    lr = jnp.minimum(lam_re.astype(F32), LAMBDA_RE_MAX)
    li = lam_im.astype(F32)
    step = jnp.exp(log_step.astype(F32))[..., None]
    m = jnp.arange(t + 1, dtype=F32)[:, None, None, None]
    mag = jnp.exp(m * (lr * step)[None])
    ang = m * (li * step)[None]
    pw_re, pw_im = mag * jnp.cos(ang), mag * jnp.sin(ang)
    nr, ni = pw_re[1] - 1.0, pw_im[1]
    den = lr * lr + li * li
    q_re, q_im = (nr * lr + ni * li) / den, (ni * lr - nr * li) / den
    bb_re = q_re[..., None] * b_re - q_im[..., None] * b_im
    bb_im = q_re[..., None] * b_im + q_im[..., None] * b_re
    cr, ci = c_re.astype(F32), c_im.astype(F32)

    cl_re = cr[:, :, None] * pw_re[:t].transpose(1, 2, 0, 3)[:, :, :, None, :] \
        - ci[:, :, None] * pw_im[:t].transpose(1, 2, 0, 3)[:, :, :, None, :]
    cl_im = cr[:, :, None] * pw_im[:t].transpose(1, 2, 0, 3)[:, :, :, None, :] \
        + ci[:, :, None] * pw_re[:t].transpose(1, 2, 0, 3)[:, :, :, None, :]
    kern = jnp.einsum('dgtkp,dgpq->dgtkq', cl_re, bb_re, precision=hp) \
        - jnp.einsum('dgtkp,dgpq->dgtkq', cl_im, bb_im, precision=hp)
    g, kk = kern.shape[1], kern.shape[3]
    zero = (jnp.arange(t) == 0).astype(F32)[None, :, None, None]
    kpos = kern[0] + zero * kern[1]
    kneg = kern[1] + zero * kern[0]
    rpos = kpos.transpose(0, 3, 1, 2).reshape(g, kk, t * kk)
    rneg = kneg.transpose(0, 3, 1, 2).reshape(g, kk, t * kk)

    pf_re, pf_im = pw_re[:t][::-1, 0], pw_im[:t][::-1, 0]
    pb_re, pb_im = pw_re[:t, 1], pw_im[:t, 1]

    def st(p_re, p_im, d):
        re = p_re[:, :, :, None] * bb_re[d][None] - p_im[:, :, :, None] * bb_im[d][None]
        im = p_re[:, :, :, None] * bb_im[d][None] + p_im[:, :, :, None] * bb_re[d][None]
        return re.transpose(1, 0, 3, 2), im.transpose(1, 0, 3, 2)

    wf_re, wf_im = st(pf_re, pf_im, 0)
    wb_re, wb_im = st(pb_re, pb_im, 1)
    p = lr.shape[-1]
    wst = jnp.concatenate([wf_re, wb_re, wf_im, wb_im], axis=-1).reshape(g, t * kk, 4 * p)

    vf_pw_re, vf_pw_im = pw_re[1:, 0], pw_im[1:, 0]
    vb_pw_re, vb_pw_im = pw_re[1:, 1][::-1], pw_im[1:, 1][::-1]

    def rd(p_re, p_im, d):
        d_re = cr[d][None] * p_re[:, :, None, :] - ci[d][None] * p_im[:, :, None, :]
        d_im = cr[d][None] * p_im[:, :, None, :] + ci[d][None] * p_re[:, :, None, :]
        return d_re.transpose(1, 3, 0, 2), -d_im.transpose(1, 3, 0, 2)

    vf_re, vf_im = rd(vf_pw_re, vf_pw_im, 0)
    vb_re, vb_im = rd(vb_pw_re, vb_pw_im, 1)
    vrd = jnp.concatenate([vf_re, vb_re, vf_im, vb_im], axis=1).reshape(g, 4 * p, t * kk)

    a_re = jnp.concatenate([pw_re[t, 0], pw_re[t, 1]], axis=-1)[:, None, :]
    a_im = jnp.concatenate([pw_im[t, 0], pw_im[t, 1]], axis=-1)[:, None, :]
    return rpos.astype(BF16), rneg.astype(BF16), wst.astype(BF16), vrd.astype(BF16), a_re, a_im


SSM_GPB = LANES // SSM_GROUP


def _ssm_block_tables(rpos, rneg, wst, vrd, a_re, a_im):
    rpos, rneg, wst, vrd = lax.optimization_barrier((rpos, rneg, wst, vrd))
    g, _, tk = rpos.shape
    t, k, gpb = SSM_CHUNK, SSM_GROUP, SSM_GPB
    nb = g // gpb
    p2 = wst.shape[2] // 2
    tl, hw = t * gpb * k, gpb * p2
    rpos = rpos.reshape(nb, gpb * k, tk)
    rneg = rneg.reshape(nb, gpb * k, tk)
    rw = wst.reshape(nb, gpb, t, k, 2 * p2).transpose(0, 2, 1, 3, 4).reshape(nb, t, gpb * k, 2 * p2)
    rv = vrd.reshape(nb, gpb, 2, p2, tk).transpose(0, 2, 1, 3, 4).reshape(nb, 2, hw, tk)
    jj, kk = np.arange(tk) // k, np.arange(tk) % k
    sel = (jj[None, :, None] == np.arange(t)[:, None, None]) & (kk[None, :, None] == (np.arange(LANES) % k)[None, None, :])
    sel = jnp.asarray(sel, BF16)
    toep_b, wst_b, vrd_b = pl.pallas_call(
        functools.partial(_ssm_expand_kernel, t=t, k=k, p2=p2),
        out_shape=[jax.ShapeDtypeStruct((nb, tl, tl), BF16), jax.ShapeDtypeStruct((nb, tl, 2 * hw), BF16),
                   jax.ShapeDtypeStruct((nb, 2 * hw, tl), BF16)],
        grid=(nb, t),
        in_specs=[pl.BlockSpec((1, gpb * k, tk), lambda b, i: (b, 0, 0)),
                  pl.BlockSpec((1, gpb * k, tk), lambda b, i: (b, 0, 0)),
                  pl.BlockSpec((1, 1, gpb * k, 2 * p2), lambda b, i: (b, i, 0, 0)),
                  pl.BlockSpec((1, 2, hw, tk), lambda b, i: (b, 0, 0, 0)),
                  pl.BlockSpec(sel.shape, lambda b, i: (0, 0, 0))],
        out_specs=[pl.BlockSpec((1, gpb * k, tl), lambda b, i: (b, i, 0)),
                   pl.BlockSpec((1, gpb * k, 2 * hw), lambda b, i: (b, i, 0)),
                   pl.BlockSpec((1, 2 * hw, gpb * k), lambda b, i: (b, 0, i))],
        compiler_params=_cparams(("arbitrary", "arbitrary")),
        name="ssm_expand",
    )(rpos, rneg, rw, rv, sel)
    return toep_b, wst_b, vrd_b, a_re.reshape(nb, 1, hw), a_im.reshape(nb, 1, hw)


def _ssm_expand_kernel(rpos_ref, rneg_ref, rw_ref, rv_ref, sel_ref, toep_ref, wst_ref, vrd_ref, *, t, k, p2):
    i = pl.program_id(1)
    lb = rpos_ref.shape[1]
    hw = rv_ref.shape[2]
    gpb = lb // k
    same_g = (lax.broadcasted_iota(jnp.int32, (lb, lb), 0) // k) == (lax.broadcasted_iota(jnp.int32, (lb, lb), 1) // k)
    for j in range(t):
        pos = _dot(rpos_ref[0], sel_ref[jnp.clip(j - i, 0, t - 1)])
        neg = _dot(rneg_ref[0], sel_ref[jnp.clip(i - j, 0, t - 1)])
        blk = jnp.where(i <= j, pos, neg)
        toep_ref[0, :, j * lb:(j + 1) * lb] = jnp.where(same_g, blk, 0.0).astype(BF16)
    w = rw_ref[0, 0]
    same_gw = (lax.broadcasted_iota(jnp.int32, (lb, hw), 0) // k) == (lax.broadcasted_iota(jnp.int32, (lb, hw), 1) // p2)
    for r in range(2):
        wr = jnp.concatenate([w[:, r * p2:(r + 1) * p2]] * gpb, axis=-1)
        wst_ref[0, :, r * hw:(r + 1) * hw] = jnp.where(same_gw, wr, jnp.zeros_like(wr))
    same_gv = (lax.broadcasted_iota(jnp.int32, (hw, lb), 0) // p2) == (lax.broadcasted_iota(jnp.int32, (hw, lb), 1) // k)
    for r in range(2):
        v = _dot(rv_ref[0, r], sel_ref[i])
        vrd_ref[0, r * hw:(r + 1) * hw, :] = jnp.where(same_gv, v, 0.0).astype(BF16)


def _ssm_kernel(u_ref, ws_ref, ar_ref, ai_ref, h0_ref, *refs, nc, want_y):
    t = SSM_CHUNK
    if want_y:
        wt_ref, v_ref, d_ref, z_ref, hfin_ref, s_ref, yi_ref, hp_ref, tmp_ref, y_ref = refs
    else:
        hfin_ref, s_ref = refs
    a = jnp.concatenate([u_ref[0, pl.ds(j, nc, stride=t), :] for j in range(t)], axis=-1).astype(BF16)
    s = _dot(a, ws_ref[0])
    nlb = s.shape[1] // LANES
    gpb = nlb // 2
    for k in range(nlb):
        s_ref[pl.ds(k, nc, stride=nlb), :] = s[:, k * LANES:(k + 1) * LANES]
    if want_y:
        yi_ref[...] = _dot(a, wt_ref[0])

    ar = ar_ref[0]
    ai = ai_ref[0]
    fwd = lax.broadcasted_iota(jnp.int32, (gpb, LANES), 1) < (LANES // 2)

    def body(c, carry):
        h_re, h_im = carry
        rf = pl.multiple_of(c * nlb, nlb)
        rb = pl.multiple_of((nc - 1 - c) * nlb, nlb)
        if want_y:
            hp_ref[pl.ds(rf, gpb), :] = h_re
            hp_ref[pl.ds(rf + gpb, gpb), :] = h_im
            tmp_ref[pl.ds(rb, gpb), :] = h_re
            tmp_ref[pl.ds(rb + gpb, gpb), :] = h_im
        s_re = jnp.where(fwd, s_ref[pl.ds(rf, gpb), :], s_ref[pl.ds(rb, gpb), :])
        s_im = jnp.where(fwd, s_ref[pl.ds(rf + gpb, gpb), :], s_ref[pl.ds(rb + gpb, gpb), :])
        return ar * h_re - ai * h_im + s_re, ar * h_im + ai * h_re + s_im

    h0 = h0_ref[0, 0]
    h_re, h_im = lax.fori_loop(0, nc, body, (h0[:gpb], h0[gpb:]))
    hfin_ref[0, 0] = jnp.concatenate([h_re, h_im], axis=0)

    if want_y:
        fwd2 = lax.broadcasted_iota(jnp.int32, hp_ref.shape, 1) < (LANES // 2)
        hp_ref[...] = jnp.where(fwd2, hp_ref[...], tmp_ref[...])
        hp = jnp.concatenate([hp_ref[pl.ds(k, nc, stride=nlb), :] for k in range(nlb)], axis=-1).astype(BF16)
        y = yi_ref[...] + _dot(hp, v_ref[0])
        for j in range(t):
            y_ref[pl.ds(j, nc, stride=t), :] = y[:, j * LANES:(j + 1) * LANES]

        def rows(r0):
            r = pl.ds(r0, ROW_CHUNK)
            z_ref[0, r, :] = jax.nn.gelu(y_ref[r, :] + d_ref[...] * u_ref[0, r, :]).astype(BF16)

        _for_row_chunks(nc * t, rows)


def _ssm(u, tables, h0, d, *, want_y):
    toep_b, wst_b, vrd_b, ar, ai = tables
    bsz, n, _ = u.shape
    nb, tl, sw = wst_b.shape
    nc = n // SSM_CHUNK
    single = pl.Buffered(1)
    nlb = sw // LANES
    gpb = nlb // 2
    shapes = [jax.ShapeDtypeStruct((bsz, nb, nlb, LANES), F32)]
    specs = [pl.BlockSpec((1, 1, nlb, LANES), lambda k, b: (b, k, 0, 0))]
    scratch = [pltpu.VMEM((nc * nlb, LANES), F32)]
    in_specs = [pl.BlockSpec((1, n, LANES), lambda k, b: (b, 0, k)),
                pl.BlockSpec((1, tl, sw), lambda k, b: (k, 0, 0), pipeline_mode=single),
                pl.BlockSpec((1, gpb, LANES), lambda k, b: (k, 0, 0)),
                pl.BlockSpec((1, gpb, LANES), lambda k, b: (k, 0, 0)),
                pl.BlockSpec((1, 1, nlb, LANES), lambda k, b: (b, k, 0, 0))]
    args = [u, wst_b, ar.reshape(nb, gpb, LANES), ai.reshape(nb, gpb, LANES), h0]
    if want_y:
        in_specs += [pl.BlockSpec((1, tl, tl), lambda k, b: (k, 0, 0), pipeline_mode=single),
                     pl.BlockSpec((1, sw, tl), lambda k, b: (k, 0, 0), pipeline_mode=single),
                     pl.BlockSpec((1, LANES), lambda k, b: (0, k))]
        args += [toep_b, vrd_b, d]
        shapes.insert(0, jax.ShapeDtypeStruct((bsz, n, nb * LANES), BF16))
        specs.insert(0, pl.BlockSpec((1, n, LANES), lambda k, b: (b, 0, k)))
        scratch += [pltpu.VMEM((nc, tl), F32), pltpu.VMEM((nc * nlb, LANES), F32),
                    pltpu.VMEM((nc * nlb, LANES), F32), pltpu.VMEM((n, LANES), F32)]
    return pl.pallas_call(
        functools.partial(_ssm_kernel, nc=nc, want_y=want_y),
        out_shape=shapes,
        grid=(nb, bsz),
        in_specs=in_specs,
        out_specs=specs,
        scratch_shapes=scratch,
        compiler_params=_cparams(("arbitrary", "arbitrary")),
        name="ssm",
    )(*args)


def _glu_kernel(z_ref, w_ref, o_ref, *, ws):
    z = z_ref[0]
    a = _dot(z, w_ref[:, :ws])
    b = _dot(z, w_ref[:, ws:])
    o_ref[0] = (a * jax.nn.sigmoid(b)).astype(BF16)


def _glu(z, w_glu, *, tm):
    bsz, s, ws = z.shape
    tm = _tile(s, tm)
    return pl.pallas_call(
        functools.partial(_glu_kernel, ws=ws),
        out_shape=jax.ShapeDtypeStruct((bsz, s, ws), BF16),
        grid=(bsz, s // tm),
        in_specs=[pl.BlockSpec((1, tm, ws), lambda b, i: (b, i, 0)),
                  pl.BlockSpec(w_glu.shape, lambda b, i: (0, 0))],
        out_specs=pl.BlockSpec((1, tm, ws), lambda b, i: (b, i, 0)),
        compiler_params=_cparams(("arbitrary", "arbitrary")),
        name="glu",
    )(z, w_glu)


POOL_TILE = 4 * GRID_W


def _pool_consts(n):
    t = np.arange(POOL_TILE)
    pcs, invs = [], []
    r = np.arange(n) // GRID_W
    c = np.arange(n) % GRID_W
    rows = n // GRID_W
    for w in POOL_WINDOWS:
        lo, hi = w // 2, w - w // 2
        same_row = (t[:, None] // GRID_W) == (t[None, :] // GRID_W)
        dc = (t[None, :] % GRID_W) - (t[:, None] % GRID_W)
        pcs.append((same_row & (dc >= -lo) & (dc < hi)).astype(np.float32))
        cnt_r = np.minimum(r + hi, rows) - np.maximum(r - lo, 0)
        cnt_c = np.minimum(c + hi, GRID_W) - np.maximum(c - lo, 0)
        invs.append((1.0 / (cnt_r * cnt_c)).astype(np.float32)[:, None])
    return jnp.asarray(np.stack(pcs), BF16), jnp.asarray(np.stack(invs), F32)


def _pool_kernel(v_ref, pc_ref, inv_ref, pw_ref, sc_ref, o_ref, cs_ref, *, n, pad):
    wi = pl.program_id(0)
    nt = n // POOL_TILE
    ch = v_ref.shape[2]
    zeros = jnp.zeros((pad, ch), F32)
    cs_ref[pl.ds(0, pad), :] = zeros
    cs_ref[pl.ds(pad + n, pad), :] = zeros
    pc = pc_ref[0]

    def col_body(i, _):
        off = pl.multiple_of(i * POOL_TILE, POOL_TILE)
        v = v_ref[0, pl.ds(off, POOL_TILE), :]
        hi = v.astype(BF16)
        lo = (v - hi.astype(F32)).astype(BF16)
        cs_ref[pl.ds(pad + off, POOL_TILE), :] = _dot(pc, hi) + _dot(pc, lo)
        return 0

    lax.fori_loop(0, nt, col_body, 0)

    for k, w in enumerate(POOL_WINDOWS):
        @pl.when(wi == k)
        def _(w=w):
            def row_body(i, _):
                off = pl.multiple_of(i * POOL_TILE, POOL_TILE)
                acc = cs_ref[pl.ds(pad + off - (w // 2) * GRID_W, POOL_TILE), :]
                for dlt in range(-(w // 2) + 1, w - w // 2):
                    acc = acc + cs_ref[pl.ds(pad + off + dlt * GRID_W, POOL_TILE), :]
                mixed = acc * inv_ref[0, pl.ds(off, POOL_TILE), :] - v_ref[0, pl.ds(off, POOL_TILE), :]
                y = _dot(mixed.astype(BF16), pw_ref[0]) * sc_ref[0]
                o_ref[0, pl.ds(off, POOL_TILE), :] = y.astype(BF16)
                return 0

            lax.fori_loop(0, nt, row_body, 0)


def _pool(usp, col0, pool_w, pool_scale, pc, inv):
    bsz, n, _ = usp.shape
    nw, ch, _ = pool_w.shape
    pad = (max(POOL_WINDOWS) // 2) * GRID_W
    cb0 = col0 // ch
    return pl.pallas_call(
        functools.partial(_pool_kernel, n=n, pad=pad),
        out_shape=jax.ShapeDtypeStruct((bsz, n, nw * ch), BF16),
        grid=(nw, bsz),
        in_specs=[pl.BlockSpec((1, n, ch), lambda w, b: (b, 0, cb0 + w)),
                  pl.BlockSpec((1, POOL_TILE, POOL_TILE), lambda w, b: (w, 0, 0)),
                  pl.BlockSpec((1, n, 1), lambda w, b: (w, 0, 0)),
                  pl.BlockSpec((1, ch, ch), lambda w, b: (w, 0, 0)),
                  pl.BlockSpec((1, 1, ch), lambda w, b: (w, 0, 0))],
        out_specs=pl.BlockSpec((1, n, ch), lambda w, b: (b, 0, w)),
        scratch_shapes=[pltpu.VMEM((n + 2 * pad, ch), F32)],
        compiler_params=_cparams(("arbitrary", "arbitrary")),
        name="pool",
    )(usp, pc, inv, pool_w, pool_scale)


def _merge_kernel(a_ref, p_ref, wa_ref, wb_ref, ga_ref, gb_ref, o_ref):
    ya = _dot(a_ref[0], wa_ref[...])
    yb = _dot(p_ref[0], wb_ref[...])
    m = jax.nn.sigmoid(ga_ref[0].astype(F32)) * ya + jax.nn.sigmoid(gb_ref[0].astype(F32)) * yb
    o_ref[0] = m.astype(BF16)


def _merge(ag, yp, wa, wb, gates, *, tm, tn):
    bsz, s, ka = ag.shape
    kb = yp.shape[2]
    d = wa.shape[1]
    tm, tn = _tile(s, tm), _tile(d, tn)
    nj = d // tn
    return pl.pallas_call(
        _merge_kernel,
        out_shape=jax.ShapeDtypeStruct((bsz, s, d), BF16),
        grid=(bsz, s // tm, nj),
        in_specs=[pl.BlockSpec((1, tm, ka), lambda b, i, j: (b, i, 0)),
                  pl.BlockSpec((1, tm, kb), lambda b, i, j: (b, i, 0)),
                  pl.BlockSpec((ka, tn), lambda b, i, j: (0, j)),
                  pl.BlockSpec((kb, tn), lambda b, i, j: (0, j)),
                  pl.BlockSpec((1, tm, tn), lambda b, i, j: (b, i, j)),
                  pl.BlockSpec((1, tm, tn), lambda b, i, j: (b, i, nj + j))],
        out_specs=pl.BlockSpec((1, tm, tn), lambda b, i, j: (b, i, j)),
        compiler_params=_cparams(("arbitrary", "arbitrary", "arbitrary")),
        name="merge",
    )(ag, yp, wa, wb, gates, gates)


def kernel(x, c, ctx, c_ctx, w_mod, b_mod, norm_g, final_g, ffn1_w_in, ffn1_w_out, ffn2_w_in, ffn2_w_out,
           w_in, ssm_lambda_re, ssm_lambda_im, ssm_log_step, ssm_b_re, ssm_b_im, ssm_c_re, ssm_c_im, ssm_d,
           w_glu, w_branch_a, pool_w, pool_scale, w_branch_b, w_out):
    bsz, seq, d = x.shape
    assert w_mod.shape[0] == 1, "single-layer problem"
    ssm_w = ssm_d.shape[1]
    pool_width = pool_scale.shape[1]
    nw = len(POOL_WINDOWS)
    pch = pool_width // nw
    p = ssm_lambda_re.shape[-1]
    assert seq % POOL_TILE == 0 and ctx.shape[1] % SSM_CHUNK == 0 and ssm_w % LANES == 0

    f1_in, f1_out = ffn1_w_in[0].astype(BF16), ffn1_w_out[0].astype(BF16)
    w_s = w_in[0][:, :ssm_w].astype(BF16)
    late = [ffn2_w_in[0], ffn2_w_out[0], w_in[0], w_out[0]]
    wbb, wba, wglu, pw = (w_branch_b[0].astype(BF16), w_branch_a[0].astype(BF16), w_glu[0].astype(BF16),
                          pool_w[0].astype(BF16))
    psc = pool_scale[0].reshape(nw, 1, pch)
    ng = norm_g[0]
    fg = final_g.reshape(1, d)

    rows = -(-(bsz + 1) // 8) * 8
    c8 = jnp.zeros((rows, d), F32).at[:bsz].set(c).at[bsz].set(c_ctx)
    mods = _mod(c8, w_mod[0], b_mod, tn=512 if w_mod.shape[2] % 512 == 0 else w_mod.shape[2]).reshape(rows, N_MOD, d)
    lat = lambda b: b
    con = lambda b: bsz

    tables = _ssm_block_tables(*_ssm_tables(ssm_lambda_re[0], ssm_lambda_im[0], ssm_log_step[0],
                                            ssm_b_re[0], ssm_b_im[0], ssm_c_re[0], ssm_c_im[0]))

    nctx = ctx.shape[1]
    _, uc = _ffn(ctx.reshape(1, bsz * nctx, d), ng, mods, con, f1_in, f1_out, fg, gi=0, mi=0, tm=1024, tf=256,
                 next_pre=(1, 3))
    us_c = _mm(uc, w_s, F32, tm=1024, tn=512).reshape(bsz, nctx, ssm_w)
    assert 2 * p == LANES, "fwd | bwd states of one group fill one lane tile"
    h_ctx, = _ssm(us_c, tables, jnp.zeros((bsz, ssm_w // LANES, 2 * SSM_GPB, LANES), F32), ssm_d, want_y=False)

    tf = 256
    nsteps = bsz * (seq // _tile(seq, 1024)) * (f1_out.shape[0] // tf)
    plans = [_cast_plan(a, nsteps) for a in late]
    x1, u, *cast = _ffn(x, ng, mods, lat, f1_in, f1_out, fg, gi=0, mi=0, tm=1024, tf=tf, next_pre=(1, 3),
                        casts=[(a, pln) for a, pln in zip(late, plans) if pln is not None])
    cast = iter(cast)
    f2_in, f2_out, w_all, wo = [a.astype(BF16) if pln is None else next(cast) for a, pln in zip(late, plans)]
    usp = _mm(u, w_all, F32, tm=1024, tn=512, ncols=ssm_w + pool_width)
    gates = _mm(u, w_all, BF16, tm=1024, tn=512, col0=ssm_w + pool_width)

    z, _ = _ssm(usp, tables, h_ctx, ssm_d, want_y=True)
    ag = _glu(z, wglu, tm=512)

    pc, inv = _pool_consts(seq)
    yp = _pool(usp, ssm_w, pw, psc, pc, inv)

    merged = _merge(ag, yp, wba, wbb, gates, tm=1024, tn=512)
    x2 = _outproj(merged, wo, x1, mods, lat, mi=5, tm=1024, tn=512)
    out, = _ffn(x2, ng, mods, lat, f2_in, f2_out, fg, gi=2, mi=6, tm=1024, tf=256, final_norm=True)
    return out
```

```python
import functools
import math

import jax
import jax.numpy as jnp
import numpy as np
from jax import lax
from jax.experimental import pallas as pl
from jax.experimental.pallas import tpu as pltpu

BF16 = jnp.bfloat16
F32 = jnp.float32

RMS_EPS = 1e-6
LAMBDA_RE_MAX = -1e-4
HALF = 0.5
N_MOD = 9
SSM_GROUP = 16
POOL_WINDOWS = (2, 4, 8, 16)
GRID_W = 64
SSM_CHUNK = 16

LANES = 128
BF16_ROWS = 16
V7X_VMEM_BYTES = 64 * 1024 * 1024
VMEM_LIMIT = 60 * 1024 * 1024


def _cparams(sem):
    return pltpu.CompilerParams(dimension_semantics=sem, vmem_limit_bytes=VMEM_LIMIT)


def _tile(n, pref):
    t = min(n, pref)
    while n % t:
        t //= 2
    return t


ROW_UNROLL = 1
ROW_CHUNK = 64


def _for_row_chunks(n, fn):
    def body(i, carry):
        fn(pl.multiple_of(i * ROW_CHUNK, ROW_CHUNK))
        return carry

    lax.fori_loop(0, n // ROW_CHUNK, body, 0, unroll=ROW_UNROLL)


NORM_ROWS = 16
NORM_UNROLL = 4


def _for_rows(n, step, fn):
    def body(i, carry):
        fn(pl.multiple_of(i * step, step))
        return carry

    lax.fori_loop(0, n // step, body, 0, unroll=NORM_UNROLL)


def _dot(a, b):
    return jnp.dot(a, b, preferred_element_type=F32)


def _mod_kernel(c_ref, w_ref, b_ref, o_ref):
    c = c_ref[...]
    s = (c * jax.nn.sigmoid(c)).astype(BF16)
    o_ref[...] = _dot(s, w_ref[...].astype(BF16)) + b_ref[...]


def _mod(c8, w_mod, b_mod, tn):
    d, n = w_mod.shape
    return pl.pallas_call(
        _mod_kernel,
        out_shape=jax.ShapeDtypeStruct((c8.shape[0], n), F32),
        grid=(n // tn,),
        in_specs=[pl.BlockSpec(c8.shape, lambda j: (0, 0)),
                  pl.BlockSpec((d, tn), lambda j: (0, j)),
                  pl.BlockSpec((1, tn), lambda j: (0, j))],
        out_specs=pl.BlockSpec((c8.shape[0], tn), lambda j: (0, j)),
        compiler_params=_cparams(("arbitrary",)),
        name="mod",
    )(c8, w_mod, b_mod)


FFN_STAGE_ROWS = 128


def _ffn_kernel(x_ref, g_ref, mod_ref, wg_ref, wu_ref, wo_ref, fg_ref, *refs,
                gi, mi, nf, nstage, final_norm, next_pre, cast_blocks):
    nc = len(cast_blocks)
    src_refs, refs = refs[:nc], refs[nc:]
    if next_pre is None:
        o_ref, refs = refs[0], refs[1:]
    else:
        o_ref, u_ref, refs = refs[0], refs[1], refs[2:]
    dst_refs, (pre_ref, acc_ref, vec_ref, r_ref) = refs[:nc], refs[nc:]
    f = pl.program_id(2)
    rc, d = x_ref.shape[1], x_ref.shape[2]
    nr = NORM_ROWS

    def bcast(v):
        return jnp.broadcast_to(v, (nr, d))

    def inv_rms(y):
        ss = jnp.sum(y * y, axis=-1, keepdims=True)
        return jnp.broadcast_to(lax.rsqrt(ss * (1.0 / d) + RMS_EPS), (nr, LANES))

    def lanes(r):
        return jnp.concatenate([r] * (d // LANES), axis=-1)

    @pl.when(f == 0)
    def _():
        vec_ref[0] = bcast(g_ref[pl.ds(gi, 1), :] * (1.0 + mod_ref[0, pl.ds(mi + 1, 1), :]))
        vec_ref[1] = bcast(mod_ref[0, pl.ds(mi, 1), :])
        vec_ref[2] = bcast(HALF * mod_ref[0, pl.ds(mi + 2, 1), :])
        if final_norm:
            vec_ref[3] = bcast(fg_ref[...])
        if next_pre is not None:
            gi2, mi2 = next_pre
            vec_ref[3] = bcast(g_ref[pl.ds(gi2, 1), :] * (1.0 + mod_ref[0, pl.ds(mi2 + 1, 1), :]))
            vec_ref[4] = bcast(mod_ref[0, pl.ds(mi2, 1), :])

    @pl.when(f < nstage)
    def _():
        base = pl.multiple_of(f * rc, rc)

        def stats(r0):
            r_ref[pl.ds(r0, nr), :] = inv_rms(x_ref[0, pl.ds(r0, nr), :])

        def apply(r0):
            pre = x_ref[0, pl.ds(r0, nr), :] * lanes(r_ref[pl.ds(r0, nr), :]) * vec_ref[0] + vec_ref[1]
            pre_ref[pl.ds(base + r0, nr), :] = pre.astype(BF16)
            acc_ref[pl.ds(base + r0, nr), :] = jnp.zeros((nr, d), F32)

        _for_rows(rc, nr, stats)
        _for_rows(rc, nr, apply)

    @pl.when((f >= nstage) & (f < nstage + nf))
    def _():
        p = pre_ref[...]
        gate = _dot(p, wg_ref[...])
        up = _dot(p, wu_ref[...])
        act = (gate * jax.nn.sigmoid(gate) * up).astype(BF16)
        acc_ref[...] += _dot(act, wo_ref[...])
        step = (pl.program_id(0) * pl.num_programs(1) + pl.program_id(1)) * nf + (f - nstage)
        for src, dst, nblk in zip(src_refs, dst_refs, cast_blocks):
            @pl.when(step < nblk)
            def _(src=src, dst=dst):
                dst[...] = src[...].astype(BF16)

    @pl.when(f >= nstage + nf)
    def _():
        base = pl.multiple_of((f - (nstage + nf)) * rc, rc)

        renorm = final_norm or next_pre is not None

        def residual(r0):
            r = pl.ds(r0, nr)
            y = x_ref[0, r, :] + vec_ref[2] * acc_ref[pl.ds(base + r0, nr), :]
            o_ref[0, r, :] = y
            if renorm:
                r_ref[r, :] = inv_rms(y)

        def norm(r0):
            r = pl.ds(r0, nr)
            yn = o_ref[0, r, :] * lanes(r_ref[r, :])
            if final_norm:
                o_ref[0, r, :] = yn * vec_ref[3]
            if next_pre is not None:
                u_ref[0, r, :] = (yn * vec_ref[3] + vec_ref[4]).astype(BF16)

        _for_rows(rc, nr, residual)
        if renorm:
            _for_rows(rc, nr, norm)


def _cast_plan(arr, nsteps):
    r, c = arr.shape
    for parts in (8, 4, 2, 1):
        if c % (parts * LANES):
            continue
        br = BF16_ROWS
        while br <= r:
            if r % br == 0 and (r // br) * parts <= nsteps:
                return br, c // parts
            br *= 2
    return None


def _ffn(x, norm_g, mods, mod_row, w_in, w_out, final_g, *, gi, mi, tm, tf, final_norm=False, next_pre=None,
         casts=()):
    bsz, s, d = x.shape
    ff = w_out.shape[0]
    nf = ff // tf
    tm = _tile(s, tm)
    rc = _tile(tm, FFN_STAGE_ROWS)
    nstage = tm // rc
    ni = s // tm

    def hid(f):
        return jnp.clip(f - nstage, 0, nf - 1)

    cast_blocks, cast_in, cast_out, cast_shapes = [], [], [], []
    for arr, (br, bc) in casts:
        nrb, ncb = arr.shape[0] // br, arr.shape[1] // bc
        assert nrb * br == arr.shape[0] and ncb * bc == arr.shape[1] and nrb * ncb <= bsz * ni * nf

        def blk(b, i, f, nrb=nrb, ncb=ncb):
            step = jnp.minimum((b * ni + i) * nf + hid(f), nrb * ncb - 1)
            return step // ncb, step % ncb

        cast_blocks.append(nrb * ncb)
        cast_in.append(pl.BlockSpec((br, bc), blk))
        cast_out.append(pl.BlockSpec((br, bc), blk))
        cast_shapes.append(jax.ShapeDtypeStruct(arr.shape, BF16))
    kern = functools.partial(_ffn_kernel, gi=gi, mi=mi, nf=nf, nstage=nstage, final_norm=final_norm,
                             next_pre=next_pre, cast_blocks=tuple(cast_blocks))

    def x_blk(b, i, f):
        return b, i * nstage + jnp.where(f < nstage, f, jnp.clip(f - (nstage + nf), 0, nstage - 1)), 0

    def o_blk(b, i, f):
        return b, i * nstage + jnp.clip(f - (nstage + nf), 0, nstage - 1), 0

    out_shape = [jax.ShapeDtypeStruct((bsz, s, d), F32)]
    out_specs = [pl.BlockSpec((1, rc, d), o_blk)]
    if next_pre is not None:
        out_shape.append(jax.ShapeDtypeStruct((bsz, s, d), BF16))
        out_specs.append(pl.BlockSpec((1, rc, d), o_blk))
    return pl.pallas_call(
        kern,
        out_shape=out_shape + cast_shapes,
        grid=(bsz, ni, nf + 2 * nstage),
        in_specs=[pl.BlockSpec((1, rc, d), x_blk),
                  pl.BlockSpec(norm_g.shape, lambda b, i, f: (0, 0)),
                  pl.BlockSpec((1, N_MOD, d), lambda b, i, f: (mod_row(b), 0, 0)),
                  pl.BlockSpec((d, tf), lambda b, i, f: (0, hid(f))),
                  pl.BlockSpec((d, tf), lambda b, i, f: (0, nf + hid(f))),
                  pl.BlockSpec((tf, d), lambda b, i, f: (hid(f), 0)),
                  pl.BlockSpec((1, d), lambda b, i, f: (0, 0))] + cast_in,
        out_specs=out_specs + cast_out,
        scratch_shapes=[pltpu.VMEM((tm, d), BF16), pltpu.VMEM((tm, d), F32),
                        pltpu.VMEM((5, NORM_ROWS, d), F32), pltpu.VMEM((rc, LANES), F32)],
        compiler_params=_cparams(("arbitrary", "arbitrary", "arbitrary")),
        name="ffn",
    )(x, norm_g, mods, w_in, w_in, w_out, final_g, *[arr for arr, _ in casts])


def _mm_kernel(a_ref, b_ref, o_ref):
    o_ref[0] = _dot(a_ref[0], b_ref[...]).astype(o_ref.dtype)


def _mm(a, w, out_dtype, *, tm, tn, col0=0, ncols=None):
    bsz, s, k = a.shape
    n = w.shape[1] - col0 if ncols is None else ncols
    tm, tn = _tile(s, tm), _tile(math.gcd(n, col0) if col0 else n, tn)
    j0 = col0 // tn
    return pl.pallas_call(
        _mm_kernel,
        out_shape=jax.ShapeDtypeStruct((bsz, s, n), out_dtype),
        grid=(bsz, s // tm, n // tn),
        in_specs=[pl.BlockSpec((1, tm, k), lambda b, i, j: (b, i, 0)),
                  pl.BlockSpec((k, tn), lambda b, i, j: (0, j0 + j))],
        out_specs=pl.BlockSpec((1, tm, tn), lambda b, i, j: (b, i, j)),
        compiler_params=_cparams(("arbitrary", "arbitrary", "arbitrary")),
        name="mm",
    )(a, w)


def _mixout_kernel(a_ref, p_ref, wa_ref, wb_ref, ga_ref, gb_ref, wo_ref, x_ref, mod_ref, o_ref, m_ref, *, mi, nj):
    j = pl.program_id(2)
    tn = wa_ref.shape[1]

    @pl.when(j < nj)
    def _():
        ya = _dot(a_ref[0], wa_ref[...])
        yb = _dot(p_ref[0], wb_ref[...])
        m = jax.nn.sigmoid(ga_ref[0].astype(F32)) * ya + jax.nn.sigmoid(gb_ref[0].astype(F32)) * yb
        m_ref[:, pl.ds(pl.multiple_of(j * tn, tn), tn)] = m.astype(BF16)

    @pl.when(j >= nj)
    def _():
        o_ref[0] = x_ref[0] + mod_ref[0, pl.ds(mi, 1), :] * _dot(m_ref[...], wo_ref[...])


def _mixout(ag, yp, wa, wb, gates, wo, x, mods, mod_row, *, mi, tm, tn):
    bsz, s, ka = ag.shape
    kb = yp.shape[2]
    d = wa.shape[1]
    tm, tn = _tile(s, tm), _tile(d, tn)
    nj = d // tn

    def first(j):
        return jnp.minimum(j, nj - 1)

    def second(j):
        return jnp.maximum(j - nj, 0)

    return pl.pallas_call(
        functools.partial(_mixout_kernel, mi=mi, nj=nj),
        out_shape=jax.ShapeDtypeStruct((bsz, s, d), F32),
        grid=(bsz, s // tm, 2 * nj),
        in_specs=[pl.BlockSpec((1, tm, ka), lambda b, i, j: (b, i, 0)),
                  pl.BlockSpec((1, tm, kb), lambda b, i, j: (b, i, 0)),
                  pl.BlockSpec((ka, tn), lambda b, i, j: (0, first(j))),
                  pl.BlockSpec((kb, tn), lambda b, i, j: (0, first(j))),
                  pl.BlockSpec((1, tm, tn), lambda b, i, j: (b, i, first(j))),
                  pl.BlockSpec((1, tm, tn), lambda b, i, j: (b, i, nj + first(j))),
                  pl.BlockSpec((d, tn), lambda b, i, j: (0, second(j))),
                  pl.BlockSpec((1, tm, tn), lambda b, i, j: (b, i, second(j))),
                  pl.BlockSpec((1, N_MOD, tn), lambda b, i, j: (mod_row(b), 0, second(j)))],
        out_specs=pl.BlockSpec((1, tm, tn), lambda b, i, j: (b, i, second(j))),
        scratch_shapes=[pltpu.VMEM((tm, d), BF16)],
        compiler_params=_cparams(("arbitrary", "arbitrary", "arbitrary")),
        name="mixout",
    )(ag, yp, wa, wb, gates, gates, wo, x, mods)


def _ssm_tables(lam_re, lam_im, log_step, b_re, b_im, c_re, c_im):
    t = SSM_CHUNK
    hp = lax.Precision.HIGHEST
    lr = jnp.minimum(lam_re.astype(F32), LAMBDA_RE_MAX)
    li = lam_im.astype(F32)
    step = jnp.exp(log_step.astype(F32))[..., None]
    m = jnp.arange(t + 1, dtype=F32)[:, None, None, None]
    mag = jnp.exp(m * (lr * step)[None])
    ang = m * (li * step)[None]
    pw_re, pw_im = mag * jnp.cos(ang), mag * jnp.sin(ang)
    nr, ni = pw_re[1] - 1.0, pw_im[1]
    den = lr * lr + li * li
    q_re, q_im = (nr * lr + ni * li) / den, (ni * lr - nr * li) / den
    bb_re = q_re[..., None] * b_re - q_im[..., None] * b_im
    bb_im = q_re[..., None] * b_im + q_im[..., None] * b_re
    cr, ci = c_re.astype(F32), c_im.astype(F32)

    cl_re = cr[:, :, None] * pw_re[:t].transpose(1, 2, 0, 3)[:, :, :, None, :] \
        - ci[:, :, None] * pw_im[:t].transpose(1, 2, 0, 3)[:, :, :, None, :]
    cl_im = cr[:, :, None] * pw_im[:t].transpose(1, 2, 0, 3)[:, :, :, None, :] \
        + ci[:, :, None] * pw_re[:t].transpose(1, 2, 0, 3)[:, :, :, None, :]
    kern = jnp.einsum('dgtkp,dgpq->dgtkq', cl_re, bb_re, precision=hp) \
        - jnp.einsum('dgtkp,dgpq->dgtkq', cl_im, bb_im, precision=hp)
    g, kk = kern.shape[1], kern.shape[3]
    zero = (jnp.arange(t) == 0).astype(F32)[None, :, None, None]
    kpos = kern[0] + zero * kern[1]
    kneg = kern[1] + zero * kern[0]
    rpos = kpos.transpose(0, 3, 1, 2).reshape(g, kk, t * kk)
    rneg = kneg.transpose(0, 3, 1, 2).reshape(g, kk, t * kk)

    pf_re, pf_im = pw_re[:t][::-1, 0], pw_im[:t][::-1, 0]
    pb_re, pb_im = pw_re[:t, 1], pw_im[:t, 1]

    def st(p_re, p_im, d):
        re = p_re[:, :, :, None] * bb_re[d][None] - p_im[:, :, :, None] * bb_im[d][None]
        im = p_re[:, :, :, None] * bb_im[d][None] + p_im[:, :, :, None] * bb_re[d][None]
        return re.transpose(1, 0, 3, 2), im.transpose(1, 0, 3, 2)

    wf_re, wf_im = st(pf_re, pf_im, 0)
    wb_re, wb_im = st(pb_re, pb_im, 1)
    p = lr.shape[-1]
    wst = jnp.concatenate([wf_re, wb_re, wf_im, wb_im], axis=-1).reshape(g, t * kk, 4 * p)

    vf_pw_re, vf_pw_im = pw_re[1:, 0], pw_im[1:, 0]
    vb_pw_re, vb_pw_im = pw_re[1:, 1][::-1], pw_im[1:, 1][::-1]

    def rd(p_re, p_im, d):
        d_re = cr[d][None] * p_re[:, :, None, :] - ci[d][None] * p_im[:, :, None, :]
        d_im = cr[d][None] * p_im[:, :, None, :] + ci[d][None] * p_re[:, :, None, :]
        return d_re.transpose(1, 3, 0, 2), -d_im.transpose(1, 3, 0, 2)

    vf_re, vf_im = rd(vf_pw_re, vf_pw_im, 0)
    vb_re, vb_im = rd(vb_pw_re, vb_pw_im, 1)
    vrd = jnp.concatenate([vf_re, vb_re, vf_im, vb_im], axis=1).reshape(g, 4 * p, t * kk)

    a_re = jnp.concatenate([pw_re[t, 0], pw_re[t, 1]], axis=-1)[:, None, :]
    a_im = jnp.concatenate([pw_im[t, 0], pw_im[t, 1]], axis=-1)[:, None, :]
    return rpos.astype(BF16), rneg.astype(BF16), wst.astype(BF16), vrd.astype(BF16), a_re, a_im


SSM_GPB = LANES // SSM_GROUP


def _ssm_block_tables(rpos, rneg, wst, vrd, a_re, a_im):
    rpos, rneg, wst, vrd = lax.optimization_barrier((rpos, rneg, wst, vrd))
    g, _, tk = rpos.shape
    t, k, gpb = SSM_CHUNK, SSM_GROUP, SSM_GPB
    nb = g // gpb
    p2 = wst.shape[2] // 2
    tl, hw = t * gpb * k, gpb * p2
    rpos = rpos.reshape(nb, gpb * k, tk)
    rneg = rneg.reshape(nb, gpb * k, tk)
    rw = wst.reshape(nb, gpb, t, k, 2 * p2).transpose(0, 2, 1, 3, 4).reshape(nb, t, gpb * k, 2 * p2)
    rv = vrd.reshape(nb, gpb, 2, p2, tk).transpose(0, 2, 1, 3, 4).reshape(nb, 2, hw, tk)
    jj, kk = np.arange(tk) // k, np.arange(tk) % k
    sel = (jj[None, :, None] == np.arange(t)[:, None, None]) & (kk[None, :, None] == (np.arange(LANES) % k)[None, None, :])
    sel = jnp.asarray(sel, BF16)
    toep_b, wst_b, vrd_b = pl.pallas_call(
        functools.partial(_ssm_expand_kernel, t=t, k=k, p2=p2),
        out_shape=[jax.ShapeDtypeStruct((nb, tl, tl), BF16), jax.ShapeDtypeStruct((nb, tl, 2 * hw), BF16),
                   jax.ShapeDtypeStruct((nb, 2 * hw, tl), BF16)],
        grid=(nb, t),
        in_specs=[pl.BlockSpec((1, gpb * k, tk), lambda b, i: (b, 0, 0)),
                  pl.BlockSpec((1, gpb * k, tk), lambda b, i: (b, 0, 0)),
                  pl.BlockSpec((1, 1, gpb * k, 2 * p2), lambda b, i: (b, i, 0, 0)),
                  pl.BlockSpec((1, 2, hw, tk), lambda b, i: (b, 0, 0, 0)),
                  pl.BlockSpec(sel.shape, lambda b, i: (0, 0, 0))],
        out_specs=[pl.BlockSpec((1, gpb * k, tl), lambda b, i: (b, i, 0)),
                   pl.BlockSpec((1, gpb * k, 2 * hw), lambda b, i: (b, i, 0)),
                   pl.BlockSpec((1, 2 * hw, gpb * k), lambda b, i: (b, 0, i))],
        compiler_params=_cparams(("arbitrary", "arbitrary")),
        name="ssm_expand",
    )(rpos, rneg, rw, rv, sel)
    return toep_b, wst_b, vrd_b, a_re.reshape(nb, 1, hw), a_im.reshape(nb, 1, hw)


def _ssm_expand_kernel(rpos_ref, rneg_ref, rw_ref, rv_ref, sel_ref, toep_ref, wst_ref, vrd_ref, *, t, k, p2):
    i = pl.program_id(1)
    lb = rpos_ref.shape[1]
    hw = rv_ref.shape[2]
    gpb = lb // k
    same_g = (lax.broadcasted_iota(jnp.int32, (lb, lb), 0) // k) == (lax.broadcasted_iota(jnp.int32, (lb, lb), 1) // k)
    for j in range(t):
        pos = _dot(rpos_ref[0], sel_ref[jnp.clip(j - i, 0, t - 1)])
        neg = _dot(rneg_ref[0], sel_ref[jnp.clip(i - j, 0, t - 1)])
        blk = jnp.where(i <= j, pos, neg)
        toep_ref[0, :, j * lb:(j + 1) * lb] = jnp.where(same_g, blk, 0.0).astype(BF16)
    w = rw_ref[0, 0]
    same_gw = (lax.broadcasted_iota(jnp.int32, (lb, hw), 0) // k) == (lax.broadcasted_iota(jnp.int32, (lb, hw), 1) // p2)
    for r in range(2):
        wr = jnp.concatenate([w[:, r * p2:(r + 1) * p2]] * gpb, axis=-1)
        wst_ref[0, :, r * hw:(r + 1) * hw] = jnp.where(same_gw, wr, jnp.zeros_like(wr))
    same_gv = (lax.broadcasted_iota(jnp.int32, (hw, lb), 0) // p2) == (lax.broadcasted_iota(jnp.int32, (hw, lb), 1) // k)
    for r in range(2):
        v = _dot(rv_ref[0, r], sel_ref[i])
        vrd_ref[0, r * hw:(r + 1) * hw, :] = jnp.where(same_gv, v, 0.0).astype(BF16)


def _ssm_kernel(u_ref, ws_ref, ar_ref, ai_ref, h0_ref, *refs, nc, want_y):
    t = SSM_CHUNK
    if want_y:
        wt_ref, v_ref, d_ref, z_ref, hfin_ref, s_ref, yi_ref, hp_ref, tmp_ref, y_ref = refs
    else:
        hfin_ref, s_ref = refs
    a = jnp.concatenate([u_ref[0, pl.ds(j, nc, stride=t), :] for j in range(t)], axis=-1).astype(BF16)
    s = _dot(a, ws_ref[0])
    nlb = s.shape[1] // LANES
    gpb = nlb // 2
    for k in range(nlb):
        s_ref[pl.ds(k, nc, stride=nlb), :] = s[:, k * LANES:(k + 1) * LANES]
    if want_y:
        yi_ref[...] = _dot(a, wt_ref[0])

    ar = ar_ref[0]
    ai = ai_ref[0]
    fwd = lax.broadcasted_iota(jnp.int32, (gpb, LANES), 1) < (LANES // 2)

    def body(c, carry):
        h_re, h_im = carry
        rf = pl.multiple_of(c * nlb, nlb)
        rb = pl.multiple_of((nc - 1 - c) * nlb, nlb)
        if want_y:
            hp_ref[pl.ds(rf, gpb), :] = h_re
            hp_ref[pl.ds(rf + gpb, gpb), :] = h_im
            tmp_ref[pl.ds(rb, gpb), :] = h_re
            tmp_ref[pl.ds(rb + gpb, gpb), :] = h_im
        s_re = jnp.where(fwd, s_ref[pl.ds(rf, gpb), :], s_ref[pl.ds(rb, gpb), :])
        s_im = jnp.where(fwd, s_ref[pl.ds(rf + gpb, gpb), :], s_ref[pl.ds(rb + gpb, gpb), :])
        return ar * h_re - ai * h_im + s_re, ar * h_im + ai * h_re + s_im

    h0 = h0_ref[0, 0]
    h_re, h_im = lax.fori_loop(0, nc, body, (h0[:gpb], h0[gpb:]))
    hfin_ref[0, 0] = jnp.concatenate([h_re, h_im], axis=0)

    if want_y:
        fwd2 = lax.broadcasted_iota(jnp.int32, hp_ref.shape, 1) < (LANES // 2)
        hp_ref[...] = jnp.where(fwd2, hp_ref[...], tmp_ref[...])
        hp = jnp.concatenate([hp_ref[pl.ds(k, nc, stride=nlb), :] for k in range(nlb)], axis=-1).astype(BF16)
        y = yi_ref[...] + _dot(hp, v_ref[0])
        for j in range(t):
            y_ref[pl.ds(j, nc, stride=t), :] = y[:, j * LANES:(j + 1) * LANES]

        def rows(r0):
            r = pl.ds(r0, ROW_CHUNK)
            z_ref[0, r, :] = jax.nn.gelu(y_ref[r, :] + d_ref[...] * u_ref[0, r, :]).astype(BF16)

        _for_row_chunks(nc * t, rows)


def _ssm(u, tables, h0, d, *, want_y):
    toep_b, wst_b, vrd_b, ar, ai = tables
    bsz, n, _ = u.shape
    nb, tl, sw = wst_b.shape
    nc = n // SSM_CHUNK
    single = pl.Buffered(1)
    nlb = sw // LANES
    gpb = nlb // 2
    shapes = [jax.ShapeDtypeStruct((bsz, nb, nlb, LANES), F32)]
    specs = [pl.BlockSpec((1, 1, nlb, LANES), lambda k, b: (b, k, 0, 0))]
    scratch = [pltpu.VMEM((nc * nlb, LANES), F32)]
    in_specs = [pl.BlockSpec((1, n, LANES), lambda k, b: (b, 0, k)),
                pl.BlockSpec((1, tl, sw), lambda k, b: (k, 0, 0)),
                pl.BlockSpec((1, gpb, LANES), lambda k, b: (k, 0, 0)),
                pl.BlockSpec((1, gpb, LANES), lambda k, b: (k, 0, 0)),
                pl.BlockSpec((1, 1, nlb, LANES), lambda k, b: (b, k, 0, 0))]
    args = [u, wst_b, ar.reshape(nb, gpb, LANES), ai.reshape(nb, gpb, LANES), h0]
    if want_y:
        in_specs += [pl.BlockSpec((1, tl, tl), lambda k, b: (k, 0, 0)),
                     pl.BlockSpec((1, sw, tl), lambda k, b: (k, 0, 0), pipeline_mode=single),
                     pl.BlockSpec((1, LANES), lambda k, b: (0, k))]
        args += [toep_b, vrd_b, d]
        shapes.insert(0, jax.ShapeDtypeStruct((bsz, n, nb * LANES), BF16))
        specs.insert(0, pl.BlockSpec((1, n, LANES), lambda k, b: (b, 0, k)))
        scratch += [pltpu.VMEM((nc, tl), F32), pltpu.VMEM((nc * nlb, LANES), F32),
                    pltpu.VMEM((nc * nlb, LANES), F32), pltpu.VMEM((n, LANES), F32)]
    return pl.pallas_call(
        functools.partial(_ssm_kernel, nc=nc, want_y=want_y),
        out_shape=shapes,
        grid=(nb, bsz),
        in_specs=in_specs,
        out_specs=specs,
        scratch_shapes=scratch,
        compiler_params=_cparams(("arbitrary", "arbitrary")),
        name="ssm",
    )(*args)


def _glu_kernel(z_ref, w_ref, o_ref, *, ws):
    z = z_ref[0]
    a = _dot(z, w_ref[:, :ws])
    b = _dot(z, w_ref[:, ws:])
    o_ref[0] = (a * jax.nn.sigmoid(b)).astype(BF16)


def _glu(z, w_glu, *, tm):
    bsz, s, ws = z.shape
    tm = _tile(s, tm)
    return pl.pallas_call(
        functools.partial(_glu_kernel, ws=ws),
        out_shape=jax.ShapeDtypeStruct((bsz, s, ws), BF16),
        grid=(bsz, s // tm),
        in_specs=[pl.BlockSpec((1, tm, ws), lambda b, i: (b, i, 0)),
                  pl.BlockSpec(w_glu.shape, lambda b, i: (0, 0))],
        out_specs=pl.BlockSpec((1, tm, ws), lambda b, i: (b, i, 0)),
        compiler_params=_cparams(("arbitrary", "arbitrary")),
        name="glu",
    )(z, w_glu)


POOL_TILE = 4 * GRID_W


def _pool_consts(n):
    t = np.arange(POOL_TILE)
    pcs, invs = [], []
    r = np.arange(n) // GRID_W
    c = np.arange(n) % GRID_W
    rows = n // GRID_W
    for w in POOL_WINDOWS:
        lo, hi = w // 2, w - w // 2
        same_row = (t[:, None] // GRID_W) == (t[None, :] // GRID_W)
        dc = (t[None, :] % GRID_W) - (t[:, None] % GRID_W)
        pcs.append((same_row & (dc >= -lo) & (dc < hi)).astype(np.float32))
        cnt_r = np.minimum(r + hi, rows) - np.maximum(r - lo, 0)
        cnt_c = np.minimum(c + hi, GRID_W) - np.maximum(c - lo, 0)
        invs.append((1.0 / (cnt_r * cnt_c)).astype(np.float32)[:, None])
    return jnp.asarray(np.stack(pcs), BF16), jnp.asarray(np.stack(invs), F32)


def _pool_kernel(v_ref, pc_ref, inv_ref, pw_ref, sc_ref, o_ref, cs_ref, *, n, pad):
    wi = pl.program_id(0)
    nt = n // POOL_TILE
    ch = v_ref.shape[2]
    zeros = jnp.zeros((pad, ch), F32)
    cs_ref[pl.ds(0, pad), :] = zeros
    cs_ref[pl.ds(pad + n, pad), :] = zeros
    pc = pc_ref[0]

    def col_body(i, _):
        off = pl.multiple_of(i * POOL_TILE, POOL_TILE)
        v = v_ref[0, pl.ds(off, POOL_TILE), :]
        hi = v.astype(BF16)
        lo = (v - hi.astype(F32)).astype(BF16)
        cs_ref[pl.ds(pad + off, POOL_TILE), :] = _dot(pc, hi) + _dot(pc, lo)
        return 0

    lax.fori_loop(0, nt, col_body, 0)

    for k, w in enumerate(POOL_WINDOWS):
        @pl.when(wi == k)
        def _(w=w):
            def row_body(i, _):
                off = pl.multiple_of(i * POOL_TILE, POOL_TILE)
                acc = cs_ref[pl.ds(pad + off - (w // 2) * GRID_W, POOL_TILE), :]
                for dlt in range(-(w // 2) + 1, w - w // 2):
                    acc = acc + cs_ref[pl.ds(pad + off + dlt * GRID_W, POOL_TILE), :]
                mixed = acc * inv_ref[0, pl.ds(off, POOL_TILE), :] - v_ref[0, pl.ds(off, POOL_TILE), :]
                y = _dot(mixed.astype(BF16), pw_ref[0]) * sc_ref[0]
                o_ref[0, pl.ds(off, POOL_TILE), :] = y.astype(BF16)
                return 0

            lax.fori_loop(0, nt, row_body, 0)


def _pool(usp, col0, pool_w, pool_scale, pc, inv):
    bsz, n, _ = usp.shape
    nw, ch, _ = pool_w.shape
    pad = (max(POOL_WINDOWS) // 2) * GRID_W
    cb0 = col0 // ch
    return pl.pallas_call(
        functools.partial(_pool_kernel, n=n, pad=pad),
        out_shape=jax.ShapeDtypeStruct((bsz, n, nw * ch), BF16),
        grid=(nw, bsz),
        in_specs=[pl.BlockSpec((1, n, ch), lambda w, b: (b, 0, cb0 + w)),
                  pl.BlockSpec((1, POOL_TILE, POOL_TILE), lambda w, b: (w, 0, 0)),
                  pl.BlockSpec((1, n, 1), lambda w, b: (w, 0, 0)),
                  pl.BlockSpec((1, ch, ch), lambda w, b: (w, 0, 0)),
                  pl.BlockSpec((1, 1, ch), lambda w, b: (w, 0, 0))],
        out_specs=pl.BlockSpec((1, n, ch), lambda w, b: (b, 0, w)),
        scratch_shapes=[pltpu.VMEM((n + 2 * pad, ch), F32)],
        compiler_params=_cparams(("arbitrary", "arbitrary")),
        name="pool",
    )(usp, pc, inv, pool_w, pool_scale)


def kernel(x, c, ctx, c_ctx, w_mod, b_mod, norm_g, final_g, ffn1_w_in, ffn1_w_out, ffn2_w_in, ffn2_w_out,
           w_in, ssm_lambda_re, ssm_lambda_im, ssm_log_step, ssm_b_re, ssm_b_im, ssm_c_re, ssm_c_im, ssm_d,
           w_glu, w_branch_a, pool_w, pool_scale, w_branch_b, w_out):
    bsz, seq, d = x.shape
    assert w_mod.shape[0] == 1, "single-layer problem"
    ssm_w = ssm_d.shape[1]
    pool_width = pool_scale.shape[1]
    nw = len(POOL_WINDOWS)
    pch = pool_width // nw
    p = ssm_lambda_re.shape[-1]
    assert seq % POOL_TILE == 0 and ctx.shape[1] % SSM_CHUNK == 0 and ssm_w % LANES == 0

    f1_in, f1_out = ffn1_w_in[0].astype(BF16), ffn1_w_out[0].astype(BF16)
    w_s = w_in[0][:, :ssm_w].astype(BF16)
    late = [ffn2_w_in[0], ffn2_w_out[0], w_in[0], w_out[0]]
    wbb, wba, wglu, pw = (w_branch_b[0].astype(BF16), w_branch_a[0].astype(BF16), w_glu[0].astype(BF16),
                          pool_w[0].astype(BF16))
    psc = pool_scale[0].reshape(nw, 1, pch)
    ng = norm_g[0]
    fg = final_g.reshape(1, d)

    rows = -(-(bsz + 1) // 8) * 8
    c8 = jnp.zeros((rows, d), F32).at[:bsz].set(c).at[bsz].set(c_ctx)
    mods = _mod(c8, w_mod[0], b_mod, tn=512 if w_mod.shape[2] % 512 == 0 else w_mod.shape[2]).reshape(rows, N_MOD, d)
    lat = lambda b: b
    con = lambda b: bsz

    tables = _ssm_block_tables(*_ssm_tables(ssm_lambda_re[0], ssm_lambda_im[0], ssm_log_step[0],
                                            ssm_b_re[0], ssm_b_im[0], ssm_c_re[0], ssm_c_im[0]))

    nctx = ctx.shape[1]
    _, uc = _ffn(ctx.reshape(1, bsz * nctx, d), ng, mods, con, f1_in, f1_out, fg, gi=0, mi=0, tm=1024, tf=256,
                 next_pre=(1, 3))
    us_c = _mm(uc, w_s, F32, tm=1024, tn=512).reshape(bsz, nctx, ssm_w)
    assert 2 * p == LANES, "fwd | bwd states of one group fill one lane tile"
    h_ctx, = _ssm(us_c, tables, jnp.zeros((bsz, ssm_w // LANES, 2 * SSM_GPB, LANES), F32), ssm_d, want_y=False)

    tf = 256
    nsteps = bsz * (seq // _tile(seq, 1024)) * (f1_out.shape[0] // tf)
    plans = [_cast_plan(a, nsteps) for a in late]
    x1, u, *cast = _ffn(x, ng, mods, lat, f1_in, f1_out, fg, gi=0, mi=0, tm=1024, tf=tf, next_pre=(1, 3),
                        casts=[(a, pln) for a, pln in zip(late, plans) if pln is not None])
    cast = iter(cast)
    f2_in, f2_out, w_all, wo = [a.astype(BF16) if pln is None else next(cast) for a, pln in zip(late, plans)]
    usp = _mm(u, w_all, F32, tm=1024, tn=512, ncols=ssm_w + pool_width)
    gates = _mm(u, w_all, BF16, tm=1024, tn=512, col0=ssm_w + pool_width)

    z, _ = _ssm(usp, tables, h_ctx, ssm_d, want_y=True)
    ag = _glu(z, wglu, tm=512)

    pc, inv = _pool_consts(seq)
    yp = _pool(usp, ssm_w, pw, psc, pc, inv)

    x2 = _mixout(ag, yp, wba, wbb, gates, wo, x1, mods, lat, mi=5, tm=1024, tn=512)
    out, = _ffn(x2, ng, mods, lat, f2_in, f2_out, fg, gi=2, mi=6, tm=1024, tf=256, final_norm=True)
    return out
```

```python
import functools
import math

import jax
import jax.numpy as jnp
import numpy as np
from jax import lax
from jax.experimental import pallas as pl
from jax.experimental.pallas import tpu as pltpu

BF16 = jnp.bfloat16
F32 = jnp.float32

RMS_EPS = 1e-6
LAMBDA_RE_MAX = -1e-4
HALF = 0.5
N_MOD = 9
SSM_GROUP = 16
POOL_WINDOWS = (2, 4, 8, 16)
GRID_W = 64
SSM_CHUNK = 16

LANES = 128
BF16_ROWS = 16
V7X_VMEM_BYTES = 64 * 1024 * 1024
VMEM_LIMIT = 60 * 1024 * 1024


def _cparams(sem):
    return pltpu.CompilerParams(dimension_semantics=sem, vmem_limit_bytes=VMEM_LIMIT)


def _tile(n, pref):
    t = min(n, pref)
    while n % t:
        t //= 2
    return t


ROW_UNROLL = 1
ROW_CHUNK = 64


def _for_row_chunks(n, fn):
    def body(i, carry):
        fn(pl.multiple_of(i * ROW_CHUNK, ROW_CHUNK))
        return carry

    lax.fori_loop(0, n // ROW_CHUNK, body, 0, unroll=ROW_UNROLL)


NORM_ROWS = 16
NORM_UNROLL = 4


def _for_rows(n, step, fn):
    def body(i, carry):
        fn(pl.multiple_of(i * step, step))
        return carry

    lax.fori_loop(0, n // step, body, 0, unroll=NORM_UNROLL)


def _dot(a, b):
    return jnp.dot(a, b, preferred_element_type=F32)


def _mod_kernel(c_ref, w_ref, b_ref, o_ref):
    c = c_ref[...]
    s = (c * jax.nn.sigmoid(c)).astype(BF16)
    o_ref[...] = _dot(s, w_ref[...].astype(BF16)) + b_ref[...]


def _mod(c8, w_mod, b_mod, tn):
    d, n = w_mod.shape
    return pl.pallas_call(
        _mod_kernel,
        out_shape=jax.ShapeDtypeStruct((c8.shape[0], n), F32),
        grid=(n // tn,),
        in_specs=[pl.BlockSpec(c8.shape, lambda j: (0, 0)),
                  pl.BlockSpec((d, tn), lambda j: (0, j)),
                  pl.BlockSpec((1, tn), lambda j: (0, j))],
        out_specs=pl.BlockSpec((c8.shape[0], tn), lambda j: (0, j)),
        compiler_params=_cparams(("arbitrary",)),
        name="mod",
    )(c8, w_mod, b_mod)


FFN_STAGE_ROWS = 128


def _ffn_kernel(x_ref, g_ref, mod_ref, wg_ref, wu_ref, wo_ref, fg_ref, *refs,
                gi, mi, nf, nstage, final_norm, next_pre, cast_blocks):
    nc = len(cast_blocks)
    src_refs, refs = refs[:nc], refs[nc:]
    if next_pre is None:
        o_ref, refs = refs[0], refs[1:]
    else:
        o_ref, u_ref, refs = refs[0], refs[1], refs[2:]
    dst_refs, (pre_ref, acc_ref, vec_ref, r_ref) = refs[:nc], refs[nc:]
    f = pl.program_id(2)
    rc, d = x_ref.shape[1], x_ref.shape[2]
    nr = NORM_ROWS

    def bcast(v):
        return jnp.broadcast_to(v, (nr, d))

    def inv_rms(y):
        ss = jnp.sum(y * y, axis=-1, keepdims=True)
        return jnp.broadcast_to(lax.rsqrt(ss * (1.0 / d) + RMS_EPS), (nr, LANES))

    def lanes(r):
        return jnp.concatenate([r] * (d // LANES), axis=-1)

    @pl.when(f == 0)
    def _():
        vec_ref[0] = bcast(g_ref[pl.ds(gi, 1), :] * (1.0 + mod_ref[0, pl.ds(mi + 1, 1), :]))
        vec_ref[1] = bcast(mod_ref[0, pl.ds(mi, 1), :])
        vec_ref[2] = bcast(HALF * mod_ref[0, pl.ds(mi + 2, 1), :])
        if final_norm:
            vec_ref[3] = bcast(fg_ref[...])
        if next_pre is not None:
            gi2, mi2 = next_pre
            vec_ref[3] = bcast(g_ref[pl.ds(gi2, 1), :] * (1.0 + mod_ref[0, pl.ds(mi2 + 1, 1), :]))
            vec_ref[4] = bcast(mod_ref[0, pl.ds(mi2, 1), :])

    @pl.when(f < nstage)
    def _():
        base = pl.multiple_of(f * rc, rc)

        def stats(r0):
            r_ref[pl.ds(r0, nr), :] = inv_rms(x_ref[0, pl.ds(r0, nr), :])

        def apply(r0):
            pre = x_ref[0, pl.ds(r0, nr), :] * lanes(r_ref[pl.ds(r0, nr), :]) * vec_ref[0] + vec_ref[1]
            pre_ref[pl.ds(base + r0, nr), :] = pre.astype(BF16)
            acc_ref[pl.ds(base + r0, nr), :] = jnp.zeros((nr, d), F32)

        _for_rows(rc, nr, stats)
        _for_rows(rc, nr, apply)

    @pl.when((f >= nstage) & (f < nstage + nf))
    def _():
        p = pre_ref[...]
        gate = _dot(p, wg_ref[0])
        up = _dot(p, wu_ref[0])
        act = (gate * jax.nn.sigmoid(gate) * up).astype(BF16)
        acc_ref[...] += _dot(act, wo_ref[...])
        step = (pl.program_id(0) * pl.num_programs(1) + pl.program_id(1)) * nf + (f - nstage)
        for src, dst, (nblk, tile) in zip(src_refs, dst_refs, cast_blocks):
            @pl.when(step < nblk)
            def _(src=src, dst=dst, tile=tile):
                if tile is None:
                    dst[...] = src[...].astype(BF16)
                else:
                    for t in range(src.shape[1] // tile):
                        dst[t] = src[:, t * tile:(t + 1) * tile].astype(BF16)

    @pl.when(f >= nstage + nf)
    def _():
        base = pl.multiple_of((f - (nstage + nf)) * rc, rc)

        renorm = final_norm or next_pre is not None

        def residual(r0):
            r = pl.ds(r0, nr)
            y = x_ref[0, r, :] + vec_ref[2] * acc_ref[pl.ds(base + r0, nr), :]
            o_ref[0, r, :] = y
            if renorm:
                r_ref[r, :] = inv_rms(y)

        def norm(r0):
            r = pl.ds(r0, nr)
            yn = o_ref[0, r, :] * lanes(r_ref[r, :])
            if final_norm:
                o_ref[0, r, :] = yn * vec_ref[3]
            if next_pre is not None:
                u_ref[0, r, :] = (yn * vec_ref[3] + vec_ref[4]).astype(BF16)

        _for_rows(rc, nr, residual)
        if renorm:
            _for_rows(rc, nr, norm)


def _cast_plan(arr, nsteps, col_multiple=LANES):
    r, c = arr.shape
    for parts in (8, 4, 2, 1):
        if c % (parts * col_multiple):
            continue
        br = BF16_ROWS
        while br <= r:
            if r % br == 0 and (r // br) * parts <= nsteps:
                return br, c // parts
            br *= 2
    return None


def _ffn(x, norm_g, mods, mod_row, w_in, w_out, final_g, *, gi, mi, tm, tf, final_norm=False, next_pre=None,
         casts=()):
    bsz, s, d = x.shape
    ff = w_out.shape[0]
    nf = ff // tf
    assert w_in.shape == (2 * nf, d, tf)
    tm = _tile(s, tm)
    rc = _tile(tm, FFN_STAGE_ROWS)
    nstage = tm // rc
    ni = s // tm

    def hid(f):
        return jnp.clip(f - nstage, 0, nf - 1)

    cast_blocks, cast_in, cast_out, cast_shapes = [], [], [], []
    for arr, (br, bc), tile in casts:
        nrb, ncb = arr.shape[0] // br, arr.shape[1] // bc
        assert nrb * br == arr.shape[0] and ncb * bc == arr.shape[1] and nrb * ncb <= bsz * ni * nf

        def blk(b, i, f, nrb=nrb, ncb=ncb):
            step = jnp.minimum((b * ni + i) * nf + hid(f), nrb * ncb - 1)
            return step // ncb, step % ncb

        cast_blocks.append((nrb * ncb, tile))
        cast_in.append(pl.BlockSpec((br, bc), blk))
        if tile is None:
            cast_out.append(pl.BlockSpec((br, bc), blk))
            cast_shapes.append(jax.ShapeDtypeStruct(arr.shape, BF16))
        else:
            assert bc % tile == 0
            cast_out.append(pl.BlockSpec((bc // tile, br, tile), lambda b, i, f, blk=blk: (blk(b, i, f)[1], blk(b, i, f)[0], 0)))
            cast_shapes.append(jax.ShapeDtypeStruct((arr.shape[1] // tile, arr.shape[0], tile), BF16))
    kern = functools.partial(_ffn_kernel, gi=gi, mi=mi, nf=nf, nstage=nstage, final_norm=final_norm,
                             next_pre=next_pre, cast_blocks=tuple(cast_blocks))

    def x_blk(b, i, f):
        return b, i * nstage + jnp.where(f < nstage, f, jnp.clip(f - (nstage + nf), 0, nstage - 1)), 0

    def o_blk(b, i, f):
        return b, i * nstage + jnp.clip(f - (nstage + nf), 0, nstage - 1), 0

    out_shape = [jax.ShapeDtypeStruct((bsz, s, d), F32)]
    out_specs = [pl.BlockSpec((1, rc, d), o_blk)]
    if next_pre is not None:
        out_shape.append(jax.ShapeDtypeStruct((bsz, s, d), BF16))
        out_specs.append(pl.BlockSpec((1, rc, d), o_blk))
    return pl.pallas_call(
        kern,
        out_shape=out_shape + cast_shapes,
        grid=(bsz, ni, nf + 2 * nstage),
        in_specs=[pl.BlockSpec((1, rc, d), x_blk),
                  pl.BlockSpec(norm_g.shape, lambda b, i, f: (0, 0)),
                  pl.BlockSpec((1, N_MOD, d), lambda b, i, f: (mod_row(b), 0, 0)),
                  pl.BlockSpec((1, d, tf), lambda b, i, f: (hid(f), 0, 0)),
                  pl.BlockSpec((1, d, tf), lambda b, i, f: (nf + hid(f), 0, 0)),
                  pl.BlockSpec((tf, d), lambda b, i, f: (hid(f), 0)),
                  pl.BlockSpec((1, d), lambda b, i, f: (0, 0))] + cast_in,
        out_specs=out_specs + cast_out,
        scratch_shapes=[pltpu.VMEM((tm, d), BF16), pltpu.VMEM((tm, d), F32),
                        pltpu.VMEM((5, NORM_ROWS, d), F32), pltpu.VMEM((rc, LANES), F32)],
        compiler_params=_cparams(("arbitrary", "arbitrary", "arbitrary")),
        name="ffn",
    )(x, norm_g, mods, w_in, w_in, w_out, final_g, *[c[0] for c in casts])


def _mm_kernel(a_ref, b_ref, o_ref):
    o_ref[0] = _dot(a_ref[0], b_ref[...]).astype(o_ref.dtype)


def _mm(a, w, out_dtype, *, tm, tn, col0=0, ncols=None):
    bsz, s, k = a.shape
    n = w.shape[1] - col0 if ncols is None else ncols
    tm, tn = _tile(s, tm), _tile(math.gcd(n, col0) if col0 else n, tn)
    j0 = col0 // tn
    return pl.pallas_call(
        _mm_kernel,
        out_shape=jax.ShapeDtypeStruct((bsz, s, n), out_dtype),
        grid=(bsz, s // tm, n // tn),
        in_specs=[pl.BlockSpec((1, tm, k), lambda b, i, j: (b, i, 0)),
                  pl.BlockSpec((k, tn), lambda b, i, j: (0, j0 + j))],
        out_specs=pl.BlockSpec((1, tm, tn), lambda b, i, j: (b, i, j)),
        compiler_params=_cparams(("arbitrary", "arbitrary", "arbitrary")),
        name="mm",
    )(a, w)


def _mixout_kernel(a_ref, p_ref, wa_ref, wb_ref, ga_ref, gb_ref, wo_ref, x_ref, mod_ref, o_ref, m_ref, *, mi, nj):
    j = pl.program_id(2)
    tn = wa_ref.shape[1]

    @pl.when(j < nj)
    def _():
        ya = _dot(a_ref[0], wa_ref[...])
        yb = _dot(p_ref[0], wb_ref[...])
        m = jax.nn.sigmoid(ga_ref[0].astype(F32)) * ya + jax.nn.sigmoid(gb_ref[0].astype(F32)) * yb
        m_ref[:, pl.ds(pl.multiple_of(j * tn, tn), tn)] = m.astype(BF16)

    @pl.when(j >= nj)
    def _():
        o_ref[0] = x_ref[0] + mod_ref[0, pl.ds(mi, 1), :] * _dot(m_ref[...], wo_ref[...])


def _mixout(ag, yp, wa, wb, gates, wo, x, mods, mod_row, *, mi, tm, tn):
    bsz, s, ka = ag.shape
    kb = yp.shape[2]
    d = wa.shape[1]
    tm, tn = _tile(s, tm), _tile(d, tn)
    nj = d // tn

    def first(j):
        return jnp.minimum(j, nj - 1)

    def second(j):
        return jnp.maximum(j - nj, 0)

    return pl.pallas_call(
        functools.partial(_mixout_kernel, mi=mi, nj=nj),
        out_shape=jax.ShapeDtypeStruct((bsz, s, d), F32),
        grid=(bsz, s // tm, 2 * nj),
        in_specs=[pl.BlockSpec((1, tm, ka), lambda b, i, j: (b, i, 0)),
                  pl.BlockSpec((1, tm, kb), lambda b, i, j: (b, i, 0)),
                  pl.BlockSpec((ka, tn), lambda b, i, j: (0, first(j))),
                  pl.BlockSpec((kb, tn), lambda b, i, j: (0, first(j))),
                  pl.BlockSpec((1, tm, tn), lambda b, i, j: (b, i, first(j))),
                  pl.BlockSpec((1, tm, tn), lambda b, i, j: (b, i, nj + first(j))),
                  pl.BlockSpec((d, tn), lambda b, i, j: (0, second(j))),
                  pl.BlockSpec((1, tm, tn), lambda b, i, j: (b, i, second(j))),
                  pl.BlockSpec((1, N_MOD, tn), lambda b, i, j: (mod_row(b), 0, second(j)))],
        out_specs=pl.BlockSpec((1, tm, tn), lambda b, i, j: (b, i, second(j))),
        scratch_shapes=[pltpu.VMEM((tm, d), BF16)],
        compiler_params=_cparams(("arbitrary", "arbitrary", "arbitrary")),
        name="mixout",
    )(ag, yp, wa, wb, gates, gates, wo, x, mods)


def _ssm_tables(lam_re, lam_im, log_step, b_re, b_im, c_re, c_im):
    t = SSM_CHUNK
    hp = lax.Precision.HIGHEST
    lr = jnp.minimum(lam_re.astype(F32), LAMBDA_RE_MAX)
    li = lam_im.astype(F32)
    step = jnp.exp(log_step.astype(F32))[..., None]
    m = jnp.arange(t + 1, dtype=F32)[:, None, None, None]
    mag = jnp.exp(m * (lr * step)[None])
    ang = m * (li * step)[None]
    pw_re, pw_im = mag * jnp.cos(ang), mag * jnp.sin(ang)
    nr, ni = pw_re[1] - 1.0, pw_im[1]
    den = lr * lr + li * li
    q_re, q_im = (nr * lr + ni * li) / den, (ni * lr - nr * li) / den
    bb_re = q_re[..., None] * b_re - q_im[..., None] * b_im
    bb_im = q_re[..., None] * b_im + q_im[..., None] * b_re
    cr, ci = c_re.astype(F32), c_im.astype(F32)

    cl_re = cr[:, :, None] * pw_re[:t].transpose(1, 2, 0, 3)[:, :, :, None, :] \
        - ci[:, :, None] * pw_im[:t].transpose(1, 2, 0, 3)[:, :, :, None, :]
    cl_im = cr[:, :, None] * pw_im[:t].transpose(1, 2, 0, 3)[:, :, :, None, :] \
        + ci[:, :, None] * pw_re[:t].transpose(1, 2, 0, 3)[:, :, :, None, :]
    kern = jnp.einsum('dgtkp,dgpq->dgtkq', cl_re, bb_re, precision=hp) \
        - jnp.einsum('dgtkp,dgpq->dgtkq', cl_im, bb_im, precision=hp)
    g, kk = kern.shape[1], kern.shape[3]
    zero = (jnp.arange(t) == 0).astype(F32)[None, :, None, None]
    kpos = kern[0] + zero * kern[1]
    kneg = kern[1] + zero * kern[0]
    rpos = kpos.transpose(0, 3, 1, 2).reshape(g, kk, t * kk)
    rneg = kneg.transpose(0, 3, 1, 2).reshape(g, kk, t * kk)

    pf_re, pf_im = pw_re[:t][::-1, 0], pw_im[:t][::-1, 0]
    pb_re, pb_im = pw_re[:t, 1], pw_im[:t, 1]

    def st(p_re, p_im, d):
        re = p_re[:, :, :, None] * bb_re[d][None] - p_im[:, :, :, None] * bb_im[d][None]
        im = p_re[:, :, :, None] * bb_im[d][None] + p_im[:, :, :, None] * bb_re[d][None]
        return re.transpose(1, 0, 3, 2), im.transpose(1, 0, 3, 2)

    wf_re, wf_im = st(pf_re, pf_im, 0)
    wb_re, wb_im = st(pb_re, pb_im, 1)
    p = lr.shape[-1]
    wst = jnp.concatenate([wf_re, wb_re, wf_im, wb_im], axis=-1).reshape(g, t * kk, 4 * p)

    vf_pw_re, vf_pw_im = pw_re[1:, 0], pw_im[1:, 0]
    vb_pw_re, vb_pw_im = pw_re[1:, 1][::-1], pw_im[1:, 1][::-1]

    def rd(p_re, p_im, d):
        d_re = cr[d][None] * p_re[:, :, None, :] - ci[d][None] * p_im[:, :, None, :]
        d_im = cr[d][None] * p_im[:, :, None, :] + ci[d][None] * p_re[:, :, None, :]
        return d_re.transpose(1, 3, 0, 2), -d_im.transpose(1, 3, 0, 2)

    vf_re, vf_im = rd(vf_pw_re, vf_pw_im, 0)
    vb_re, vb_im = rd(vb_pw_re, vb_pw_im, 1)
    vrd = jnp.concatenate([vf_re, vb_re, vf_im, vb_im], axis=1).reshape(g, 4 * p, t * kk)

    a_re = jnp.concatenate([pw_re[t, 0], pw_re[t, 1]], axis=-1)[:, None, :]
    a_im = jnp.concatenate([pw_im[t, 0], pw_im[t, 1]], axis=-1)[:, None, :]
    return rpos.astype(BF16), rneg.astype(BF16), wst.astype(BF16), vrd.astype(BF16), a_re, a_im


SSM_GPB = LANES // SSM_GROUP


def _ssm_block_tables(rpos, rneg, wst, vrd, a_re, a_im):
    rpos, rneg, wst, vrd = lax.optimization_barrier((rpos, rneg, wst, vrd))
    g, _, tk = rpos.shape
    t, k, gpb = SSM_CHUNK, SSM_GROUP, SSM_GPB
    nb = g // gpb
    p2 = wst.shape[2] // 2
    tl, hw = t * gpb * k, gpb * p2
    rpos = rpos.reshape(nb, gpb * k, tk)
    rneg = rneg.reshape(nb, gpb * k, tk)
    rw = wst.reshape(nb, gpb, t, k, 2 * p2).transpose(0, 2, 1, 3, 4).reshape(nb, t, gpb * k, 2 * p2)
    rv = vrd.reshape(nb, gpb, 2, p2, tk).transpose(0, 2, 1, 3, 4).reshape(nb, 2, hw, tk)
    jj, kk = np.arange(tk) // k, np.arange(tk) % k
    sel = (jj[None, :, None] == np.arange(t)[:, None, None]) & (kk[None, :, None] == (np.arange(LANES) % k)[None, None, :])
    sel = jnp.asarray(sel, BF16)
    toep_b, wst_b, vrd_b = pl.pallas_call(
        functools.partial(_ssm_expand_kernel, t=t, k=k, p2=p2),
        out_shape=[jax.ShapeDtypeStruct((nb, tl, tl), BF16), jax.ShapeDtypeStruct((nb, tl, 2 * hw), BF16),
                   jax.ShapeDtypeStruct((nb, 2 * hw, tl), BF16)],
        grid=(nb, t),
        in_specs=[pl.BlockSpec((1, gpb * k, tk), lambda b, i: (b, 0, 0)),
                  pl.BlockSpec((1, gpb * k, tk), lambda b, i: (b, 0, 0)),
                  pl.BlockSpec((1, 1, gpb * k, 2 * p2), lambda b, i: (b, i, 0, 0)),
                  pl.BlockSpec((1, 2, hw, tk), lambda b, i: (b, 0, 0, 0)),
                  pl.BlockSpec(sel.shape, lambda b, i: (0, 0, 0))],
        out_specs=[pl.BlockSpec((1, gpb * k, tl), lambda b, i: (b, i, 0)),
                   pl.BlockSpec((1, gpb * k, 2 * hw), lambda b, i: (b, i, 0)),
                   pl.BlockSpec((1, 2 * hw, gpb * k), lambda b, i: (b, 0, i))],
        compiler_params=_cparams(("arbitrary", "arbitrary")),
        name="ssm_expand",
    )(rpos, rneg, rw, rv, sel)
    return toep_b, wst_b, vrd_b, a_re.reshape(nb, 1, hw), a_im.reshape(nb, 1, hw)


def _ssm_expand_kernel(rpos_ref, rneg_ref, rw_ref, rv_ref, sel_ref, toep_ref, wst_ref, vrd_ref, *, t, k, p2):
    i = pl.program_id(1)
    lb = rpos_ref.shape[1]
    hw = rv_ref.shape[2]
    gpb = lb // k
    same_g = (lax.broadcasted_iota(jnp.int32, (lb, lb), 0) // k) == (lax.broadcasted_iota(jnp.int32, (lb, lb), 1) // k)
    for j in range(t):
        pos = _dot(rpos_ref[0], sel_ref[jnp.clip(j - i, 0, t - 1)])
        neg = _dot(rneg_ref[0], sel_ref[jnp.clip(i - j, 0, t - 1)])
        blk = jnp.where(i <= j, pos, neg)
        toep_ref[0, :, j * lb:(j + 1) * lb] = jnp.where(same_g, blk, 0.0).astype(BF16)
    w = rw_ref[0, 0]
    same_gw = (lax.broadcasted_iota(jnp.int32, (lb, hw), 0) // k) == (lax.broadcasted_iota(jnp.int32, (lb, hw), 1) // p2)
    for r in range(2):
        wr = jnp.concatenate([w[:, r * p2:(r + 1) * p2]] * gpb, axis=-1)
        wst_ref[0, :, r * hw:(r + 1) * hw] = jnp.where(same_gw, wr, jnp.zeros_like(wr))
    same_gv = (lax.broadcasted_iota(jnp.int32, (hw, lb), 0) // p2) == (lax.broadcasted_iota(jnp.int32, (hw, lb), 1) // k)
    for r in range(2):
        v = _dot(rv_ref[0, r], sel_ref[i])
        vrd_ref[0, r * hw:(r + 1) * hw, :] = jnp.where(same_gv, v, 0.0).astype(BF16)


def _ssm_kernel(u_ref, ws_ref, ar_ref, ai_ref, h0_ref, *refs, nc, want_y):
    t = SSM_CHUNK
    if want_y:
        wt_ref, v_ref, d_ref, z_ref, hfin_ref, s_ref, yi_ref, hp_ref, tmp_ref, y_ref = refs
    else:
        hfin_ref, s_ref = refs
    a = jnp.concatenate([u_ref[0, pl.ds(j, nc, stride=t), :] for j in range(t)], axis=-1).astype(BF16)
    s = _dot(a, ws_ref[0])
    nlb = s.shape[1] // LANES
    gpb = nlb // 2
    for k in range(nlb):
        s_ref[pl.ds(k, nc, stride=nlb), :] = s[:, k * LANES:(k + 1) * LANES]
    if want_y:
        yi_ref[...] = _dot(a, wt_ref[0])

    ar = ar_ref[0]
    ai = ai_ref[0]
    fwd = lax.broadcasted_iota(jnp.int32, (gpb, LANES), 1) < (LANES // 2)

    def body(c, carry):
        h_re, h_im = carry
        rf = pl.multiple_of(c * nlb, nlb)
        rb = pl.multiple_of((nc - 1 - c) * nlb, nlb)
        if want_y:
            hp_ref[pl.ds(rf, gpb), :] = h_re
            hp_ref[pl.ds(rf + gpb, gpb), :] = h_im
            tmp_ref[pl.ds(rb, gpb), :] = h_re
            tmp_ref[pl.ds(rb + gpb, gpb), :] = h_im
        s_re = jnp.where(fwd, s_ref[pl.ds(rf, gpb), :], s_ref[pl.ds(rb, gpb), :])
        s_im = jnp.where(fwd, s_ref[pl.ds(rf + gpb, gpb), :], s_ref[pl.ds(rb + gpb, gpb), :])
        return ar * h_re - ai * h_im + s_re, ar * h_im + ai * h_re + s_im

    h0 = h0_ref[0, 0]
    h_re, h_im = lax.fori_loop(0, nc, body, (h0[:gpb], h0[gpb:]))
    hfin_ref[0, 0] = jnp.concatenate([h_re, h_im], axis=0)

    if want_y:
        fwd2 = lax.broadcasted_iota(jnp.int32, hp_ref.shape, 1) < (LANES // 2)
        hp_ref[...] = jnp.where(fwd2, hp_ref[...], tmp_ref[...])
        hp = jnp.concatenate([hp_ref[pl.ds(k, nc, stride=nlb), :] for k in range(nlb)], axis=-1).astype(BF16)
        y = yi_ref[...] + _dot(hp, v_ref[0])
        for j in range(t):
            y_ref[pl.ds(j, nc, stride=t), :] = y[:, j * LANES:(j + 1) * LANES]

        def rows(r0):
            r = pl.ds(r0, ROW_CHUNK)
            z_ref[0, r, :] = jax.nn.gelu(y_ref[r, :] + d_ref[...] * u_ref[0, r, :]).astype(BF16)

        _for_row_chunks(nc * t, rows)


def _ssm(u, tables, h0, d, *, want_y):
    toep_b, wst_b, vrd_b, ar, ai = tables
    bsz, n, _ = u.shape
    nb, tl, sw = wst_b.shape
    nc = n // SSM_CHUNK
    single = pl.Buffered(1)
    nlb = sw // LANES
    gpb = nlb // 2
    shapes = [jax.ShapeDtypeStruct((bsz, nb, nlb, LANES), F32)]
    specs = [pl.BlockSpec((1, 1, nlb, LANES), lambda k, b: (b, k, 0, 0))]
    scratch = [pltpu.VMEM((nc * nlb, LANES), F32)]
    in_specs = [pl.BlockSpec((1, n, LANES), lambda k, b: (b, 0, k)),
                pl.BlockSpec((1, tl, sw), lambda k, b: (k, 0, 0)),
                pl.BlockSpec((1, gpb, LANES), lambda k, b: (k, 0, 0)),
                pl.BlockSpec((1, gpb, LANES), lambda k, b: (k, 0, 0)),
                pl.BlockSpec((1, 1, nlb, LANES), lambda k, b: (b, k, 0, 0))]
    args = [u, wst_b, ar.reshape(nb, gpb, LANES), ai.reshape(nb, gpb, LANES), h0]
    if want_y:
        in_specs += [pl.BlockSpec((1, tl, tl), lambda k, b: (k, 0, 0)),
                     pl.BlockSpec((1, sw, tl), lambda k, b: (k, 0, 0), pipeline_mode=single),
                     pl.BlockSpec((1, LANES), lambda k, b: (0, k))]
        args += [toep_b, vrd_b, d]
        shapes.insert(0, jax.ShapeDtypeStruct((bsz, n, nb * LANES), BF16))
        specs.insert(0, pl.BlockSpec((1, n, LANES), lambda k, b: (b, 0, k)))
        scratch += [pltpu.VMEM((nc, tl), F32), pltpu.VMEM((nc * nlb, LANES), F32),
                    pltpu.VMEM((nc * nlb, LANES), F32), pltpu.VMEM((n, LANES), F32)]
    return pl.pallas_call(
        functools.partial(_ssm_kernel, nc=nc, want_y=want_y),
        out_shape=shapes,
        grid=(nb, bsz),
        in_specs=in_specs,
        out_specs=specs,
        scratch_shapes=scratch,
        compiler_params=_cparams(("arbitrary", "arbitrary")),
        name="ssm",
    )(*args)


def _glu_kernel(z_ref, w_ref, o_ref, *, ws):
    z = z_ref[0]
    a = _dot(z, w_ref[:, :ws])
    b = _dot(z, w_ref[:, ws:])
    o_ref[0] = (a * jax.nn.sigmoid(b)).astype(BF16)


def _glu(z, w_glu, *, tm):
    bsz, s, ws = z.shape
    tm = _tile(s, tm)
    return pl.pallas_call(
        functools.partial(_glu_kernel, ws=ws),
        out_shape=jax.ShapeDtypeStruct((bsz, s, ws), BF16),
        grid=(bsz, s // tm),
        in_specs=[pl.BlockSpec((1, tm, ws), lambda b, i: (b, i, 0)),
                  pl.BlockSpec(w_glu.shape, lambda b, i: (0, 0))],
        out_specs=pl.BlockSpec((1, tm, ws), lambda b, i: (b, i, 0)),
        compiler_params=_cparams(("arbitrary", "arbitrary")),
        name="glu",
    )(z, w_glu)


POOL_TILE = 4 * GRID_W


def _pool_consts(n):
    t = np.arange(POOL_TILE)
    pcs, invs = [], []
    r = np.arange(n) // GRID_W
    c = np.arange(n) % GRID_W
    rows = n // GRID_W
    for w in POOL_WINDOWS:
        lo, hi = w // 2, w - w // 2
        same_row = (t[:, None] // GRID_W) == (t[None, :] // GRID_W)
        dc = (t[None, :] % GRID_W) - (t[:, None] % GRID_W)
        pcs.append((same_row & (dc >= -lo) & (dc < hi)).astype(np.float32))
        cnt_r = np.minimum(r + hi, rows) - np.maximum(r - lo, 0)
        cnt_c = np.minimum(c + hi, GRID_W) - np.maximum(c - lo, 0)
        invs.append((1.0 / (cnt_r * cnt_c)).astype(np.float32)[:, None])
    return jnp.asarray(np.stack(pcs), BF16), jnp.asarray(np.stack(invs), F32)


def _pool_kernel(v_ref, pc_ref, inv_ref, pw_ref, sc_ref, o_ref, cs_ref, *, n, pad):
    wi = pl.program_id(0)
    nt = n // POOL_TILE
    ch = v_ref.shape[2]
    zeros = jnp.zeros((pad, ch), F32)
    cs_ref[pl.ds(0, pad), :] = zeros
    cs_ref[pl.ds(pad + n, pad), :] = zeros
    pc = pc_ref[0]

    def col_body(i, _):
        off = pl.multiple_of(i * POOL_TILE, POOL_TILE)
        v = v_ref[0, pl.ds(off, POOL_TILE), :]
        hi = v.astype(BF16)
        lo = (v - hi.astype(F32)).astype(BF16)
        cs_ref[pl.ds(pad + off, POOL_TILE), :] = _dot(pc, hi) + _dot(pc, lo)
        return 0

    lax.fori_loop(0, nt, col_body, 0)

    for k, w in enumerate(POOL_WINDOWS):
        @pl.when(wi == k)
        def _(w=w):
            def row_body(i, _):
                off = pl.multiple_of(i * POOL_TILE, POOL_TILE)
                acc = cs_ref[pl.ds(pad + off - (w // 2) * GRID_W, POOL_TILE), :]
                for dlt in range(-(w // 2) + 1, w - w // 2):
                    acc = acc + cs_ref[pl.ds(pad + off + dlt * GRID_W, POOL_TILE), :]
                mixed = acc * inv_ref[0, pl.ds(off, POOL_TILE), :] - v_ref[0, pl.ds(off, POOL_TILE), :]
                y = _dot(mixed.astype(BF16), pw_ref[0]) * sc_ref[0]
                o_ref[0, pl.ds(off, POOL_TILE), :] = y.astype(BF16)
                return 0

            lax.fori_loop(0, nt, row_body, 0)


def _pool(usp, col0, pool_w, pool_scale, pc, inv):
    bsz, n, _ = usp.shape
    nw, ch, _ = pool_w.shape
    pad = (max(POOL_WINDOWS) // 2) * GRID_W
    cb0 = col0 // ch
    return pl.pallas_call(
        functools.partial(_pool_kernel, n=n, pad=pad),
        out_shape=jax.ShapeDtypeStruct((bsz, n, nw * ch), BF16),
        grid=(nw, bsz),
        in_specs=[pl.BlockSpec((1, n, ch), lambda w, b: (b, 0, cb0 + w)),
                  pl.BlockSpec((1, POOL_TILE, POOL_TILE), lambda w, b: (w, 0, 0)),
                  pl.BlockSpec((1, n, 1), lambda w, b: (w, 0, 0)),
                  pl.BlockSpec((1, ch, ch), lambda w, b: (w, 0, 0)),
                  pl.BlockSpec((1, 1, ch), lambda w, b: (w, 0, 0))],
        out_specs=pl.BlockSpec((1, n, ch), lambda w, b: (b, 0, w)),
        scratch_shapes=[pltpu.VMEM((n + 2 * pad, ch), F32)],
        compiler_params=_cparams(("arbitrary", "arbitrary")),
        name="pool",
    )(usp, pc, inv, pool_w, pool_scale)


def kernel(x, c, ctx, c_ctx, w_mod, b_mod, norm_g, final_g, ffn1_w_in, ffn1_w_out, ffn2_w_in, ffn2_w_out,
           w_in, ssm_lambda_re, ssm_lambda_im, ssm_log_step, ssm_b_re, ssm_b_im, ssm_c_re, ssm_c_im, ssm_d,
           w_glu, w_branch_a, pool_w, pool_scale, w_branch_b, w_out):
    bsz, seq, d = x.shape
    assert w_mod.shape[0] == 1, "single-layer problem"
    ssm_w = ssm_d.shape[1]
    pool_width = pool_scale.shape[1]
    nw = len(POOL_WINDOWS)
    pch = pool_width // nw
    p = ssm_lambda_re.shape[-1]
    assert seq % POOL_TILE == 0 and ctx.shape[1] % SSM_CHUNK == 0 and ssm_w % LANES == 0

    tf = 256

    def col_tiles(w):
        return w.reshape(w.shape[0], w.shape[1] // tf, tf).transpose(1, 0, 2)

    f1_in, f1_out = col_tiles(ffn1_w_in[0].astype(BF16)), ffn1_w_out[0].astype(BF16)
    w_s = w_in[0][:, :ssm_w].astype(BF16)
    late = [ffn2_w_in[0], ffn2_w_out[0], w_in[0], w_out[0]]
    late_tile = [tf, None, None, None]
    wbb, wba, wglu, pw = (w_branch_b[0].astype(BF16), w_branch_a[0].astype(BF16), w_glu[0].astype(BF16),
                          pool_w[0].astype(BF16))
    psc = pool_scale[0].reshape(nw, 1, pch)
    ng = norm_g[0]
    fg = final_g.reshape(1, d)

    rows = -(-(bsz + 1) // 8) * 8
    c8 = jnp.zeros((rows, d), F32).at[:bsz].set(c).at[bsz].set(c_ctx)
    mods = _mod(c8, w_mod[0], b_mod, tn=512 if w_mod.shape[2] % 512 == 0 else w_mod.shape[2]).reshape(rows, N_MOD, d)
    lat = lambda b: b
    con = lambda b: bsz

    tables = _ssm_block_tables(*_ssm_tables(ssm_lambda_re[0], ssm_lambda_im[0], ssm_log_step[0],
                                            ssm_b_re[0], ssm_b_im[0], ssm_c_re[0], ssm_c_im[0]))

    nctx = ctx.shape[1]
    _, uc = _ffn(ctx.reshape(1, bsz * nctx, d), ng, mods, con, f1_in, f1_out, fg, gi=0, mi=0, tm=1024, tf=tf,
                 next_pre=(1, 3))
    us_c = _mm(uc, w_s, F32, tm=1024, tn=512).reshape(bsz, nctx, ssm_w)
    assert 2 * p == LANES, "fwd | bwd states of one group fill one lane tile"
    h_ctx, = _ssm(us_c, tables, jnp.zeros((bsz, ssm_w // LANES, 2 * SSM_GPB, LANES), F32), ssm_d, want_y=False)

    nsteps = bsz * (seq // _tile(seq, 1024)) * (f1_out.shape[0] // tf)
    plans = [_cast_plan(a, nsteps, LANES if t is None else t) for a, t in zip(late, late_tile)]
    x1, u, *cast = _ffn(x, ng, mods, lat, f1_in, f1_out, fg, gi=0, mi=0, tm=1024, tf=tf, next_pre=(1, 3),
                        casts=[(a, pln, t) for a, pln, t in zip(late, plans, late_tile) if pln is not None])
    cast = iter(cast)
    f2_in, f2_out, w_all, wo = [
        (a.astype(BF16) if t is None else col_tiles(a.astype(BF16))) if pln is None else next(cast)
        for a, pln, t in zip(late, plans, late_tile)]
    usp = _mm(u, w_all, F32, tm=1024, tn=512, ncols=ssm_w + pool_width)
    gates = _mm(u, w_all, BF16, tm=1024, tn=512, col0=ssm_w + pool_width)

    z, _ = _ssm(usp, tables, h_ctx, ssm_d, want_y=True)
    ag = _glu(z, wglu, tm=512)

    pc, inv = _pool_consts(seq)
    yp = _pool(usp, ssm_w, pw, psc, pc, inv)

    x2 = _mixout(ag, yp, wba, wbb, gates, wo, x1, mods, lat, mi=5, tm=1024, tn=512)
    out, = _ffn(x2, ng, mods, lat, f2_in, f2_out, fg, gi=2, mi=6, tm=1024, tf=tf, final_norm=True)
    return out
```

```python
import functools
import math

import jax
import jax.numpy as jnp
import numpy as np
from jax import lax
from jax.experimental import pallas as pl
from jax.experimental.pallas import tpu as pltpu

BF16 = jnp.bfloat16
F32 = jnp.float32

RMS_EPS = 1e-6
LAMBDA_RE_MAX = -1e-4
HALF = 0.5
N_MOD = 9
SSM_GROUP = 16
POOL_WINDOWS = (2, 4, 8, 16)
GRID_W = 64
SSM_CHUNK = 16

LANES = 128
BF16_ROWS = 16
V7X_VMEM_BYTES = 64 * 1024 * 1024
VMEM_LIMIT = V7X_VMEM_BYTES - 4 * 1024 * 1024

FFN_ROW_TILE = 1024
FFN_HIDDEN_TILE = 256
MM_ROW_TILE = 1024
MM_COL_TILE = 512
GLU_ROW_TILE = 512
MOD_COL_TILE = 512


def _cparams(sem):
    return pltpu.CompilerParams(dimension_semantics=sem, vmem_limit_bytes=VMEM_LIMIT)


def _tile(n, pref):
    t = min(n, pref)
    while n % t:
        t //= 2
    return t


ROW_CHUNK = 64


def _for_row_chunks(n, fn):
    def body(i, carry):
        fn(pl.multiple_of(i * ROW_CHUNK, ROW_CHUNK))
        return carry

    lax.fori_loop(0, n // ROW_CHUNK, body, 0)


NORM_ROWS = 16
NORM_UNROLL = 4


def _for_rows(n, step, fn):
    def body(i, carry):
        fn(pl.multiple_of(i * step, step))
        return carry

    lax.fori_loop(0, n // step, body, 0, unroll=NORM_UNROLL)


def _dot(a, b):
    return jnp.dot(a, b, preferred_element_type=F32)


def _mod_kernel(c_ref, w_ref, b_ref, o_ref):
    c = c_ref[...]
    s = (c * jax.nn.sigmoid(c)).astype(BF16)
    o_ref[...] = _dot(s, w_ref[...].astype(BF16)) + b_ref[...]


def _mod(c8, w_mod, b_mod, tn):
    d, n = w_mod.shape
    return pl.pallas_call(
        _mod_kernel,
        out_shape=jax.ShapeDtypeStruct((c8.shape[0], n), F32),
        grid=(n // tn,),
        in_specs=[pl.BlockSpec(c8.shape, lambda j: (0, 0)),
                  pl.BlockSpec((d, tn), lambda j: (0, j)),
                  pl.BlockSpec((1, tn), lambda j: (0, j))],
        out_specs=pl.BlockSpec((c8.shape[0], tn), lambda j: (0, j)),
        compiler_params=_cparams(("arbitrary",)),
        name="mod",
    )(c8, w_mod, b_mod)


FFN_STAGE_ROWS = 128


def _ffn_kernel(x_ref, g_ref, mod_ref, wg_ref, wu_ref, wo_ref, fg_ref, *refs,
                gi, mi, nf, nstage, final_norm, next_pre, cast_blocks):
    nc = len(cast_blocks)
    src_refs, refs = refs[:nc], refs[nc:]
    if next_pre is None:
        o_ref, refs = refs[0], refs[1:]
    else:
        o_ref, u_ref, refs = refs[0], refs[1], refs[2:]
    dst_refs, (pre_ref, acc_ref, vec_ref, r_ref) = refs[:nc], refs[nc:]
    f = pl.program_id(2)
    rc, d = x_ref.shape[1], x_ref.shape[2]
    nr = NORM_ROWS

    def bcast(v):
        return jnp.broadcast_to(v, (nr, d))

    def inv_rms(y):
        ss = jnp.sum(y * y, axis=-1, keepdims=True)
        return jnp.broadcast_to(lax.rsqrt(ss * (1.0 / d) + RMS_EPS), (nr, LANES))

    def lanes(r):
        return jnp.concatenate([r] * (d // LANES), axis=-1)

    @pl.when(f == 0)
    def _():
        vec_ref[0] = bcast(g_ref[pl.ds(gi, 1), :] * (1.0 + mod_ref[0, pl.ds(mi + 1, 1), :]))
        vec_ref[1] = bcast(mod_ref[0, pl.ds(mi, 1), :])
        vec_ref[2] = bcast(HALF * mod_ref[0, pl.ds(mi + 2, 1), :])
        if final_norm:
            vec_ref[3] = bcast(fg_ref[...])
        if next_pre is not None:
            gi2, mi2 = next_pre
            vec_ref[3] = bcast(g_ref[pl.ds(gi2, 1), :] * (1.0 + mod_ref[0, pl.ds(mi2 + 1, 1), :]))
            vec_ref[4] = bcast(mod_ref[0, pl.ds(mi2, 1), :])

    @pl.when(f < nstage)
    def _():
        base = pl.multiple_of(f * rc, rc)

        def stats(r0):
            r_ref[pl.ds(r0, nr), :] = inv_rms(x_ref[0, pl.ds(r0, nr), :])

        def apply(r0):
            pre = x_ref[0, pl.ds(r0, nr), :] * lanes(r_ref[pl.ds(r0, nr), :]) * vec_ref[0] + vec_ref[1]
            pre_ref[pl.ds(base + r0, nr), :] = pre.astype(BF16)
            acc_ref[pl.ds(base + r0, nr), :] = jnp.zeros((nr, d), F32)

        _for_rows(rc, nr, stats)
        _for_rows(rc, nr, apply)

    @pl.when((f >= nstage) & (f < nstage + nf))
    def _():
        p = pre_ref[...]
        gate = _dot(p, wg_ref[...])
        up = _dot(p, wu_ref[...])
        act = (gate * jax.nn.sigmoid(gate) * up).astype(BF16)
        acc_ref[...] += _dot(act, wo_ref[...])
        step = (pl.program_id(0) * pl.num_programs(1) + pl.program_id(1)) * nf + (f - nstage)
        for src, dst, nblk in zip(src_refs, dst_refs, cast_blocks):
            @pl.when(step < nblk)
            def _(src=src, dst=dst):
                dst[...] = src[...].astype(BF16)

    @pl.when(f >= nstage + nf)
    def _():
        base = pl.multiple_of((f - (nstage + nf)) * rc, rc)

        renorm = final_norm or next_pre is not None

        def residual(r0):
            r = pl.ds(r0, nr)
            y = x_ref[0, r, :] + vec_ref[2] * acc_ref[pl.ds(base + r0, nr), :]
            o_ref[0, r, :] = y
            if renorm:
                r_ref[r, :] = inv_rms(y)

        def norm(r0):
            r = pl.ds(r0, nr)
            yn = o_ref[0, r, :] * lanes(r_ref[r, :])
            if final_norm:
                o_ref[0, r, :] = yn * vec_ref[3]
            if next_pre is not None:
                u_ref[0, r, :] = (yn * vec_ref[3] + vec_ref[4]).astype(BF16)

        _for_rows(rc, nr, residual)
        if renorm:
            _for_rows(rc, nr, norm)


def _cast_plan(arr, nsteps):
    r, c = arr.shape
    for parts in (8, 4, 2, 1):
        if c % (parts * LANES):
            continue
        br = BF16_ROWS
        while br <= r:
            if r % br == 0 and (r // br) * parts <= nsteps:
                return br, c // parts
            br *= 2
    return None


def _ffn(x, norm_g, mods, mod_row, w_in, w_out, final_g, *, gi, mi, tm, tf, final_norm=False, next_pre=None,
         casts=()):
    bsz, s, d = x.shape
    ff = w_out.shape[0]
    nf = ff // tf
    tm = _tile(s, tm)
    rc = _tile(tm, FFN_STAGE_ROWS)
    nstage = tm // rc
    ni = s // tm

    def hid(f):
        return jnp.clip(f - nstage, 0, nf - 1)

    cast_blocks, cast_in, cast_out, cast_shapes = [], [], [], []
    for arr, (br, bc) in casts:
        nrb, ncb = arr.shape[0] // br, arr.shape[1] // bc
        assert nrb * br == arr.shape[0] and ncb * bc == arr.shape[1] and nrb * ncb <= bsz * ni * nf

        def blk(b, i, f, nrb=nrb, ncb=ncb):
            step = jnp.minimum((b * ni + i) * nf + hid(f), nrb * ncb - 1)
            return step // ncb, step % ncb

        cast_blocks.append(nrb * ncb)
        cast_in.append(pl.BlockSpec((br, bc), blk))
        cast_out.append(pl.BlockSpec((br, bc), blk))
        cast_shapes.append(jax.ShapeDtypeStruct(arr.shape, BF16))
    kern = functools.partial(_ffn_kernel, gi=gi, mi=mi, nf=nf, nstage=nstage, final_norm=final_norm,
                             next_pre=next_pre, cast_blocks=tuple(cast_blocks))

    def x_blk(b, i, f):
        return b, i * nstage + jnp.where(f < nstage, f, jnp.clip(f - (nstage + nf), 0, nstage - 1)), 0

    def o_blk(b, i, f):
        return b, i * nstage + jnp.clip(f - (nstage + nf), 0, nstage - 1), 0

    out_shape = [jax.ShapeDtypeStruct((bsz, s, d), F32)]
    out_specs = [pl.BlockSpec((1, rc, d), o_blk)]
    if next_pre is not None:
        out_shape.append(jax.ShapeDtypeStruct((bsz, s, d), BF16))
        out_specs.append(pl.BlockSpec((1, rc, d), o_blk))
    return pl.pallas_call(
        kern,
        out_shape=out_shape + cast_shapes,
        grid=(bsz, ni, nf + 2 * nstage),
        in_specs=[pl.BlockSpec((1, rc, d), x_blk),
                  pl.BlockSpec(norm_g.shape, lambda b, i, f: (0, 0)),
                  pl.BlockSpec((1, N_MOD, d), lambda b, i, f: (mod_row(b), 0, 0)),
                  pl.BlockSpec((d, tf), lambda b, i, f: (0, hid(f))),
                  pl.BlockSpec((d, tf), lambda b, i, f: (0, nf + hid(f))),
                  pl.BlockSpec((tf, d), lambda b, i, f: (hid(f), 0)),
                  pl.BlockSpec((1, d), lambda b, i, f: (0, 0))] + cast_in,
        out_specs=out_specs + cast_out,
        scratch_shapes=[pltpu.VMEM((tm, d), BF16), pltpu.VMEM((tm, d), F32),
                        pltpu.VMEM((5, NORM_ROWS, d), F32), pltpu.VMEM((rc, LANES), F32)],
        compiler_params=_cparams(("arbitrary", "arbitrary", "arbitrary")),
        name="ffn",
    )(x, norm_g, mods, w_in, w_in, w_out, final_g, *[arr for arr, _ in casts])


def _mm_kernel(a_ref, b_ref, o_ref):
    o_ref[0] = _dot(a_ref[0], b_ref[...]).astype(o_ref.dtype)


def _mm(a, w, out_dtype, *, tm, tn, col0=0, ncols=None):
    bsz, s, k = a.shape
    n = w.shape[1] - col0 if ncols is None else ncols
    tm, tn = _tile(s, tm), _tile(math.gcd(n, col0) if col0 else n, tn)
    j0 = col0 // tn
    return pl.pallas_call(
        _mm_kernel,
        out_shape=jax.ShapeDtypeStruct((bsz, s, n), out_dtype),
        grid=(bsz, s // tm, n // tn),
        in_specs=[pl.BlockSpec((1, tm, k), lambda b, i, j: (b, i, 0)),
                  pl.BlockSpec((k, tn), lambda b, i, j: (0, j0 + j))],
        out_specs=pl.BlockSpec((1, tm, tn), lambda b, i, j: (b, i, j)),
        compiler_params=_cparams(("arbitrary", "arbitrary", "arbitrary")),
        name="mm",
    )(a, w)


def _mixout_kernel(a_ref, p_ref, wa_ref, wb_ref, ga_ref, gb_ref, wo_ref, x_ref, mod_ref, o_ref, m_ref, *, mi, nj):
    j = pl.program_id(2)
    tn = wa_ref.shape[1]

    @pl.when(j < nj)
    def _():
        ya = _dot(a_ref[0], wa_ref[...])
        yb = _dot(p_ref[0], wb_ref[...])
        m = jax.nn.sigmoid(ga_ref[0].astype(F32)) * ya + jax.nn.sigmoid(gb_ref[0].astype(F32)) * yb
        m_ref[:, pl.ds(pl.multiple_of(j * tn, tn), tn)] = m.astype(BF16)

    @pl.when(j >= nj)
    def _():
        o_ref[0] = x_ref[0] + mod_ref[0, pl.ds(mi, 1), :] * _dot(m_ref[...], wo_ref[...])


def _mixout(ag, yp, wa, wb, gates, wo, x, mods, mod_row, *, mi, tm, tn):
    bsz, s, ka = ag.shape
    kb = yp.shape[2]
    d = wa.shape[1]
    tm, tn = _tile(s, tm), _tile(d, tn)
    nj = d // tn

    def first(j):
        return jnp.minimum(j, nj - 1)

    def second(j):
        return jnp.maximum(j - nj, 0)

    return pl.pallas_call(
        functools.partial(_mixout_kernel, mi=mi, nj=nj),
        out_shape=jax.ShapeDtypeStruct((bsz, s, d), F32),
        grid=(bsz, s // tm, 2 * nj),
        in_specs=[pl.BlockSpec((1, tm, ka), lambda b, i, j: (b, i, 0)),
                  pl.BlockSpec((1, tm, kb), lambda b, i, j: (b, i, 0)),
                  pl.BlockSpec((ka, tn), lambda b, i, j: (0, first(j))),
                  pl.BlockSpec((kb, tn), lambda b, i, j: (0, first(j))),
                  pl.BlockSpec((1, tm, tn), lambda b, i, j: (b, i, first(j))),
                  pl.BlockSpec((1, tm, tn), lambda b, i, j: (b, i, nj + first(j))),
                  pl.BlockSpec((d, tn), lambda b, i, j: (0, second(j))),
                  pl.BlockSpec((1, tm, tn), lambda b, i, j: (b, i, second(j))),
                  pl.BlockSpec((1, N_MOD, tn), lambda b, i, j: (mod_row(b), 0, second(j)))],
        out_specs=pl.BlockSpec((1, tm, tn), lambda b, i, j: (b, i, second(j))),
        scratch_shapes=[pltpu.VMEM((tm, d), BF16)],
        compiler_params=_cparams(("arbitrary", "arbitrary", "arbitrary")),
        name="mixout",
    )(ag, yp, wa, wb, gates, gates, wo, x, mods)


def _ssm_tables(lam_re, lam_im, log_step, b_re, b_im, c_re, c_im):
    t = SSM_CHUNK
    hp = lax.Precision.HIGHEST
    lr = jnp.minimum(lam_re.astype(F32), LAMBDA_RE_MAX)
    li = lam_im.astype(F32)
    step = jnp.exp(log_step.astype(F32))[..., None]
    m = jnp.arange(t + 1, dtype=F32)[:, None, None, None]
    mag = jnp.exp(m * (lr * step)[None])
    ang = m * (li * step)[None]
    pw_re, pw_im = mag * jnp.cos(ang), mag * jnp.sin(ang)
    nr, ni = pw_re[1] - 1.0, pw_im[1]
    den = lr * lr + li * li
    q_re, q_im = (nr * lr + ni * li) / den, (ni * lr - nr * li) / den
    bb_re = q_re[..., None] * b_re - q_im[..., None] * b_im
    bb_im = q_re[..., None] * b_im + q_im[..., None] * b_re
    cr, ci = c_re.astype(F32), c_im.astype(F32)

    cl_re = cr[:, :, None] * pw_re[:t].transpose(1, 2, 0, 3)[:, :, :, None, :] \
        - ci[:, :, None] * pw_im[:t].transpose(1, 2, 0, 3)[:, :, :, None, :]
    cl_im = cr[:, :, None] * pw_im[:t].transpose(1, 2, 0, 3)[:, :, :, None, :] \
        + ci[:, :, None] * pw_re[:t].transpose(1, 2, 0, 3)[:, :, :, None, :]
    kern = jnp.einsum('dgtkp,dgpq->dgtkq', cl_re, bb_re, precision=hp) \
        - jnp.einsum('dgtkp,dgpq->dgtkq', cl_im, bb_im, precision=hp)
    g, kk = kern.shape[1], kern.shape[3]
    zero = (jnp.arange(t) == 0).astype(F32)[None, :, None, None]
    kpos = kern[0] + zero * kern[1]
    kneg = kern[1] + zero * kern[0]
    rpos = kpos.transpose(0, 3, 1, 2).reshape(g, kk, t * kk)
    rneg = kneg.transpose(0, 3, 1, 2).reshape(g, kk, t * kk)

    pf_re, pf_im = pw_re[:t][::-1, 0], pw_im[:t][::-1, 0]
    pb_re, pb_im = pw_re[:t, 1], pw_im[:t, 1]

    def st(p_re, p_im, d):
        re = p_re[:, :, :, None] * bb_re[d][None] - p_im[:, :, :, None] * bb_im[d][None]
        im = p_re[:, :, :, None] * bb_im[d][None] + p_im[:, :, :, None] * bb_re[d][None]
        return re.transpose(1, 0, 3, 2), im.transpose(1, 0, 3, 2)

    wf_re, wf_im = st(pf_re, pf_im, 0)
    wb_re, wb_im = st(pb_re, pb_im, 1)
    p = lr.shape[-1]
    wst = jnp.concatenate([wf_re, wb_re, wf_im, wb_im], axis=-1).reshape(g, t * kk, 4 * p)

    vf_pw_re, vf_pw_im = pw_re[1:, 0], pw_im[1:, 0]
    vb_pw_re, vb_pw_im = pw_re[1:, 1][::-1], pw_im[1:, 1][::-1]

    def rd(p_re, p_im, d):
        d_re = cr[d][None] * p_re[:, :, None, :] - ci[d][None] * p_im[:, :, None, :]
        d_im = cr[d][None] * p_im[:, :, None, :] + ci[d][None] * p_re[:, :, None, :]
        return d_re.transpose(1, 3, 0, 2), -d_im.transpose(1, 3, 0, 2)

    vf_re, vf_im = rd(vf_pw_re, vf_pw_im, 0)
    vb_re, vb_im = rd(vb_pw_re, vb_pw_im, 1)
    vrd = jnp.concatenate([vf_re, vb_re, vf_im, vb_im], axis=1).reshape(g, 4 * p, t * kk)

    a_re = jnp.concatenate([pw_re[t, 0], pw_re[t, 1]], axis=-1)[:, None, :]
    a_im = jnp.concatenate([pw_im[t, 0], pw_im[t, 1]], axis=-1)[:, None, :]
    return rpos.astype(BF16), rneg.astype(BF16), wst.astype(BF16), vrd.astype(BF16), a_re, a_im


SSM_GPB = LANES // SSM_GROUP


def _ssm_block_tables(rpos, rneg, wst, vrd, a_re, a_im):
    rpos, rneg, wst, vrd = lax.optimization_barrier((rpos, rneg, wst, vrd))
    g, _, tk = rpos.shape
    t, k, gpb = SSM_CHUNK, SSM_GROUP, SSM_GPB
    nb = g // gpb
    p2 = wst.shape[2] // 2
    tl, hw = t * gpb * k, gpb * p2
    rpos = rpos.reshape(nb, gpb * k, tk)
    rneg = rneg.reshape(nb, gpb * k, tk)
    rw = wst.reshape(nb, gpb, t, k, 2 * p2).transpose(0, 2, 1, 3, 4).reshape(nb, t, gpb * k, 2 * p2)
    rv = vrd.reshape(nb, gpb, 2, p2, tk).transpose(0, 2, 1, 3, 4).reshape(nb, 2, hw, tk)
    jj, kk = np.arange(tk) // k, np.arange(tk) % k
    sel = (jj[None, :, None] == np.arange(t)[:, None, None]) & (kk[None, :, None] == (np.arange(LANES) % k)[None, None, :])
    sel = jnp.asarray(sel, BF16)
    toep_b, wst_b, vrd_b = pl.pallas_call(
        functools.partial(_ssm_expand_kernel, t=t, k=k, p2=p2),
        out_shape=[jax.ShapeDtypeStruct((nb, tl, tl), BF16), jax.ShapeDtypeStruct((nb, tl, 2 * hw), BF16),
                   jax.ShapeDtypeStruct((nb, 2 * hw, tl), BF16)],
        grid=(nb, t),
        in_specs=[pl.BlockSpec((1, gpb * k, tk), lambda b, i: (b, 0, 0)),
                  pl.BlockSpec((1, gpb * k, tk), lambda b, i: (b, 0, 0)),
                  pl.BlockSpec((1, 1, gpb * k, 2 * p2), lambda b, i: (b, i, 0, 0)),
                  pl.BlockSpec((1, 2, hw, tk), lambda b, i: (b, 0, 0, 0)),
                  pl.BlockSpec(sel.shape, lambda b, i: (0, 0, 0))],
        out_specs=[pl.BlockSpec((1, gpb * k, tl), lambda b, i: (b, i, 0)),
                   pl.BlockSpec((1, gpb * k, 2 * hw), lambda b, i: (b, i, 0)),
                   pl.BlockSpec((1, 2 * hw, gpb * k), lambda b, i: (b, 0, i))],
        compiler_params=_cparams(("arbitrary", "arbitrary")),
        name="ssm_expand",
    )(rpos, rneg, rw, rv, sel)
    return toep_b, wst_b, vrd_b, a_re.reshape(nb, 1, hw), a_im.reshape(nb, 1, hw)


def _ssm_expand_kernel(rpos_ref, rneg_ref, rw_ref, rv_ref, sel_ref, toep_ref, wst_ref, vrd_ref, *, t, k, p2):
    i = pl.program_id(1)
    lb = rpos_ref.shape[1]
    hw = rv_ref.shape[2]
    gpb = lb // k
    same_g = (lax.broadcasted_iota(jnp.int32, (lb, lb), 0) // k) == (lax.broadcasted_iota(jnp.int32, (lb, lb), 1) // k)
    for j in range(t):
        pos = _dot(rpos_ref[0], sel_ref[jnp.clip(j - i, 0, t - 1)])
        neg = _dot(rneg_ref[0], sel_ref[jnp.clip(i - j, 0, t - 1)])
        blk = jnp.where(i <= j, pos, neg)
        toep_ref[0, :, j * lb:(j + 1) * lb] = jnp.where(same_g, blk, 0.0).astype(BF16)
    w = rw_ref[0, 0]
    same_gw = (lax.broadcasted_iota(jnp.int32, (lb, hw), 0) // k) == (lax.broadcasted_iota(jnp.int32, (lb, hw), 1) // p2)
    for r in range(2):
        wr = jnp.concatenate([w[:, r * p2:(r + 1) * p2]] * gpb, axis=-1)
        wst_ref[0, :, r * hw:(r + 1) * hw] = jnp.where(same_gw, wr, jnp.zeros_like(wr))
    same_gv = (lax.broadcasted_iota(jnp.int32, (hw, lb), 0) // p2) == (lax.broadcasted_iota(jnp.int32, (hw, lb), 1) // k)
    for r in range(2):
        v = _dot(rv_ref[0, r], sel_ref[i])
        vrd_ref[0, r * hw:(r + 1) * hw, :] = jnp.where(same_gv, v, 0.0).astype(BF16)


def _ssm_kernel(u_ref, ws_ref, ar_ref, ai_ref, h0_ref, *refs, nc, want_y):
    t = SSM_CHUNK
    if want_y:
        wt_ref, v_ref, d_ref, z_ref, hfin_ref, s_ref, yi_ref, hp_ref, tmp_ref, y_ref = refs
    else:
        hfin_ref, s_ref = refs
    a = jnp.concatenate([u_ref[0, pl.ds(j, nc, stride=t), :] for j in range(t)], axis=-1).astype(BF16)
    s = _dot(a, ws_ref[0])
    nlb = s.shape[1] // LANES
    gpb = nlb // 2
    for k in range(nlb):
        s_ref[pl.ds(k, nc, stride=nlb), :] = s[:, k * LANES:(k + 1) * LANES]
    if want_y:
        yi_ref[...] = _dot(a, wt_ref[0])

    ar = ar_ref[0]
    ai = ai_ref[0]
    fwd = lax.broadcasted_iota(jnp.int32, (gpb, LANES), 1) < (LANES // 2)

    def body(c, carry):
        h_re, h_im = carry
        rf = pl.multiple_of(c * nlb, nlb)
        rb = pl.multiple_of((nc - 1 - c) * nlb, nlb)
        if want_y:
            hp_ref[pl.ds(rf, gpb), :] = h_re
            hp_ref[pl.ds(rf + gpb, gpb), :] = h_im
            tmp_ref[pl.ds(rb, gpb), :] = h_re
            tmp_ref[pl.ds(rb + gpb, gpb), :] = h_im
        s_re = jnp.where(fwd, s_ref[pl.ds(rf, gpb), :], s_ref[pl.ds(rb, gpb), :])
        s_im = jnp.where(fwd, s_ref[pl.ds(rf + gpb, gpb), :], s_ref[pl.ds(rb + gpb, gpb), :])
        return ar * h_re - ai * h_im + s_re, ar * h_im + ai * h_re + s_im

    h0 = h0_ref[0, 0]
    h_re, h_im = lax.fori_loop(0, nc, body, (h0[:gpb], h0[gpb:]))
    hfin_ref[0, 0] = jnp.concatenate([h_re, h_im], axis=0)

    if want_y:
        fwd2 = lax.broadcasted_iota(jnp.int32, hp_ref.shape, 1) < (LANES // 2)
        hp_ref[...] = jnp.where(fwd2, hp_ref[...], tmp_ref[...])
        hp = jnp.concatenate([hp_ref[pl.ds(k, nc, stride=nlb), :] for k in range(nlb)], axis=-1).astype(BF16)
        y = yi_ref[...] + _dot(hp, v_ref[0])
        for j in range(t):
            y_ref[pl.ds(j, nc, stride=t), :] = y[:, j * LANES:(j + 1) * LANES]

        def rows(r0):
            r = pl.ds(r0, ROW_CHUNK)
            z_ref[0, r, :] = jax.nn.gelu(y_ref[r, :] + d_ref[...] * u_ref[0, r, :]).astype(BF16)

        _for_row_chunks(nc * t, rows)


def _ssm(u, tables, h0, d, *, want_y):
    toep_b, wst_b, vrd_b, ar, ai = tables
    bsz, n, _ = u.shape
    nb, tl, sw = wst_b.shape
    nc = n // SSM_CHUNK
    single = pl.Buffered(1)
    nlb = sw // LANES
    gpb = nlb // 2
    shapes = [jax.ShapeDtypeStruct((bsz, nb, nlb, LANES), F32)]
    specs = [pl.BlockSpec((1, 1, nlb, LANES), lambda k, b: (b, k, 0, 0))]
    scratch = [pltpu.VMEM((nc * nlb, LANES), F32)]
    in_specs = [pl.BlockSpec((1, n, LANES), lambda k, b: (b, 0, k)),
                pl.BlockSpec((1, tl, sw), lambda k, b: (k, 0, 0)),
                pl.BlockSpec((1, gpb, LANES), lambda k, b: (k, 0, 0)),
                pl.BlockSpec((1, gpb, LANES), lambda k, b: (k, 0, 0)),
                pl.BlockSpec((1, 1, nlb, LANES), lambda k, b: (b, k, 0, 0))]
    args = [u, wst_b, ar.reshape(nb, gpb, LANES), ai.reshape(nb, gpb, LANES), h0]
    if want_y:
        in_specs += [pl.BlockSpec((1, tl, tl), lambda k, b: (k, 0, 0)),
                     pl.BlockSpec((1, sw, tl), lambda k, b: (k, 0, 0), pipeline_mode=single),
                     pl.BlockSpec((1, LANES), lambda k, b: (0, k))]
        args += [toep_b, vrd_b, d]
        shapes.insert(0, jax.ShapeDtypeStruct((bsz, n, nb * LANES), BF16))
        specs.insert(0, pl.BlockSpec((1, n, LANES), lambda k, b: (b, 0, k)))
        scratch += [pltpu.VMEM((nc, tl), F32), pltpu.VMEM((nc * nlb, LANES), F32),
                    pltpu.VMEM((nc * nlb, LANES), F32), pltpu.VMEM((n, LANES), F32)]
    return pl.pallas_call(
        functools.partial(_ssm_kernel, nc=nc, want_y=want_y),
        out_shape=shapes,
        grid=(nb, bsz),
        in_specs=in_specs,
        out_specs=specs,
        scratch_shapes=scratch,
        compiler_params=_cparams(("arbitrary", "arbitrary")),
        name="ssm",
    )(*args)


def _glu_kernel(z_ref, w_ref, o_ref, *, ws):
    z = z_ref[0]
    a = _dot(z, w_ref[:, :ws])
    b = _dot(z, w_ref[:, ws:])
    o_ref[0] = (a * jax.nn.sigmoid(b)).astype(BF16)


def _glu(z, w_glu, *, tm):
    bsz, s, ws = z.shape
    tm = _tile(s, tm)
    return pl.pallas_call(
        functools.partial(_glu_kernel, ws=ws),
        out_shape=jax.ShapeDtypeStruct((bsz, s, ws), BF16),
        grid=(bsz, s // tm),
        in_specs=[pl.BlockSpec((1, tm, ws), lambda b, i: (b, i, 0)),
                  pl.BlockSpec(w_glu.shape, lambda b, i: (0, 0))],
        out_specs=pl.BlockSpec((1, tm, ws), lambda b, i: (b, i, 0)),
        compiler_params=_cparams(("arbitrary", "arbitrary")),
        name="glu",
    )(z, w_glu)


POOL_TILE = 4 * GRID_W


def _pool_consts(n):
    t = np.arange(POOL_TILE)
    pcs, invs = [], []
    r = np.arange(n) // GRID_W
    c = np.arange(n) % GRID_W
    rows = n // GRID_W
    for w in POOL_WINDOWS:
        lo, hi = w // 2, w - w // 2
        same_row = (t[:, None] // GRID_W) == (t[None, :] // GRID_W)
        dc = (t[None, :] % GRID_W) - (t[:, None] % GRID_W)
        pcs.append((same_row & (dc >= -lo) & (dc < hi)).astype(np.float32))
        cnt_r = np.minimum(r + hi, rows) - np.maximum(r - lo, 0)
        cnt_c = np.minimum(c + hi, GRID_W) - np.maximum(c - lo, 0)
        invs.append((1.0 / (cnt_r * cnt_c)).astype(np.float32)[:, None])
    return jnp.asarray(np.stack(pcs), BF16), jnp.asarray(np.stack(invs), F32)


def _pool_kernel(v_ref, pc_ref, inv_ref, pw_ref, sc_ref, o_ref, cs_ref, *, n, pad):
    wi = pl.program_id(0)
    nt = n // POOL_TILE
    ch = v_ref.shape[2]
    zeros = jnp.zeros((pad, ch), F32)
    cs_ref[pl.ds(0, pad), :] = zeros
    cs_ref[pl.ds(pad + n, pad), :] = zeros
    pc = pc_ref[0]

    def col_body(i, _):
        off = pl.multiple_of(i * POOL_TILE, POOL_TILE)
        v = v_ref[0, pl.ds(off, POOL_TILE), :]
        hi = v.astype(BF16)
        lo = (v - hi.astype(F32)).astype(BF16)
        cs_ref[pl.ds(pad + off, POOL_TILE), :] = _dot(pc, hi) + _dot(pc, lo)
        return 0

    lax.fori_loop(0, nt, col_body, 0)

    for k, w in enumerate(POOL_WINDOWS):
        @pl.when(wi == k)
        def _(w=w):
            def row_body(i, _):
                off = pl.multiple_of(i * POOL_TILE, POOL_TILE)
                acc = cs_ref[pl.ds(pad + off - (w // 2) * GRID_W, POOL_TILE), :]
                for dlt in range(-(w // 2) + 1, w - w // 2):
                    acc = acc + cs_ref[pl.ds(pad + off + dlt * GRID_W, POOL_TILE), :]
                mixed = acc * inv_ref[0, pl.ds(off, POOL_TILE), :] - v_ref[0, pl.ds(off, POOL_TILE), :]
                y = _dot(mixed.astype(BF16), pw_ref[0]) * sc_ref[0]
                o_ref[0, pl.ds(off, POOL_TILE), :] = y.astype(BF16)
                return 0

            lax.fori_loop(0, nt, row_body, 0)


def _pool(usp, col0, pool_w, pool_scale, pc, inv):
    bsz, n, _ = usp.shape
    nw, ch, _ = pool_w.shape
    pad = (max(POOL_WINDOWS) // 2) * GRID_W
    cb0 = col0 // ch
    return pl.pallas_call(
        functools.partial(_pool_kernel, n=n, pad=pad),
        out_shape=jax.ShapeDtypeStruct((bsz, n, nw * ch), BF16),
        grid=(nw, bsz),
        in_specs=[pl.BlockSpec((1, n, ch), lambda w, b: (b, 0, cb0 + w)),
                  pl.BlockSpec((1, POOL_TILE, POOL_TILE), lambda w, b: (w, 0, 0)),
                  pl.BlockSpec((1, n, 1), lambda w, b: (w, 0, 0)),
                  pl.BlockSpec((1, ch, ch), lambda w, b: (w, 0, 0)),
                  pl.BlockSpec((1, 1, ch), lambda w, b: (w, 0, 0))],
        out_specs=pl.BlockSpec((1, n, ch), lambda w, b: (b, 0, w)),
        scratch_shapes=[pltpu.VMEM((n + 2 * pad, ch), F32)],
        compiler_params=_cparams(("arbitrary", "arbitrary")),
        name="pool",
    )(usp, pc, inv, pool_w, pool_scale)


def kernel(x, c, ctx, c_ctx, w_mod, b_mod, norm_g, final_g, ffn1_w_in, ffn1_w_out, ffn2_w_in, ffn2_w_out,
           w_in, ssm_lambda_re, ssm_lambda_im, ssm_log_step, ssm_b_re, ssm_b_im, ssm_c_re, ssm_c_im, ssm_d,
           w_glu, w_branch_a, pool_w, pool_scale, w_branch_b, w_out):
    bsz, seq, d = x.shape
    assert w_mod.shape[0] == 1, "single-layer problem"
    ssm_w = ssm_d.shape[1]
    pool_width = pool_scale.shape[1]
    nw = len(POOL_WINDOWS)
    pch = pool_width // nw
    p = ssm_lambda_re.shape[-1]
    assert seq % POOL_TILE == 0 and ctx.shape[1] % SSM_CHUNK == 0 and ssm_w % LANES == 0

    f1_in, f1_out = ffn1_w_in[0].astype(BF16), ffn1_w_out[0].astype(BF16)
    w_s = w_in[0][:, :ssm_w].astype(BF16)
    late = [ffn2_w_in[0], ffn2_w_out[0], w_in[0], w_out[0]]
    wbb, wba, wglu, pw = (w_branch_b[0].astype(BF16), w_branch_a[0].astype(BF16), w_glu[0].astype(BF16),
                          pool_w[0].astype(BF16))
    psc = pool_scale[0].reshape(nw, 1, pch)
    ng = norm_g[0]
    fg = final_g.reshape(1, d)

    rows = -(-(bsz + 1) // 8) * 8
    c8 = jnp.zeros((rows, d), F32).at[:bsz].set(c).at[bsz].set(c_ctx)
    mods = _mod(c8, w_mod[0], b_mod, tn=_tile(w_mod.shape[2], MOD_COL_TILE)).reshape(rows, N_MOD, d)
    lat = lambda b: b
    con = lambda b: bsz
    ffn_tiles = dict(tm=FFN_ROW_TILE, tf=FFN_HIDDEN_TILE)
    mm_tiles = dict(tm=MM_ROW_TILE, tn=MM_COL_TILE)

    tables = _ssm_block_tables(*_ssm_tables(ssm_lambda_re[0], ssm_lambda_im[0], ssm_log_step[0],
                                            ssm_b_re[0], ssm_b_im[0], ssm_c_re[0], ssm_c_im[0]))

    nctx = ctx.shape[1]
    _, uc = _ffn(ctx.reshape(1, bsz * nctx, d), ng, mods, con, f1_in, f1_out, fg, gi=0, mi=0, next_pre=(1, 3),
                 **ffn_tiles)
    us_c = _mm(uc, w_s, F32, **mm_tiles).reshape(bsz, nctx, ssm_w)
    assert 2 * p == LANES, "fwd | bwd states of one group fill one lane tile"
    h_ctx, = _ssm(us_c, tables, jnp.zeros((bsz, ssm_w // LANES, 2 * SSM_GPB, LANES), F32), ssm_d, want_y=False)

    nsteps = bsz * (seq // _tile(seq, FFN_ROW_TILE)) * (f1_out.shape[0] // FFN_HIDDEN_TILE)
    plans = [_cast_plan(a, nsteps) for a in late]
    x1, u, *cast = _ffn(x, ng, mods, lat, f1_in, f1_out, fg, gi=0, mi=0, next_pre=(1, 3),
                        casts=[(a, pln) for a, pln in zip(late, plans) if pln is not None], **ffn_tiles)
    cast = iter(cast)
    f2_in, f2_out, w_all, wo = [a.astype(BF16) if pln is None else next(cast) for a, pln in zip(late, plans)]
    usp = _mm(u, w_all, F32, ncols=ssm_w + pool_width, **mm_tiles)
    gates = _mm(u, w_all, BF16, col0=ssm_w + pool_width, **mm_tiles)

    z, _ = _ssm(usp, tables, h_ctx, ssm_d, want_y=True)
    ag = _glu(z, wglu, tm=GLU_ROW_TILE)

    pc, inv = _pool_consts(seq)
    yp = _pool(usp, ssm_w, pw, psc, pc, inv)

    x2 = _mixout(ag, yp, wba, wbb, gates, wo, x1, mods, lat, mi=5, **mm_tiles)
    out, = _ffn(x2, ng, mods, lat, f2_in, f2_out, fg, gi=2, mi=6, final_norm=True, **ffn_tiles)
    return out
```

```python
import functools
import math

import jax
import jax.numpy as jnp
import numpy as np
from jax import lax
from jax.experimental import pallas as pl
from jax.experimental.pallas import tpu as pltpu

BF16 = jnp.bfloat16
F32 = jnp.float32

RMS_EPS = 1e-6
LAMBDA_RE_MAX = -1e-4
HALF = 0.5
N_MOD = 9
SSM_GROUP = 16
POOL_WINDOWS = (2, 4, 8, 16)
GRID_W = 64
SSM_CHUNK = 16

LANES = 128
BF16_ROWS = 16
V7X_VMEM_BYTES = 64 * 1024 * 1024
VMEM_LIMIT = V7X_VMEM_BYTES - 4 * 1024 * 1024

FFN_ROW_TILE = 1024
FFN_HIDDEN_TILE = 256
MM_ROW_TILE = 2048
MIX_ROW_TILE = 1024
MM_COL_TILE = 512
GLU_ROW_TILE = 1024
MOD_COL_TILE = 512


def _cparams(sem):
    return pltpu.CompilerParams(dimension_semantics=sem, vmem_limit_bytes=VMEM_LIMIT)


def _tile(n, pref):
    t = min(n, pref)
    while n % t:
        t //= 2
    return t


ROW_CHUNK = 64


def _for_row_chunks(n, fn):
    def body(i, carry):
        fn(pl.multiple_of(i * ROW_CHUNK, ROW_CHUNK))
        return carry

    lax.fori_loop(0, n // ROW_CHUNK, body, 0)


NORM_ROWS = 16
NORM_UNROLL = 4


def _for_rows(n, step, fn):
    def body(i, carry):
        fn(pl.multiple_of(i * step, step))
        return carry

    lax.fori_loop(0, n // step, body, 0, unroll=NORM_UNROLL)


def _dot(a, b):
    return jnp.dot(a, b, preferred_element_type=F32)


def _mod_kernel(c_ref, w_ref, b_ref, o_ref):
    c = c_ref[...]
    s = (c * jax.nn.sigmoid(c)).astype(BF16)
    o_ref[...] = _dot(s, w_ref[...].astype(BF16)) + b_ref[...]


def _mod(c8, w_mod, b_mod, tn):
    d, n = w_mod.shape
    return pl.pallas_call(
        _mod_kernel,
        out_shape=jax.ShapeDtypeStruct((c8.shape[0], n), F32),
        grid=(n // tn,),
        in_specs=[pl.BlockSpec(c8.shape, lambda j: (0, 0)),
                  pl.BlockSpec((d, tn), lambda j: (0, j)),
                  pl.BlockSpec((1, tn), lambda j: (0, j))],
        out_specs=pl.BlockSpec((c8.shape[0], tn), lambda j: (0, j)),
        compiler_params=_cparams(("arbitrary",)),
        name="mod",
    )(c8, w_mod, b_mod)


FFN_STAGE_ROWS = 128


def _ffn_kernel(x_ref, g_ref, mod_ref, wg_ref, wu_ref, wo_ref, fg_ref, *refs,
                gi, mi, nf, nstage, final_norm, next_pre, cast_blocks):
    nc = len(cast_blocks)
    src_refs, refs = refs[:nc], refs[nc:]
    if next_pre is None:
        o_ref, refs = refs[0], refs[1:]
    else:
        o_ref, u_ref, refs = refs[0], refs[1], refs[2:]
    dst_refs, (pre_ref, acc_ref, vec_ref, r_ref) = refs[:nc], refs[nc:]
    f = pl.program_id(2)
    rc, d = x_ref.shape[1], x_ref.shape[2]
    nr = NORM_ROWS

    def bcast(v):
        return jnp.broadcast_to(v, (nr, d))

    def inv_rms(y):
        ss = jnp.sum(y * y, axis=-1, keepdims=True)
        return jnp.broadcast_to(lax.rsqrt(ss * (1.0 / d) + RMS_EPS), (nr, LANES))

    def lanes(r):
        return jnp.concatenate([r] * (d // LANES), axis=-1)

    @pl.when(f == 0)
    def _():
        vec_ref[0] = bcast(g_ref[pl.ds(gi, 1), :] * (1.0 + mod_ref[0, pl.ds(mi + 1, 1), :]))
        vec_ref[1] = bcast(mod_ref[0, pl.ds(mi, 1), :])
        vec_ref[2] = bcast(HALF * mod_ref[0, pl.ds(mi + 2, 1), :])
        if final_norm:
            vec_ref[3] = bcast(fg_ref[...])
        if next_pre is not None:
            gi2, mi2 = next_pre
            vec_ref[3] = bcast(g_ref[pl.ds(gi2, 1), :] * (1.0 + mod_ref[0, pl.ds(mi2 + 1, 1), :]))
            vec_ref[4] = bcast(mod_ref[0, pl.ds(mi2, 1), :])

    @pl.when(f < nstage)
    def _():
        base = pl.multiple_of(f * rc, rc)

        def stats(r0):
            r_ref[pl.ds(r0, nr), :] = inv_rms(x_ref[0, pl.ds(r0, nr), :])

        def apply(r0):
            pre = x_ref[0, pl.ds(r0, nr), :] * lanes(r_ref[pl.ds(r0, nr), :]) * vec_ref[0] + vec_ref[1]
            pre_ref[pl.ds(base + r0, nr), :] = pre.astype(BF16)
            acc_ref[pl.ds(base + r0, nr), :] = jnp.zeros((nr, d), F32)

        _for_rows(rc, nr, stats)
        _for_rows(rc, nr, apply)

    @pl.when((f >= nstage) & (f < nstage + nf))
    def _():
        p = pre_ref[...]
        gate = _dot(p, wg_ref[...])
        up = _dot(p, wu_ref[...])
        act = (gate * jax.nn.sigmoid(gate) * up).astype(BF16)
        acc_ref[...] += _dot(act, wo_ref[...])
        step = (pl.program_id(0) * pl.num_programs(1) + pl.program_id(1)) * nf + (f - nstage)
        for src, dst, nblk in zip(src_refs, dst_refs, cast_blocks):
            @pl.when(step < nblk)
            def _(src=src, dst=dst):
                dst[...] = src[...].astype(BF16)

    @pl.when(f >= nstage + nf)
    def _():
        base = pl.multiple_of((f - (nstage + nf)) * rc, rc)

        renorm = final_norm or next_pre is not None

        def residual(r0):
            r = pl.ds(r0, nr)
            y = x_ref[0, r, :] + vec_ref[2] * acc_ref[pl.ds(base + r0, nr), :]
            o_ref[0, r, :] = y
            if renorm:
                r_ref[r, :] = inv_rms(y)

        def norm(r0):
            r = pl.ds(r0, nr)
            yn = o_ref[0, r, :] * lanes(r_ref[r, :])
            if final_norm:
                o_ref[0, r, :] = yn * vec_ref[3]
            if next_pre is not None:
                u_ref[0, r, :] = (yn * vec_ref[3] + vec_ref[4]).astype(BF16)

        _for_rows(rc, nr, residual)
        if renorm:
            _for_rows(rc, nr, norm)


def _cast_plan(arr, nsteps):
    r, c = arr.shape
    for parts in (8, 4, 2, 1):
        if c % (parts * LANES):
            continue
        br = BF16_ROWS
        while br <= r:
            if r % br == 0 and (r // br) * parts <= nsteps:
                return br, c // parts
            br *= 2
    return None


def _ffn(x, norm_g, mods, mod_row, w_in, w_out, final_g, *, gi, mi, tm, tf, final_norm=False, next_pre=None,
         casts=()):
    bsz, s, d = x.shape
    ff = w_out.shape[0]
    nf = ff // tf
    tm = _tile(s, tm)
    rc = _tile(tm, FFN_STAGE_ROWS)
    nstage = tm // rc
    ni = s // tm

    def hid(f):
        return jnp.clip(f - nstage, 0, nf - 1)

    cast_blocks, cast_in, cast_out, cast_shapes = [], [], [], []
    for arr, (br, bc) in casts:
        nrb, ncb = arr.shape[0] // br, arr.shape[1] // bc
        assert nrb * br == arr.shape[0] and ncb * bc == arr.shape[1] and nrb * ncb <= bsz * ni * nf

        def blk(b, i, f, nrb=nrb, ncb=ncb):
            step = jnp.minimum((b * ni + i) * nf + hid(f), nrb * ncb - 1)
            return step // ncb, step % ncb

        cast_blocks.append(nrb * ncb)
        cast_in.append(pl.BlockSpec((br, bc), blk))
        cast_out.append(pl.BlockSpec((br, bc), blk))
        cast_shapes.append(jax.ShapeDtypeStruct(arr.shape, BF16))
    kern = functools.partial(_ffn_kernel, gi=gi, mi=mi, nf=nf, nstage=nstage, final_norm=final_norm,
                             next_pre=next_pre, cast_blocks=tuple(cast_blocks))

    def x_blk(b, i, f):
        return b, i * nstage + jnp.where(f < nstage, f, jnp.clip(f - (nstage + nf), 0, nstage - 1)), 0

    def o_blk(b, i, f):
        return b, i * nstage + jnp.clip(f - (nstage + nf), 0, nstage - 1), 0

    out_shape = [jax.ShapeDtypeStruct((bsz, s, d), F32)]
    out_specs = [pl.BlockSpec((1, rc, d), o_blk)]
    if next_pre is not None:
        out_shape.append(jax.ShapeDtypeStruct((bsz, s, d), BF16))
        out_specs.append(pl.BlockSpec((1, rc, d), o_blk))
    return pl.pallas_call(
        kern,
        out_shape=out_shape + cast_shapes,
        grid=(bsz, ni, nf + 2 * nstage),
        in_specs=[pl.BlockSpec((1, rc, d), x_blk),
                  pl.BlockSpec(norm_g.shape, lambda b, i, f: (0, 0)),
                  pl.BlockSpec((1, N_MOD, d), lambda b, i, f: (mod_row(b), 0, 0)),
                  pl.BlockSpec((d, tf), lambda b, i, f: (0, hid(f))),
                  pl.BlockSpec((d, tf), lambda b, i, f: (0, nf + hid(f))),
                  pl.BlockSpec((tf, d), lambda b, i, f: (hid(f), 0)),
                  pl.BlockSpec((1, d), lambda b, i, f: (0, 0))] + cast_in,
        out_specs=out_specs + cast_out,
        scratch_shapes=[pltpu.VMEM((tm, d), BF16), pltpu.VMEM((tm, d), F32),
                        pltpu.VMEM((5, NORM_ROWS, d), F32), pltpu.VMEM((rc, LANES), F32)],
        compiler_params=_cparams(("arbitrary", "arbitrary", "arbitrary")),
        name="ffn",
    )(x, norm_g, mods, w_in, w_in, w_out, final_g, *[arr for arr, _ in casts])


def _mm_kernel(a_ref, b_ref, o_ref):
    o_ref[0] = _dot(a_ref[0], b_ref[...]).astype(o_ref.dtype)


def _mm(a, w, out_dtype, *, tm, tn, col0=0, ncols=None):
    bsz, s, k = a.shape
    n = w.shape[1] - col0 if ncols is None else ncols
    tm, tn = _tile(s, tm), _tile(math.gcd(n, col0) if col0 else n, tn)
    j0 = col0 // tn
    return pl.pallas_call(
        _mm_kernel,
        out_shape=jax.ShapeDtypeStruct((bsz, s, n), out_dtype),
        grid=(bsz, s // tm, n // tn),
        in_specs=[pl.BlockSpec((1, tm, k), lambda b, i, j: (b, i, 0)),
                  pl.BlockSpec((k, tn), lambda b, i, j: (0, j0 + j))],
        out_specs=pl.BlockSpec((1, tm, tn), lambda b, i, j: (b, i, j)),
        compiler_params=_cparams(("arbitrary", "arbitrary", "arbitrary")),
        name="mm",
    )(a, w)


def _mixout_kernel(a_ref, p_ref, wa_ref, wb_ref, ga_ref, gb_ref, wo_ref, x_ref, mod_ref, o_ref, m_ref, *, mi, nj):
    j = pl.program_id(2)
    tn = wa_ref.shape[1]

    @pl.when(j < nj)
    def _():
        ya = _dot(a_ref[0], wa_ref[...])
        yb = _dot(p_ref[0], wb_ref[...])
        m = jax.nn.sigmoid(ga_ref[0].astype(F32)) * ya + jax.nn.sigmoid(gb_ref[0].astype(F32)) * yb
        m_ref[:, pl.ds(pl.multiple_of(j * tn, tn), tn)] = m.astype(BF16)

    @pl.when(j >= nj)
    def _():
        o_ref[0] = x_ref[0] + mod_ref[0, pl.ds(mi, 1), :] * _dot(m_ref[...], wo_ref[...])


def _mixout(ag, yp, wa, wb, gates, wo, x, mods, mod_row, *, mi, tm, tn):
    bsz, s, ka = ag.shape
    kb = yp.shape[2]
    d = wa.shape[1]
    tm, tn = _tile(s, tm), _tile(d, tn)
    nj = d // tn

    def first(j):
        return jnp.minimum(j, nj - 1)

    def second(j):
        return jnp.maximum(j - nj, 0)

    return pl.pallas_call(
        functools.partial(_mixout_kernel, mi=mi, nj=nj),
        out_shape=jax.ShapeDtypeStruct((bsz, s, d), F32),
        grid=(bsz, s // tm, 2 * nj),
        in_specs=[pl.BlockSpec((1, tm, ka), lambda b, i, j: (b, i, 0)),
                  pl.BlockSpec((1, tm, kb), lambda b, i, j: (b, i, 0)),
                  pl.BlockSpec((ka, tn), lambda b, i, j: (0, first(j))),
                  pl.BlockSpec((kb, tn), lambda b, i, j: (0, first(j))),
                  pl.BlockSpec((1, tm, tn), lambda b, i, j: (b, i, first(j))),
                  pl.BlockSpec((1, tm, tn), lambda b, i, j: (b, i, nj + first(j))),
                  pl.BlockSpec((d, tn), lambda b, i, j: (0, second(j))),
                  pl.BlockSpec((1, tm, tn), lambda b, i, j: (b, i, second(j))),
                  pl.BlockSpec((1, N_MOD, tn), lambda b, i, j: (mod_row(b), 0, second(j)))],
        out_specs=pl.BlockSpec((1, tm, tn), lambda b, i, j: (b, i, second(j))),
        scratch_shapes=[pltpu.VMEM((tm, d), BF16)],
        compiler_params=_cparams(("arbitrary", "arbitrary", "arbitrary")),
        name="mixout",
    )(ag, yp, wa, wb, gates, gates, wo, x, mods)


def _ssm_tables(lam_re, lam_im, log_step, b_re, b_im, c_re, c_im):
    t = SSM_CHUNK
    hp = lax.Precision.HIGHEST
    lr = jnp.minimum(lam_re.astype(F32), LAMBDA_RE_MAX)
    li = lam_im.astype(F32)
    step = jnp.exp(log_step.astype(F32))[..., None]
    m = jnp.arange(t + 1, dtype=F32)[:, None, None, None]
    mag = jnp.exp(m * (lr * step)[None])
    ang = m * (li * step)[None]
    pw_re, pw_im = mag * jnp.cos(ang), mag * jnp.sin(ang)
    nr, ni = pw_re[1] - 1.0, pw_im[1]
    den = lr * lr + li * li
    q_re, q_im = (nr * lr + ni * li) / den, (ni * lr - nr * li) / den
    bb_re = q_re[..., None] * b_re - q_im[..., None] * b_im
    bb_im = q_re[..., None] * b_im + q_im[..., None] * b_re
    cr, ci = c_re.astype(F32), c_im.astype(F32)

    cl_re = cr[:, :, None] * pw_re[:t].transpose(1, 2, 0, 3)[:, :, :, None, :] \
        - ci[:, :, None] * pw_im[:t].transpose(1, 2, 0, 3)[:, :, :, None, :]
    cl_im = cr[:, :, None] * pw_im[:t].transpose(1, 2, 0, 3)[:, :, :, None, :] \
        + ci[:, :, None] * pw_re[:t].transpose(1, 2, 0, 3)[:, :, :, None, :]
    kern = jnp.einsum('dgtkp,dgpq->dgtkq', cl_re, bb_re, precision=hp) \
        - jnp.einsum('dgtkp,dgpq->dgtkq', cl_im, bb_im, precision=hp)
    g, kk = kern.shape[1], kern.shape[3]
    zero = (jnp.arange(t) == 0).astype(F32)[None, :, None, None]
    kpos = kern[0] + zero * kern[1]
    kneg = kern[1] + zero * kern[0]
    rpos = kpos.transpose(0, 3, 1, 2).reshape(g, kk, t * kk)
    rneg = kneg.transpose(0, 3, 1, 2).reshape(g, kk, t * kk)

    pf_re, pf_im = pw_re[:t][::-1, 0], pw_im[:t][::-1, 0]
    pb_re, pb_im = pw_re[:t, 1], pw_im[:t, 1]

    def st(p_re, p_im, d):
        re = p_re[:, :, :, None] * bb_re[d][None] - p_im[:, :, :, None] * bb_im[d][None]
        im = p_re[:, :, :, None] * bb_im[d][None] + p_im[:, :, :, None] * bb_re[d][None]
        return re.transpose(1, 0, 3, 2), im.transpose(1, 0, 3, 2)

    wf_re, wf_im = st(pf_re, pf_im, 0)
    wb_re, wb_im = st(pb_re, pb_im, 1)
    p = lr.shape[-1]
    wst = jnp.concatenate([wf_re, wb_re, wf_im, wb_im], axis=-1).reshape(g, t * kk, 4 * p)

    vf_pw_re, vf_pw_im = pw_re[1:, 0], pw_im[1:, 0]
    vb_pw_re, vb_pw_im = pw_re[1:, 1][::-1], pw_im[1:, 1][::-1]

    def rd(p_re, p_im, d):
        d_re = cr[d][None] * p_re[:, :, None, :] - ci[d][None] * p_im[:, :, None, :]
        d_im = cr[d][None] * p_im[:, :, None, :] + ci[d][None] * p_re[:, :, None, :]
        return d_re.transpose(1, 3, 0, 2), -d_im.transpose(1, 3, 0, 2)

    vf_re, vf_im = rd(vf_pw_re, vf_pw_im, 0)
    vb_re, vb_im = rd(vb_pw_re, vb_pw_im, 1)
    vrd = jnp.concatenate([vf_re, vb_re, vf_im, vb_im], axis=1).reshape(g, 4 * p, t * kk)

    a_re = jnp.concatenate([pw_re[t, 0], pw_re[t, 1]], axis=-1)[:, None, :]
    a_im = jnp.concatenate([pw_im[t, 0], pw_im[t, 1]], axis=-1)[:, None, :]
    return rpos.astype(BF16), rneg.astype(BF16), wst.astype(BF16), vrd.astype(BF16), a_re, a_im


SSM_GPB = LANES // SSM_GROUP


def _ssm_block_tables(rpos, rneg, wst, vrd, a_re, a_im):
    rpos, rneg, wst, vrd = lax.optimization_barrier((rpos, rneg, wst, vrd))
    g, _, tk = rpos.shape
    t, k, gpb = SSM_CHUNK, SSM_GROUP, SSM_GPB
    nb = g // gpb
    p2 = wst.shape[2] // 2
    tl, hw = t * gpb * k, gpb * p2
    rpos = rpos.reshape(nb, gpb * k, tk)
    rneg = rneg.reshape(nb, gpb * k, tk)
    rw = wst.reshape(nb, gpb, t, k, 2 * p2).transpose(0, 2, 1, 3, 4).reshape(nb, t, gpb * k, 2 * p2)
    rv = vrd.reshape(nb, gpb, 2, p2, tk).transpose(0, 2, 1, 3, 4).reshape(nb, 2, hw, tk)
    jj, kk = np.arange(tk) // k, np.arange(tk) % k
    sel = (jj[None, :, None] == np.arange(t)[:, None, None]) & (kk[None, :, None] == (np.arange(LANES) % k)[None, None, :])
    sel = jnp.asarray(sel, BF16)
    toep_b, wst_b, vrd_b = pl.pallas_call(
        functools.partial(_ssm_expand_kernel, t=t, k=k, p2=p2),
        out_shape=[jax.ShapeDtypeStruct((nb, tl, tl), BF16), jax.ShapeDtypeStruct((nb, tl, 2 * hw), BF16),
                   jax.ShapeDtypeStruct((nb, 2 * hw, tl), BF16)],
        grid=(nb, t),
        in_specs=[pl.BlockSpec((1, gpb * k, tk), lambda b, i: (b, 0, 0)),
                  pl.BlockSpec((1, gpb * k, tk), lambda b, i: (b, 0, 0)),
                  pl.BlockSpec((1, 1, gpb * k, 2 * p2), lambda b, i: (b, i, 0, 0)),
                  pl.BlockSpec((1, 2, hw, tk), lambda b, i: (b, 0, 0, 0)),
                  pl.BlockSpec(sel.shape, lambda b, i: (0, 0, 0))],
        out_specs=[pl.BlockSpec((1, gpb * k, tl), lambda b, i: (b, i, 0)),
                   pl.BlockSpec((1, gpb * k, 2 * hw), lambda b, i: (b, i, 0)),
                   pl.BlockSpec((1, 2 * hw, gpb * k), lambda b, i: (b, 0, i))],
        compiler_params=_cparams(("arbitrary", "arbitrary")),
        name="ssm_expand",
    )(rpos, rneg, rw, rv, sel)
    return toep_b, wst_b, vrd_b, a_re.reshape(nb, 1, hw), a_im.reshape(nb, 1, hw)


def _ssm_expand_kernel(rpos_ref, rneg_ref, rw_ref, rv_ref, sel_ref, toep_ref, wst_ref, vrd_ref, *, t, k, p2):
    i = pl.program_id(1)
    lb = rpos_ref.shape[1]
    hw = rv_ref.shape[2]
    gpb = lb // k
    same_g = (lax.broadcasted_iota(jnp.int32, (lb, lb), 0) // k) == (lax.broadcasted_iota(jnp.int32, (lb, lb), 1) // k)
    for j in range(t):
        pos = _dot(rpos_ref[0], sel_ref[jnp.clip(j - i, 0, t - 1)])
        neg = _dot(rneg_ref[0], sel_ref[jnp.clip(i - j, 0, t - 1)])
        blk = jnp.where(i <= j, pos, neg)
        toep_ref[0, :, j * lb:(j + 1) * lb] = jnp.where(same_g, blk, 0.0).astype(BF16)
    w = rw_ref[0, 0]
    same_gw = (lax.broadcasted_iota(jnp.int32, (lb, hw), 0) // k) == (lax.broadcasted_iota(jnp.int32, (lb, hw), 1) // p2)
    for r in range(2):
        wr = jnp.concatenate([w[:, r * p2:(r + 1) * p2]] * gpb, axis=-1)
        wst_ref[0, :, r * hw:(r + 1) * hw] = jnp.where(same_gw, wr, jnp.zeros_like(wr))
    same_gv = (lax.broadcasted_iota(jnp.int32, (hw, lb), 0) // p2) == (lax.broadcasted_iota(jnp.int32, (hw, lb), 1) // k)
    for r in range(2):
        v = _dot(rv_ref[0, r], sel_ref[i])
        vrd_ref[0, r * hw:(r + 1) * hw, :] = jnp.where(same_gv, v, 0.0).astype(BF16)


def _ssm_kernel(u_ref, ws_ref, ar_ref, ai_ref, h0_ref, *refs, nc, want_y):
    t = SSM_CHUNK
    if want_y:
        wt_ref, v_ref, d_ref, z_ref, hfin_ref, s_ref, yi_ref, hp_ref, tmp_ref, y_ref = refs
    else:
        hfin_ref, s_ref = refs
    a = jnp.concatenate([u_ref[0, pl.ds(j, nc, stride=t), :] for j in range(t)], axis=-1).astype(BF16)
    s = _dot(a, ws_ref[0])
    nlb = s.shape[1] // LANES
    gpb = nlb // 2
    for k in range(nlb):
        s_ref[pl.ds(k, nc, stride=nlb), :] = s[:, k * LANES:(k + 1) * LANES]
    if want_y:
        yi_ref[...] = _dot(a, wt_ref[0])

    ar = ar_ref[0]
    ai = ai_ref[0]
    fwd = lax.broadcasted_iota(jnp.int32, (gpb, LANES), 1) < (LANES // 2)

    def body(c, carry):
        h_re, h_im = carry
        rf = pl.multiple_of(c * nlb, nlb)
        rb = pl.multiple_of((nc - 1 - c) * nlb, nlb)
        if want_y:
            hp_ref[pl.ds(rf, gpb), :] = h_re
            hp_ref[pl.ds(rf + gpb, gpb), :] = h_im
            tmp_ref[pl.ds(rb, gpb), :] = h_re
            tmp_ref[pl.ds(rb + gpb, gpb), :] = h_im
        s_re = jnp.where(fwd, s_ref[pl.ds(rf, gpb), :], s_ref[pl.ds(rb, gpb), :])
        s_im = jnp.where(fwd, s_ref[pl.ds(rf + gpb, gpb), :], s_ref[pl.ds(rb + gpb, gpb), :])
        return ar * h_re - ai * h_im + s_re, ar * h_im + ai * h_re + s_im

    h0 = h0_ref[0, 0]
    h_re, h_im = lax.fori_loop(0, nc, body, (h0[:gpb], h0[gpb:]))
    hfin_ref[0, 0] = jnp.concatenate([h_re, h_im], axis=0)

    if want_y:
        fwd2 = lax.broadcasted_iota(jnp.int32, hp_ref.shape, 1) < (LANES // 2)
        hp_ref[...] = jnp.where(fwd2, hp_ref[...], tmp_ref[...])
        hp = jnp.concatenate([hp_ref[pl.ds(k, nc, stride=nlb), :] for k in range(nlb)], axis=-1).astype(BF16)
        y = yi_ref[...] + _dot(hp, v_ref[0])
        for j in range(t):
            y_ref[pl.ds(j, nc, stride=t), :] = y[:, j * LANES:(j + 1) * LANES]

        def rows(r0):
            r = pl.ds(r0, ROW_CHUNK)
            z_ref[0, r, :] = jax.nn.gelu(y_ref[r, :] + d_ref[...] * u_ref[0, r, :]).astype(BF16)

        _for_row_chunks(nc * t, rows)


def _ssm(u, tables, h0, d, *, want_y):
    toep_b, wst_b, vrd_b, ar, ai = tables
    bsz, n, _ = u.shape
    nb, tl, sw = wst_b.shape
    nc = n // SSM_CHUNK
    single = pl.Buffered(1)
    nlb = sw // LANES
    gpb = nlb // 2
    shapes = [jax.ShapeDtypeStruct((bsz, nb, nlb, LANES), F32)]
    specs = [pl.BlockSpec((1, 1, nlb, LANES), lambda k, b: (b, k, 0, 0))]
    scratch = [pltpu.VMEM((nc * nlb, LANES), F32)]
    in_specs = [pl.BlockSpec((1, n, LANES), lambda k, b: (b, 0, k)),
                pl.BlockSpec((1, tl, sw), lambda k, b: (k, 0, 0)),
                pl.BlockSpec((1, gpb, LANES), lambda k, b: (k, 0, 0)),
                pl.BlockSpec((1, gpb, LANES), lambda k, b: (k, 0, 0)),
                pl.BlockSpec((1, 1, nlb, LANES), lambda k, b: (b, k, 0, 0))]
    args = [u, wst_b, ar.reshape(nb, gpb, LANES), ai.reshape(nb, gpb, LANES), h0]
    if want_y:
        in_specs += [pl.BlockSpec((1, tl, tl), lambda k, b: (k, 0, 0)),
                     pl.BlockSpec((1, sw, tl), lambda k, b: (k, 0, 0), pipeline_mode=single),
                     pl.BlockSpec((1, LANES), lambda k, b: (0, k))]
        args += [toep_b, vrd_b, d]
        shapes.insert(0, jax.ShapeDtypeStruct((bsz, n, nb * LANES), BF16))
        specs.insert(0, pl.BlockSpec((1, n, LANES), lambda k, b: (b, 0, k)))
        scratch += [pltpu.VMEM((nc, tl), F32), pltpu.VMEM((nc * nlb, LANES), F32),
                    pltpu.VMEM((nc * nlb, LANES), F32), pltpu.VMEM((n, LANES), F32)]
    return pl.pallas_call(
        functools.partial(_ssm_kernel, nc=nc, want_y=want_y),
        out_shape=shapes,
        grid=(nb, bsz),
        in_specs=in_specs,
        out_specs=specs,
        scratch_shapes=scratch,
        compiler_params=_cparams(("arbitrary", "arbitrary")),
        name="ssm",
    )(*args)


def _glu_kernel(z_ref, w_ref, o_ref, *, ws):
    z = z_ref[0]
    a = _dot(z, w_ref[:, :ws])
    b = _dot(z, w_ref[:, ws:])
    o_ref[0] = (a * jax.nn.sigmoid(b)).astype(BF16)


def _glu(z, w_glu, *, tm):
    bsz, s, ws = z.shape
    tm = _tile(s, tm)
    return pl.pallas_call(
        functools.partial(_glu_kernel, ws=ws),
        out_shape=jax.ShapeDtypeStruct((bsz, s, ws), BF16),
        grid=(bsz, s // tm),
        in_specs=[pl.BlockSpec((1, tm, ws), lambda b, i: (b, i, 0)),
                  pl.BlockSpec(w_glu.shape, lambda b, i: (0, 0))],
        out_specs=pl.BlockSpec((1, tm, ws), lambda b, i: (b, i, 0)),
        compiler_params=_cparams(("arbitrary", "arbitrary")),
        name="glu",
    )(z, w_glu)


POOL_TILE = 4 * GRID_W


def _pool_consts(n):
    t = np.arange(POOL_TILE)
    pcs, invs = [], []
    r = np.arange(n) // GRID_W
    c = np.arange(n) % GRID_W
    rows = n // GRID_W
    for w in POOL_WINDOWS:
        lo, hi = w // 2, w - w // 2
        same_row = (t[:, None] // GRID_W) == (t[None, :] // GRID_W)
        dc = (t[None, :] % GRID_W) - (t[:, None] % GRID_W)
        pcs.append((same_row & (dc >= -lo) & (dc < hi)).astype(np.float32))
        cnt_r = np.minimum(r + hi, rows) - np.maximum(r - lo, 0)
        cnt_c = np.minimum(c + hi, GRID_W) - np.maximum(c - lo, 0)
        invs.append((1.0 / (cnt_r * cnt_c)).astype(np.float32)[:, None])
    return jnp.asarray(np.stack(pcs), BF16), jnp.asarray(np.stack(invs), F32)


def _pool_kernel(v_ref, pc_ref, inv_ref, pw_ref, sc_ref, o_ref, cs_ref, *, n, pad):
    wi = pl.program_id(0)
    nt = n // POOL_TILE
    ch = v_ref.shape[2]
    zeros = jnp.zeros((pad, ch), F32)
    cs_ref[pl.ds(0, pad), :] = zeros
    cs_ref[pl.ds(pad + n, pad), :] = zeros
    pc = pc_ref[0]

    def col_body(i, _):
        off = pl.multiple_of(i * POOL_TILE, POOL_TILE)
        v = v_ref[0, pl.ds(off, POOL_TILE), :]
        hi = v.astype(BF16)
        lo = (v - hi.astype(F32)).astype(BF16)
        cs_ref[pl.ds(pad + off, POOL_TILE), :] = _dot(pc, hi) + _dot(pc, lo)
        return 0

    lax.fori_loop(0, nt, col_body, 0)

    for k, w in enumerate(POOL_WINDOWS):
        @pl.when(wi == k)
        def _(w=w):
            def row_body(i, _):
                off = pl.multiple_of(i * POOL_TILE, POOL_TILE)
                acc = cs_ref[pl.ds(pad + off - (w // 2) * GRID_W, POOL_TILE), :]
                for dlt in range(-(w // 2) + 1, w - w // 2):
                    acc = acc + cs_ref[pl.ds(pad + off + dlt * GRID_W, POOL_TILE), :]
                mixed = acc * inv_ref[0, pl.ds(off, POOL_TILE), :] - v_ref[0, pl.ds(off, POOL_TILE), :]
                y = _dot(mixed.astype(BF16), pw_ref[0]) * sc_ref[0]
                o_ref[0, pl.ds(off, POOL_TILE), :] = y.astype(BF16)
                return 0

            lax.fori_loop(0, nt, row_body, 0)


def _pool(usp, col0, pool_w, pool_scale, pc, inv):
    bsz, n, _ = usp.shape
    nw, ch, _ = pool_w.shape
    pad = (max(POOL_WINDOWS) // 2) * GRID_W
    cb0 = col0 // ch
    return pl.pallas_call(
        functools.partial(_pool_kernel, n=n, pad=pad),
        out_shape=jax.ShapeDtypeStruct((bsz, n, nw * ch), BF16),
        grid=(nw, bsz),
        in_specs=[pl.BlockSpec((1, n, ch), lambda w, b: (b, 0, cb0 + w)),
                  pl.BlockSpec((1, POOL_TILE, POOL_TILE), lambda w, b: (w, 0, 0)),
                  pl.BlockSpec((1, n, 1), lambda w, b: (w, 0, 0)),
                  pl.BlockSpec((1, ch, ch), lambda w, b: (w, 0, 0)),
                  pl.BlockSpec((1, 1, ch), lambda w, b: (w, 0, 0))],
        out_specs=pl.BlockSpec((1, n, ch), lambda w, b: (b, 0, w)),
        scratch_shapes=[pltpu.VMEM((n + 2 * pad, ch), F32)],
        compiler_params=_cparams(("arbitrary", "arbitrary")),
        name="pool",
    )(usp, pc, inv, pool_w, pool_scale)


def kernel(x, c, ctx, c_ctx, w_mod, b_mod, norm_g, final_g, ffn1_w_in, ffn1_w_out, ffn2_w_in, ffn2_w_out,
           w_in, ssm_lambda_re, ssm_lambda_im, ssm_log_step, ssm_b_re, ssm_b_im, ssm_c_re, ssm_c_im, ssm_d,
           w_glu, w_branch_a, pool_w, pool_scale, w_branch_b, w_out):
    bsz, seq, d = x.shape
    assert w_mod.shape[0] == 1, "single-layer problem"
    ssm_w = ssm_d.shape[1]
    pool_width = pool_scale.shape[1]
    nw = len(POOL_WINDOWS)
    pch = pool_width // nw
    p = ssm_lambda_re.shape[-1]
    assert seq % POOL_TILE == 0 and ctx.shape[1] % SSM_CHUNK == 0 and ssm_w % LANES == 0

    f1_in, f1_out = ffn1_w_in[0].astype(BF16), ffn1_w_out[0].astype(BF16)
    w_s = w_in[0][:, :ssm_w].astype(BF16)
    late = [ffn2_w_in[0], ffn2_w_out[0], w_in[0], w_out[0]]
    wbb, wba, wglu, pw = (w_branch_b[0].astype(BF16), w_branch_a[0].astype(BF16), w_glu[0].astype(BF16),
                          pool_w[0].astype(BF16))
    psc = pool_scale[0].reshape(nw, 1, pch)
    ng = norm_g[0]
    fg = final_g.reshape(1, d)

    rows = -(-(bsz + 1) // 8) * 8
    c8 = jnp.zeros((rows, d), F32).at[:bsz].set(c).at[bsz].set(c_ctx)
    mods = _mod(c8, w_mod[0], b_mod, tn=_tile(w_mod.shape[2], MOD_COL_TILE)).reshape(rows, N_MOD, d)
    lat = lambda b: b
    con = lambda b: bsz
    ffn_tiles = dict(tm=FFN_ROW_TILE, tf=FFN_HIDDEN_TILE)
    mm_tiles = dict(tm=MM_ROW_TILE, tn=MM_COL_TILE)

    tables = _ssm_block_tables(*_ssm_tables(ssm_lambda_re[0], ssm_lambda_im[0], ssm_log_step[0],
                                            ssm_b_re[0], ssm_b_im[0], ssm_c_re[0], ssm_c_im[0]))

    nctx = ctx.shape[1]
    _, uc = _ffn(ctx.reshape(1, bsz * nctx, d), ng, mods, con, f1_in, f1_out, fg, gi=0, mi=0, next_pre=(1, 3),
                 **ffn_tiles)
    us_c = _mm(uc, w_s, F32, **mm_tiles).reshape(bsz, nctx, ssm_w)
    assert 2 * p == LANES, "fwd | bwd states of one group fill one lane tile"
    h_ctx, = _ssm(us_c, tables, jnp.zeros((bsz, ssm_w // LANES, 2 * SSM_GPB, LANES), F32), ssm_d, want_y=False)

    nsteps = bsz * (seq // _tile(seq, FFN_ROW_TILE)) * (f1_out.shape[0] // FFN_HIDDEN_TILE)
    plans = [_cast_plan(a, nsteps) for a in late]
    x1, u, *cast = _ffn(x, ng, mods, lat, f1_in, f1_out, fg, gi=0, mi=0, next_pre=(1, 3),
                        casts=[(a, pln) for a, pln in zip(late, plans) if pln is not None], **ffn_tiles)
    cast = iter(cast)
    f2_in, f2_out, w_all, wo = [a.astype(BF16) if pln is None else next(cast) for a, pln in zip(late, plans)]
    usp = _mm(u, w_all, F32, ncols=ssm_w + pool_width, **mm_tiles)
    gates = _mm(u, w_all, BF16, col0=ssm_w + pool_width, **mm_tiles)

    z, _ = _ssm(usp, tables, h_ctx, ssm_d, want_y=True)
    ag = _glu(z, wglu, tm=GLU_ROW_TILE)

    pc, inv = _pool_consts(seq)
    yp = _pool(usp, ssm_w, pw, psc, pc, inv)

    x2 = _mixout(ag, yp, wba, wbb, gates, wo, x1, mods, lat, mi=5, tm=MIX_ROW_TILE, tn=MM_COL_TILE)
    out, = _ffn(x2, ng, mods, lat, f2_in, f2_out, fg, gi=2, mi=6, final_norm=True, **ffn_tiles)
    return out
```

```python
import functools
import math

import jax
import jax.numpy as jnp
import numpy as np
from jax import lax
from jax.experimental import pallas as pl
from jax.experimental.pallas import tpu as pltpu

BF16 = jnp.bfloat16
F32 = jnp.float32

RMS_EPS = 1e-6
LAMBDA_RE_MAX = -1e-4
HALF = 0.5
N_MOD = 9
SSM_GROUP = 16
POOL_WINDOWS = (2, 4, 8, 16)
GRID_W = 64
SSM_CHUNK = 16

LANES = 128
BF16_ROWS = 16
V7X_VMEM_BYTES = 64 * 1024 * 1024
VMEM_LIMIT = V7X_VMEM_BYTES - 4 * 1024 * 1024

FFN_ROW_TILE = 1024
FFN_HIDDEN_TILE = 256
MM_ROW_TILE = 2048
MIX_ROW_TILE = 1024
MM_COL_TILE = 512
GLU_ROW_TILE = 1024
MOD_COL_TILE = 512


def _cparams(sem):
    return pltpu.CompilerParams(dimension_semantics=sem, vmem_limit_bytes=VMEM_LIMIT)


def _tile(n, pref):
    t = min(n, pref)
    while n % t:
        t //= 2
    return t


ROW_CHUNK = 64


def _for_row_chunks(n, fn):
    def body(i, carry):
        fn(pl.multiple_of(i * ROW_CHUNK, ROW_CHUNK))
        return carry

    lax.fori_loop(0, n // ROW_CHUNK, body, 0)


NORM_ROWS = 16
NORM_UNROLL = 4


def _for_rows(n, step, fn):
    def body(i, carry):
        fn(pl.multiple_of(i * step, step))
        return carry

    lax.fori_loop(0, n // step, body, 0, unroll=NORM_UNROLL)


def _dot(a, b):
    return jnp.dot(a, b, preferred_element_type=F32)


def _mod_kernel(c_ref, w_ref, b_ref, o_ref):
    c = c_ref[...]
    s = (c * jax.nn.sigmoid(c)).astype(BF16)
    o_ref[...] = _dot(s, w_ref[...].astype(BF16)) + b_ref[...]


def _mod(c8, w_mod, b_mod, tn):
    d, n = w_mod.shape
    return pl.pallas_call(
        _mod_kernel,
        out_shape=jax.ShapeDtypeStruct((c8.shape[0], n), F32),
        grid=(n // tn,),
        in_specs=[pl.BlockSpec(c8.shape, lambda j: (0, 0)),
                  pl.BlockSpec((d, tn), lambda j: (0, j)),
                  pl.BlockSpec((1, tn), lambda j: (0, j))],
        out_specs=pl.BlockSpec((c8.shape[0], tn), lambda j: (0, j)),
        compiler_params=_cparams(("arbitrary",)),
        name="mod",
    )(c8, w_mod, b_mod)


FFN_STAGE_ROWS = 128


def _ffn_kernel(x_ref, g_ref, mod_ref, wg_ref, wu_ref, wo_ref, fg_ref, *refs,
                gi, mi, nf, nstage, final_norm, next_pre, cast_blocks):
    nc = len(cast_blocks)
    src_refs, refs = refs[:nc], refs[nc:]
    if next_pre is None:
        o_ref, refs = refs[0], refs[1:]
    else:
        o_ref, u_ref, refs = refs[0], refs[1], refs[2:]
    dst_refs, (pre_ref, acc_ref, vec_ref, r_ref) = refs[:nc], refs[nc:]
    f = pl.program_id(2)
    rc, d = x_ref.shape[1], x_ref.shape[2]
    nr = NORM_ROWS

    def bcast(v):
        return jnp.broadcast_to(v, (nr, d))

    def inv_rms(y):
        ss = jnp.sum(y * y, axis=-1, keepdims=True)
        return jnp.broadcast_to(lax.rsqrt(ss * (1.0 / d) + RMS_EPS), (nr, LANES))

    def lanes(r):
        return jnp.concatenate([r] * (d // LANES), axis=-1)

    @pl.when(f == 0)
    def _():
        vec_ref[0] = bcast(g_ref[pl.ds(gi, 1), :] * (1.0 + mod_ref[0, pl.ds(mi + 1, 1), :]))
        vec_ref[1] = bcast(mod_ref[0, pl.ds(mi, 1), :])
        vec_ref[2] = bcast(HALF * mod_ref[0, pl.ds(mi + 2, 1), :])
        if final_norm:
            vec_ref[3] = bcast(fg_ref[...])
        if next_pre is not None:
            gi2, mi2 = next_pre
            vec_ref[3] = bcast(g_ref[pl.ds(gi2, 1), :] * (1.0 + mod_ref[0, pl.ds(mi2 + 1, 1), :]))
            vec_ref[4] = bcast(mod_ref[0, pl.ds(mi2, 1), :])

    @pl.when(f < nstage)
    def _():
        base = pl.multiple_of(f * rc, rc)

        def stats(r0):
            r_ref[pl.ds(r0, nr), :] = inv_rms(x_ref[0, pl.ds(r0, nr), :])

        def apply(r0):
            pre = x_ref[0, pl.ds(r0, nr), :] * lanes(r_ref[pl.ds(r0, nr), :]) * vec_ref[0] + vec_ref[1]
            pre_ref[pl.ds(base + r0, nr), :] = pre.astype(BF16)
            acc_ref[pl.ds(base + r0, nr), :] = jnp.zeros((nr, d), F32)

        _for_rows(rc, nr, stats)
        _for_rows(rc, nr, apply)

    @pl.when((f >= nstage) & (f < nstage + nf))
    def _():
        p = pre_ref[...]
        gate = _dot(p, wg_ref[...])
        up = _dot(p, wu_ref[...])
        act = (gate * jax.nn.sigmoid(gate) * up).astype(BF16)
        acc_ref[...] += _dot(act, wo_ref[...])
        step = (pl.program_id(0) * pl.num_programs(1) + pl.program_id(1)) * nf + (f - nstage)
        for src, dst, nblk in zip(src_refs, dst_refs, cast_blocks):
            @pl.when(step < nblk)
            def _(src=src, dst=dst):
                dst[...] = src[...].astype(BF16)

    @pl.when(f >= nstage + nf)
    def _():
        base = pl.multiple_of((f - (nstage + nf)) * rc, rc)

        renorm = final_norm or next_pre is not None

        def residual(r0):
            r = pl.ds(r0, nr)
            y = x_ref[0, r, :] + vec_ref[2] * acc_ref[pl.ds(base + r0, nr), :]
            o_ref[0, r, :] = y
            if renorm:
                r_ref[r, :] = inv_rms(y)

        def norm(r0):
            r = pl.ds(r0, nr)
            yn = o_ref[0, r, :] * lanes(r_ref[r, :])
            if final_norm:
                o_ref[0, r, :] = yn * vec_ref[3]
            if next_pre is not None:
                u_ref[0, r, :] = (yn * vec_ref[3] + vec_ref[4]).astype(BF16)

        _for_rows(rc, nr, residual)
        if renorm:
            _for_rows(rc, nr, norm)


def _cast_plan(arr, nsteps):
    r, c = arr.shape
    for parts in (8, 4, 2, 1):
        if c % (parts * LANES):
            continue
        br = BF16_ROWS
        while br <= r:
            if r % br == 0 and (r // br) * parts <= nsteps:
                return br, c // parts
            br *= 2
    return None


def _ffn(x, norm_g, mods, mod_row, w_in, w_out, final_g, *, gi, mi, tm, tf, final_norm=False, next_pre=None,
         casts=()):
    bsz, s, d = x.shape
    ff = w_out.shape[0]
    nf = ff // tf
    tm = _tile(s, tm)
    rc = _tile(tm, FFN_STAGE_ROWS)
    nstage = tm // rc
    ni = s // tm

    def hid(f):
        return jnp.clip(f - nstage, 0, nf - 1)

    cast_blocks, cast_in, cast_out, cast_shapes = [], [], [], []
    for arr, (br, bc) in casts:
        nrb, ncb = arr.shape[0] // br, arr.shape[1] // bc
        assert nrb * br == arr.shape[0] and ncb * bc == arr.shape[1] and nrb * ncb <= bsz * ni * nf

        def blk(b, i, f, nrb=nrb, ncb=ncb):
            step = jnp.minimum((b * ni + i) * nf + hid(f), nrb * ncb - 1)
            return step // ncb, step % ncb

        cast_blocks.append(nrb * ncb)
        cast_in.append(pl.BlockSpec((br, bc), blk))
        cast_out.append(pl.BlockSpec((br, bc), blk))
        cast_shapes.append(jax.ShapeDtypeStruct(arr.shape, BF16))
    kern = functools.partial(_ffn_kernel, gi=gi, mi=mi, nf=nf, nstage=nstage, final_norm=final_norm,
                             next_pre=next_pre, cast_blocks=tuple(cast_blocks))

    def x_blk(b, i, f):
        return b, i * nstage + jnp.where(f < nstage, f, jnp.clip(f - (nstage + nf), 0, nstage - 1)), 0

    def o_blk(b, i, f):
        return b, i * nstage + jnp.clip(f - (nstage + nf), 0, nstage - 1), 0

    out_shape = [jax.ShapeDtypeStruct((bsz, s, d), F32)]
    out_specs = [pl.BlockSpec((1, rc, d), o_blk)]
    if next_pre is not None:
        out_shape.append(jax.ShapeDtypeStruct((bsz, s, d), BF16))
        out_specs.append(pl.BlockSpec((1, rc, d), o_blk))
    return pl.pallas_call(
        kern,
        out_shape=out_shape + cast_shapes,
        grid=(bsz, ni, nf + 2 * nstage),
        in_specs=[pl.BlockSpec((1, rc, d), x_blk),
                  pl.BlockSpec(norm_g.shape, lambda b, i, f: (0, 0)),
                  pl.BlockSpec((1, N_MOD, d), lambda b, i, f: (mod_row(b), 0, 0)),
                  pl.BlockSpec((d, tf), lambda b, i, f: (0, hid(f))),
                  pl.BlockSpec((d, tf), lambda b, i, f: (0, nf + hid(f))),
                  pl.BlockSpec((tf, d), lambda b, i, f: (hid(f), 0)),
                  pl.BlockSpec((1, d), lambda b, i, f: (0, 0))] + cast_in,
        out_specs=out_specs + cast_out,
        scratch_shapes=[pltpu.VMEM((tm, d), BF16), pltpu.VMEM((tm, d), F32),
                        pltpu.VMEM((5, NORM_ROWS, d), F32), pltpu.VMEM((rc, LANES), F32)],
        compiler_params=_cparams(("arbitrary", "arbitrary", "arbitrary")),
        name="ffn",
    )(x, norm_g, mods, w_in, w_in, w_out, final_g, *[arr for arr, _ in casts])


def _mm_kernel(a_ref, b_ref, o_ref):
    o_ref[0] = _dot(a_ref[0], b_ref[...]).astype(o_ref.dtype)


def _mm(a, w, out_dtype, *, tm, tn, col0=0, ncols=None):
    bsz, s, k = a.shape
    n = w.shape[1] - col0 if ncols is None else ncols
    tm, tn = _tile(s, tm), _tile(math.gcd(n, col0) if col0 else n, tn)
    j0 = col0 // tn
    return pl.pallas_call(
        _mm_kernel,
        out_shape=jax.ShapeDtypeStruct((bsz, s, n), out_dtype),
        grid=(bsz, s // tm, n // tn),
        in_specs=[pl.BlockSpec((1, tm, k), lambda b, i, j: (b, i, 0)),
                  pl.BlockSpec((k, tn), lambda b, i, j: (0, j0 + j))],
        out_specs=pl.BlockSpec((1, tm, tn), lambda b, i, j: (b, i, j)),
        compiler_params=_cparams(("arbitrary", "arbitrary", "arbitrary")),
        name="mm",
    )(a, w)


def _mixout_kernel(a_ref, p_ref, wa_ref, wb_ref, ga_ref, gb_ref, wo_ref, x_ref, mod_ref, o_ref, m_ref, *, mi, nj):
    j = pl.program_id(2)
    tn = wa_ref.shape[1]

    @pl.when(j < nj)
    def _():
        ya = _dot(a_ref[0], wa_ref[...])
        yb = _dot(p_ref[0], wb_ref[...])
        m = jax.nn.sigmoid(ga_ref[0].astype(F32)) * ya + jax.nn.sigmoid(gb_ref[0].astype(F32)) * yb
        m_ref[:, pl.ds(pl.multiple_of(j * tn, tn), tn)] = m.astype(BF16)

    @pl.when(j >= nj)
    def _():
        o_ref[0] = x_ref[0] + mod_ref[0, pl.ds(mi, 1), :] * _dot(m_ref[...], wo_ref[...])


def _mixout(ag, yp, wa, wb, gates, wo, x, mods, mod_row, *, mi, tm, tn):
    bsz, s, ka = ag.shape
    kb = yp.shape[2]
    d = wa.shape[1]
    tm, tn = _tile(s, tm), _tile(d, tn)
    nj = d // tn

    def first(j):
        return jnp.minimum(j, nj - 1)

    def second(j):
        return jnp.maximum(j - nj, 0)

    return pl.pallas_call(
        functools.partial(_mixout_kernel, mi=mi, nj=nj),
        out_shape=jax.ShapeDtypeStruct((bsz, s, d), F32),
        grid=(bsz, s // tm, 2 * nj),
        in_specs=[pl.BlockSpec((1, tm, ka), lambda b, i, j: (b, i, 0)),
                  pl.BlockSpec((1, tm, kb), lambda b, i, j: (b, i, 0)),
                  pl.BlockSpec((ka, tn), lambda b, i, j: (0, first(j))),
                  pl.BlockSpec((kb, tn), lambda b, i, j: (0, first(j))),
                  pl.BlockSpec((1, tm, tn), lambda b, i, j: (b, i, first(j))),
                  pl.BlockSpec((1, tm, tn), lambda b, i, j: (b, i, nj + first(j))),
                  pl.BlockSpec((d, tn), lambda b, i, j: (0, second(j))),
                  pl.BlockSpec((1, tm, tn), lambda b, i, j: (b, i, second(j))),
                  pl.BlockSpec((1, N_MOD, tn), lambda b, i, j: (mod_row(b), 0, second(j)))],
        out_specs=pl.BlockSpec((1, tm, tn), lambda b, i, j: (b, i, second(j))),
        scratch_shapes=[pltpu.VMEM((tm, d), BF16)],
        compiler_params=_cparams(("arbitrary", "arbitrary", "arbitrary")),
        name="mixout",
    )(ag, yp, wa, wb, gates, gates, wo, x, mods)


def _ssm_tables(lam_re, lam_im, log_step, b_re, b_im, c_re, c_im):
    t = SSM_CHUNK
    hp = lax.Precision.HIGHEST
    lr = jnp.minimum(lam_re.astype(F32), LAMBDA_RE_MAX)
    li = lam_im.astype(F32)
    step = jnp.exp(log_step.astype(F32))[..., None]
    m = jnp.arange(t + 1, dtype=F32)[:, None, None, None]
    mag = jnp.exp(m * (lr * step)[None])
    ang = m * (li * step)[None]
    pw_re, pw_im = mag * jnp.cos(ang), mag * jnp.sin(ang)
    nr, ni = pw_re[1] - 1.0, pw_im[1]
    den = lr * lr + li * li
    q_re, q_im = (nr * lr + ni * li) / den, (ni * lr - nr * li) / den
    bb_re = q_re[..., None] * b_re - q_im[..., None] * b_im
    bb_im = q_re[..., None] * b_im + q_im[..., None] * b_re
    cr, ci = c_re.astype(F32), c_im.astype(F32)

    cl_re = cr[:, :, None] * pw_re[:t].transpose(1, 2, 0, 3)[:, :, :, None, :] \
        - ci[:, :, None] * pw_im[:t].transpose(1, 2, 0, 3)[:, :, :, None, :]
    cl_im = cr[:, :, None] * pw_im[:t].transpose(1, 2, 0, 3)[:, :, :, None, :] \
        + ci[:, :, None] * pw_re[:t].transpose(1, 2, 0, 3)[:, :, :, None, :]
    kern = jnp.einsum('dgtkp,dgpq->dgtkq', cl_re, bb_re, precision=hp) \
        - jnp.einsum('dgtkp,dgpq->dgtkq', cl_im, bb_im, precision=hp)
    g, kk = kern.shape[1], kern.shape[3]
    zero = (jnp.arange(t) == 0).astype(F32)[None, :, None, None]
    kpos = kern[0] + zero * kern[1]
    kneg = kern[1] + zero * kern[0]
    rpos = kpos.transpose(0, 3, 1, 2).reshape(g, kk, t * kk)
    rneg = kneg.transpose(0, 3, 1, 2).reshape(g, kk, t * kk)

    pf_re, pf_im = pw_re[:t][::-1, 0], pw_im[:t][::-1, 0]
    pb_re, pb_im = pw_re[:t, 1], pw_im[:t, 1]

    def st(p_re, p_im, d):
        re = p_re[:, :, :, None] * bb_re[d][None] - p_im[:, :, :, None] * bb_im[d][None]
        im = p_re[:, :, :, None] * bb_im[d][None] + p_im[:, :, :, None] * bb_re[d][None]
        return re.transpose(1, 0, 3, 2), im.transpose(1, 0, 3, 2)

    wf_re, wf_im = st(pf_re, pf_im, 0)
    wb_re, wb_im = st(pb_re, pb_im, 1)
    p = lr.shape[-1]
    wst = jnp.concatenate([wf_re, wb_re, wf_im, wb_im], axis=-1).reshape(g, t * kk, 4 * p)

    vf_pw_re, vf_pw_im = pw_re[1:, 0], pw_im[1:, 0]
    vb_pw_re, vb_pw_im = pw_re[1:, 1][::-1], pw_im[1:, 1][::-1]

    def rd(p_re, p_im, d):
        d_re = cr[d][None] * p_re[:, :, None, :] - ci[d][None] * p_im[:, :, None, :]
        d_im = cr[d][None] * p_im[:, :, None, :] + ci[d][None] * p_re[:, :, None, :]
        return d_re.transpose(1, 3, 0, 2), -d_im.transpose(1, 3, 0, 2)

    vf_re, vf_im = rd(vf_pw_re, vf_pw_im, 0)
    vb_re, vb_im = rd(vb_pw_re, vb_pw_im, 1)
    vrd = jnp.concatenate([vf_re, vb_re, vf_im, vb_im], axis=1).reshape(g, 4 * p, t * kk)

    a_re = jnp.concatenate([pw_re[t, 0], pw_re[t, 1]], axis=-1)[:, None, :]
    a_im = jnp.concatenate([pw_im[t, 0], pw_im[t, 1]], axis=-1)[:, None, :]
    return rpos.astype(BF16), rneg.astype(BF16), wst.astype(BF16), vrd.astype(BF16), a_re, a_im


SSM_GPB = LANES // SSM_GROUP


def _ssm_block_tables(rpos, rneg, wst, vrd, a_re, a_im):
    rpos, rneg, wst, vrd = lax.optimization_barrier((rpos, rneg, wst, vrd))
    g, _, tk = rpos.shape
    t, k, gpb = SSM_CHUNK, SSM_GROUP, SSM_GPB
    nb = g // gpb
    p2 = wst.shape[2] // 2
    tl, hw = t * gpb * k, gpb * p2
    rpos = rpos.reshape(nb, gpb * k, tk)
    rneg = rneg.reshape(nb, gpb * k, tk)
    rw = wst.reshape(nb, gpb, t, k, 2 * p2).transpose(0, 2, 1, 3, 4).reshape(nb, t, gpb * k, 2 * p2)
    rv = vrd.reshape(nb, gpb, 2, p2, tk).transpose(0, 2, 1, 3, 4).reshape(nb, 2, hw, tk)
    jj, kk = np.arange(tk) // k, np.arange(tk) % k
    sel = (jj[None, :, None] == np.arange(t)[:, None, None]) & (kk[None, :, None] == (np.arange(LANES) % k)[None, None, :])
    sel = jnp.asarray(sel, BF16)
    toep_b, wst_b, vrd_b = pl.pallas_call(
        functools.partial(_ssm_expand_kernel, t=t, k=k, p2=p2),
        out_shape=[jax.ShapeDtypeStruct((nb, tl, tl), BF16), jax.ShapeDtypeStruct((nb, tl, 2 * hw), BF16),
                   jax.ShapeDtypeStruct((nb, 2 * hw, tl), BF16)],
        grid=(nb, t),
        in_specs=[pl.BlockSpec((1, gpb * k, tk), lambda b, i: (b, 0, 0)),
                  pl.BlockSpec((1, gpb * k, tk), lambda b, i: (b, 0, 0)),
                  pl.BlockSpec((1, 1, gpb * k, 2 * p2), lambda b, i: (b, i, 0, 0)),
                  pl.BlockSpec((1, 2, hw, tk), lambda b, i: (b, 0, 0, 0)),
                  pl.BlockSpec(sel.shape, lambda b, i: (0, 0, 0))],
        out_specs=[pl.BlockSpec((1, gpb * k, tl), lambda b, i: (b, i, 0)),
                   pl.BlockSpec((1, gpb * k, 2 * hw), lambda b, i: (b, i, 0)),
                   pl.BlockSpec((1, 2 * hw, gpb * k), lambda b, i: (b, 0, i))],
        scratch_shapes=[pltpu.VMEM((2 * t - 1, gpb * k, gpb * k), BF16)],
        compiler_params=_cparams(("arbitrary", "arbitrary")),
        name="ssm_expand",
    )(rpos, rneg, rw, rv, sel)
    return toep_b, wst_b, vrd_b, a_re.reshape(nb, 1, hw), a_im.reshape(nb, 1, hw)


def _ssm_expand_kernel(rpos_ref, rneg_ref, rw_ref, rv_ref, sel_ref, toep_ref, wst_ref, vrd_ref, bd_ref, *, t, k, p2):
    i = pl.program_id(1)
    lb = rpos_ref.shape[1]
    hw = rv_ref.shape[2]
    gpb = lb // k
    @pl.when(i == 0)
    def _():
        same_g = (lax.broadcasted_iota(jnp.int32, (lb, lb), 0) // k) == (lax.broadcasted_iota(jnp.int32, (lb, lb), 1) // k)
        for m in range(t):
            bd_ref[t - 1 + m] = jnp.where(same_g, _dot(rpos_ref[0], sel_ref[m]), 0.0).astype(BF16)
            if m:
                bd_ref[t - 1 - m] = jnp.where(same_g, _dot(rneg_ref[0], sel_ref[m]), 0.0).astype(BF16)

    for j in range(t):
        toep_ref[0, :, j * lb:(j + 1) * lb] = bd_ref[j - i + (t - 1)]
    w = rw_ref[0, 0]
    same_gw = (lax.broadcasted_iota(jnp.int32, (lb, hw), 0) // k) == (lax.broadcasted_iota(jnp.int32, (lb, hw), 1) // p2)
    for r in range(2):
        wr = jnp.concatenate([w[:, r * p2:(r + 1) * p2]] * gpb, axis=-1)
        wst_ref[0, :, r * hw:(r + 1) * hw] = jnp.where(same_gw, wr, jnp.zeros_like(wr))
    same_gv = (lax.broadcasted_iota(jnp.int32, (hw, lb), 0) // p2) == (lax.broadcasted_iota(jnp.int32, (hw, lb), 1) // k)
    for r in range(2):
        v = _dot(rv_ref[0, r], sel_ref[i])
        vrd_ref[0, r * hw:(r + 1) * hw, :] = jnp.where(same_gv, v, 0.0).astype(BF16)


def _ssm_kernel(u_ref, ws_ref, ar_ref, ai_ref, h0_ref, *refs, nc, want_y):
    t = SSM_CHUNK
    if want_y:
        wt_ref, v_ref, d_ref, z_ref, hfin_ref, s_ref, yi_ref, hp_ref, tmp_ref, y_ref = refs
    else:
        hfin_ref, s_ref = refs
    a = jnp.concatenate([u_ref[0, pl.ds(j, nc, stride=t), :] for j in range(t)], axis=-1).astype(BF16)
    s = _dot(a, ws_ref[0])
    nlb = s.shape[1] // LANES
    gpb = nlb // 2
    for k in range(nlb):
        s_ref[pl.ds(k, nc, stride=nlb), :] = s[:, k * LANES:(k + 1) * LANES]
    if want_y:
        yi_ref[...] = _dot(a, wt_ref[0])

    ar = ar_ref[0]
    ai = ai_ref[0]
    fwd = lax.broadcasted_iota(jnp.int32, (gpb, LANES), 1) < (LANES // 2)

    def body(c, carry):
        h_re, h_im = carry
        rf = pl.multiple_of(c * nlb, nlb)
        rb = pl.multiple_of((nc - 1 - c) * nlb, nlb)
        if want_y:
            hp_ref[pl.ds(rf, gpb), :] = h_re
            hp_ref[pl.ds(rf + gpb, gpb), :] = h_im
            tmp_ref[pl.ds(rb, gpb), :] = h_re
            tmp_ref[pl.ds(rb + gpb, gpb), :] = h_im
        s_re = jnp.where(fwd, s_ref[pl.ds(rf, gpb), :], s_ref[pl.ds(rb, gpb), :])
        s_im = jnp.where(fwd, s_ref[pl.ds(rf + gpb, gpb), :], s_ref[pl.ds(rb + gpb, gpb), :])
        return ar * h_re - ai * h_im + s_re, ar * h_im + ai * h_re + s_im

    h0 = h0_ref[0, 0]
    h_re, h_im = lax.fori_loop(0, nc, body, (h0[:gpb], h0[gpb:]))
    hfin_ref[0, 0] = jnp.concatenate([h_re, h_im], axis=0)

    if want_y:
        fwd2 = lax.broadcasted_iota(jnp.int32, hp_ref.shape, 1) < (LANES // 2)
        hp_ref[...] = jnp.where(fwd2, hp_ref[...], tmp_ref[...])
        hp = jnp.concatenate([hp_ref[pl.ds(k, nc, stride=nlb), :] for k in range(nlb)], axis=-1).astype(BF16)
        y = yi_ref[...] + _dot(hp, v_ref[0])
        for j in range(t):
            y_ref[pl.ds(j, nc, stride=t), :] = y[:, j * LANES:(j + 1) * LANES]

        def rows(r0):
            r = pl.ds(r0, ROW_CHUNK)
            z_ref[0, r, :] = jax.nn.gelu(y_ref[r, :] + d_ref[...] * u_ref[0, r, :]).astype(BF16)

        _for_row_chunks(nc * t, rows)


def _ssm(u, tables, h0, d, *, want_y):
    toep_b, wst_b, vrd_b, ar, ai = tables
    bsz, n, _ = u.shape
    nb, tl, sw = wst_b.shape
    nc = n // SSM_CHUNK
    single = pl.Buffered(1)
    nlb = sw // LANES
    gpb = nlb // 2
    shapes = [jax.ShapeDtypeStruct((bsz, nb, nlb, LANES), F32)]
    specs = [pl.BlockSpec((1, 1, nlb, LANES), lambda k, b: (b, k, 0, 0))]
    scratch = [pltpu.VMEM((nc * nlb, LANES), F32)]
    in_specs = [pl.BlockSpec((1, n, LANES), lambda k, b: (b, 0, k)),
                pl.BlockSpec((1, tl, sw), lambda k, b: (k, 0, 0)),
                pl.BlockSpec((1, gpb, LANES), lambda k, b: (k, 0, 0)),
                pl.BlockSpec((1, gpb, LANES), lambda k, b: (k, 0, 0)),
                pl.BlockSpec((1, 1, nlb, LANES), lambda k, b: (b, k, 0, 0))]
    args = [u, wst_b, ar.reshape(nb, gpb, LANES), ai.reshape(nb, gpb, LANES), h0]
    if want_y:
        in_specs += [pl.BlockSpec((1, tl, tl), lambda k, b: (k, 0, 0)),
                     pl.BlockSpec((1, sw, tl), lambda k, b: (k, 0, 0), pipeline_mode=single),
                     pl.BlockSpec((1, LANES), lambda k, b: (0, k))]
        args += [toep_b, vrd_b, d]
        shapes.insert(0, jax.ShapeDtypeStruct((bsz, n, nb * LANES), BF16))
        specs.insert(0, pl.BlockSpec((1, n, LANES), lambda k, b: (b, 0, k)))
        scratch += [pltpu.VMEM((nc, tl), F32), pltpu.VMEM((nc * nlb, LANES), F32),
                    pltpu.VMEM((nc * nlb, LANES), F32), pltpu.VMEM((n, LANES), F32)]
    return pl.pallas_call(
        functools.partial(_ssm_kernel, nc=nc, want_y=want_y),
        out_shape=shapes,
        grid=(nb, bsz),
        in_specs=in_specs,
        out_specs=specs,
        scratch_shapes=scratch,
        compiler_params=_cparams(("arbitrary", "arbitrary")),
        name="ssm",
    )(*args)


def _glu_kernel(z_ref, w_ref, o_ref, *, ws):
    z = z_ref[0]
    a = _dot(z, w_ref[:, :ws])
    b = _dot(z, w_ref[:, ws:])
    o_ref[0] = (a * jax.nn.sigmoid(b)).astype(BF16)


def _glu(z, w_glu, *, tm):
    bsz, s, ws = z.shape
    tm = _tile(s, tm)
    return pl.pallas_call(
        functools.partial(_glu_kernel, ws=ws),
        out_shape=jax.ShapeDtypeStruct((bsz, s, ws), BF16),
        grid=(bsz, s // tm),
        in_specs=[pl.BlockSpec((1, tm, ws), lambda b, i: (b, i, 0)),
                  pl.BlockSpec(w_glu.shape, lambda b, i: (0, 0))],
        out_specs=pl.BlockSpec((1, tm, ws), lambda b, i: (b, i, 0)),
        compiler_params=_cparams(("arbitrary", "arbitrary")),
        name="glu",
    )(z, w_glu)


POOL_TILE = 4 * GRID_W


def _pool_consts(n):
    t = np.arange(POOL_TILE)
    pcs, invs = [], []
    r = np.arange(n) // GRID_W
    c = np.arange(n) % GRID_W
    rows = n // GRID_W
    for w in POOL_WINDOWS:
        lo, hi = w // 2, w - w // 2
        same_row = (t[:, None] // GRID_W) == (t[None, :] // GRID_W)
        dc = (t[None, :] % GRID_W) - (t[:, None] % GRID_W)
        pcs.append((same_row & (dc >= -lo) & (dc < hi)).astype(np.float32))
        cnt_r = np.minimum(r + hi, rows) - np.maximum(r - lo, 0)
        cnt_c = np.minimum(c + hi, GRID_W) - np.maximum(c - lo, 0)
        invs.append((1.0 / (cnt_r * cnt_c)).astype(np.float32)[:, None])
    return jnp.asarray(np.stack(pcs), BF16), jnp.asarray(np.stack(invs), F32)


def _pool_kernel(v_ref, pc_ref, inv_ref, pw_ref, sc_ref, o_ref, cs_ref, *, n, pad):
    wi = pl.program_id(0)
    nt = n // POOL_TILE
    ch = v_ref.shape[2]
    zeros = jnp.zeros((pad, ch), F32)
    cs_ref[pl.ds(0, pad), :] = zeros
    cs_ref[pl.ds(pad + n, pad), :] = zeros
    pc = pc_ref[0]

    def col_body(i, _):
        off = pl.multiple_of(i * POOL_TILE, POOL_TILE)
        v = v_ref[0, pl.ds(off, POOL_TILE), :]
        hi = v.astype(BF16)
        lo = (v - hi.astype(F32)).astype(BF16)
        cs_ref[pl.ds(pad + off, POOL_TILE), :] = _dot(pc, hi) + _dot(pc, lo)
        return 0

    lax.fori_loop(0, nt, col_body, 0)

    for k, w in enumerate(POOL_WINDOWS):
        @pl.when(wi == k)
        def _(w=w):
            def row_body(i, _):
                off = pl.multiple_of(i * POOL_TILE, POOL_TILE)
                acc = cs_ref[pl.ds(pad + off - (w // 2) * GRID_W, POOL_TILE), :]
                for dlt in range(-(w // 2) + 1, w - w // 2):
                    acc = acc + cs_ref[pl.ds(pad + off + dlt * GRID_W, POOL_TILE), :]
                mixed = acc * inv_ref[0, pl.ds(off, POOL_TILE), :] - v_ref[0, pl.ds(off, POOL_TILE), :]
                y = _dot(mixed.astype(BF16), pw_ref[0]) * sc_ref[0]
                o_ref[0, pl.ds(off, POOL_TILE), :] = y.astype(BF16)
                return 0

            lax.fori_loop(0, nt, row_body, 0)


def _pool(usp, col0, pool_w, pool_scale, pc, inv):
    bsz, n, _ = usp.shape
    nw, ch, _ = pool_w.shape
    pad = (max(POOL_WINDOWS) // 2) * GRID_W
    cb0 = col0 // ch
    return pl.pallas_call(
        functools.partial(_pool_kernel, n=n, pad=pad),
        out_shape=jax.ShapeDtypeStruct((bsz, n, nw * ch), BF16),
        grid=(nw, bsz),
        in_specs=[pl.BlockSpec((1, n, ch), lambda w, b: (b, 0, cb0 + w)),
                  pl.BlockSpec((1, POOL_TILE, POOL_TILE), lambda w, b: (w, 0, 0)),
                  pl.BlockSpec((1, n, 1), lambda w, b: (w, 0, 0)),
                  pl.BlockSpec((1, ch, ch), lambda w, b: (w, 0, 0)),
                  pl.BlockSpec((1, 1, ch), lambda w, b: (w, 0, 0))],
        out_specs=pl.BlockSpec((1, n, ch), lambda w, b: (b, 0, w)),
        scratch_shapes=[pltpu.VMEM((n + 2 * pad, ch), F32)],
        compiler_params=_cparams(("arbitrary", "arbitrary")),
        name="pool",
    )(usp, pc, inv, pool_w, pool_scale)


def kernel(x, c, ctx, c_ctx, w_mod, b_mod, norm_g, final_g, ffn1_w_in, ffn1_w_out, ffn2_w_in, ffn2_w_out,
           w_in, ssm_lambda_re, ssm_lambda_im, ssm_log_step, ssm_b_re, ssm_b_im, ssm_c_re, ssm_c_im, ssm_d,
           w_glu, w_branch_a, pool_w, pool_scale, w_branch_b, w_out):
    bsz, seq, d = x.shape
    assert w_mod.shape[0] == 1, "single-layer problem"
    ssm_w = ssm_d.shape[1]
    pool_width = pool_scale.shape[1]
    nw = len(POOL_WINDOWS)
    pch = pool_width // nw
    p = ssm_lambda_re.shape[-1]
    assert seq % POOL_TILE == 0 and ctx.shape[1] % SSM_CHUNK == 0 and ssm_w % LANES == 0

    f1_in, f1_out = ffn1_w_in[0].astype(BF16), ffn1_w_out[0].astype(BF16)
    w_s = w_in[0][:, :ssm_w].astype(BF16)
    late = [ffn2_w_in[0], ffn2_w_out[0], w_in[0], w_out[0]]
    wbb, wba, wglu, pw = (w_branch_b[0].astype(BF16), w_branch_a[0].astype(BF16), w_glu[0].astype(BF16),
                          pool_w[0].astype(BF16))
    psc = pool_scale[0].reshape(nw, 1, pch)
    ng = norm_g[0]
    fg = final_g.reshape(1, d)

    rows = -(-(bsz + 1) // 8) * 8
    c8 = jnp.zeros((rows, d), F32).at[:bsz].set(c).at[bsz].set(c_ctx)
    mods = _mod(c8, w_mod[0], b_mod, tn=_tile(w_mod.shape[2], MOD_COL_TILE)).reshape(rows, N_MOD, d)
    lat = lambda b: b
    con = lambda b: bsz
    ffn_tiles = dict(tm=FFN_ROW_TILE, tf=FFN_HIDDEN_TILE)
    mm_tiles = dict(tm=MM_ROW_TILE, tn=MM_COL_TILE)

    tables = _ssm_block_tables(*_ssm_tables(ssm_lambda_re[0], ssm_lambda_im[0], ssm_log_step[0],
                                            ssm_b_re[0], ssm_b_im[0], ssm_c_re[0], ssm_c_im[0]))

    nctx = ctx.shape[1]
    _, uc = _ffn(ctx.reshape(1, bsz * nctx, d), ng, mods, con, f1_in, f1_out, fg, gi=0, mi=0, next_pre=(1, 3),
                 **ffn_tiles)
    us_c = _mm(uc, w_s, F32, **mm_tiles).reshape(bsz, nctx, ssm_w)
    assert 2 * p == LANES, "fwd | bwd states of one group fill one lane tile"
    h_ctx, = _ssm(us_c, tables, jnp.zeros((bsz, ssm_w // LANES, 2 * SSM_GPB, LANES), F32), ssm_d, want_y=False)

    nsteps = bsz * (seq // _tile(seq, FFN_ROW_TILE)) * (f1_out.shape[0] // FFN_HIDDEN_TILE)
    plans = [_cast_plan(a, nsteps) for a in late]
    x1, u, *cast = _ffn(x, ng, mods, lat, f1_in, f1_out, fg, gi=0, mi=0, next_pre=(1, 3),
                        casts=[(a, pln) for a, pln in zip(late, plans) if pln is not None], **ffn_tiles)
    cast = iter(cast)
    f2_in, f2_out, w_all, wo = [a.astype(BF16) if pln is None else next(cast) for a, pln in zip(late, plans)]
    usp = _mm(u, w_all, F32, ncols=ssm_w + pool_width, **mm_tiles)
    gates = _mm(u, w_all, BF16, col0=ssm_w + pool_width, **mm_tiles)

    z, _ = _ssm(usp, tables, h_ctx, ssm_d, want_y=True)
    ag = _glu(z, wglu, tm=GLU_ROW_TILE)

    pc, inv = _pool_consts(seq)
    yp = _pool(usp, ssm_w, pw, psc, pc, inv)

    x2 = _mixout(ag, yp, wba, wbb, gates, wo, x1, mods, lat, mi=5, tm=MIX_ROW_TILE, tn=MM_COL_TILE)
    out, = _ffn(x2, ng, mods, lat, f2_in, f2_out, fg, gi=2, mi=6, final_norm=True, **ffn_tiles)
    return out
```

```python
import functools
import math

import jax
import jax.numpy as jnp
import numpy as np
from jax import lax
from jax.experimental import pallas as pl
from jax.experimental.pallas import tpu as pltpu

BF16 = jnp.bfloat16
F32 = jnp.float32

RMS_EPS = 1e-6
LAMBDA_RE_MAX = -1e-4
HALF = 0.5
N_MOD = 9
SSM_GROUP = 16
POOL_WINDOWS = (2, 4, 8, 16)
GRID_W = 64
SSM_CHUNK = 16

LANES = 128
BF16_ROWS = 16
V7X_VMEM_BYTES = 64 * 1024 * 1024
VMEM_LIMIT = V7X_VMEM_BYTES - 4 * 1024 * 1024

FFN_ROW_TILE = 1024
FFN_HIDDEN_TILE = 256
MM_ROW_TILE = 2048
MIX_ROW_TILE = 1024
MM_COL_TILE = 512
GLU_ROW_TILE = 1024
MOD_COL_TILE = 512


def _cparams(sem):
    return pltpu.CompilerParams(dimension_semantics=sem, vmem_limit_bytes=VMEM_LIMIT)


def _tile(n, pref):
    t = min(n, pref)
    while n % t:
        t //= 2
    return t


ROW_CHUNK = 64


def _for_row_chunks(n, fn):
    def body(i, carry):
        fn(pl.multiple_of(i * ROW_CHUNK, ROW_CHUNK))
        return carry

    lax.fori_loop(0, n // ROW_CHUNK, body, 0)


NORM_ROWS = 16
NORM_UNROLL = 4


def _for_rows(n, step, fn):
    def body(i, carry):
        fn(pl.multiple_of(i * step, step))
        return carry

    lax.fori_loop(0, n // step, body, 0, unroll=NORM_UNROLL)


def _dot(a, b):
    return jnp.dot(a, b, preferred_element_type=F32)


def _mod_kernel(c_ref, w_ref, b_ref, o_ref):
    c = c_ref[...]
    s = (c * jax.nn.sigmoid(c)).astype(BF16)
    o_ref[...] = _dot(s, w_ref[...].astype(BF16)) + b_ref[...]


def _mod(c8, w_mod, b_mod, tn):
    d, n = w_mod.shape
    return pl.pallas_call(
        _mod_kernel,
        out_shape=jax.ShapeDtypeStruct((c8.shape[0], n), F32),
        grid=(n // tn,),
        in_specs=[pl.BlockSpec(c8.shape, lambda j: (0, 0)),
                  pl.BlockSpec((d, tn), lambda j: (0, j)),
                  pl.BlockSpec((1, tn), lambda j: (0, j))],
        out_specs=pl.BlockSpec((c8.shape[0], tn), lambda j: (0, j)),
        compiler_params=_cparams(("arbitrary",)),
        name="mod",
    )(c8, w_mod, b_mod)


FFN_STAGE_ROWS = 128


def _ffn_kernel(x_ref, g_ref, mod_ref, wg_ref, wu_ref, wo_ref, fg_ref, *refs,
                gi, mi, nf, nstage, final_norm, next_pre, cast_blocks):
    nc = len(cast_blocks)
    src_refs, refs = refs[:nc], refs[nc:]
    if next_pre is None:
        o_ref, refs = refs[0], refs[1:]
    else:
        o_ref, u_ref, refs = refs[0], refs[1], refs[2:]
    dst_refs, (pre_ref, acc_ref, vec_ref, r_ref) = refs[:nc], refs[nc:]
    f = pl.program_id(2)
    rc, d = x_ref.shape[1], x_ref.shape[2]
    nr = NORM_ROWS

    def bcast(v):
        return jnp.broadcast_to(v, (nr, d))

    def inv_rms(y):
        ss = jnp.sum(y * y, axis=-1, keepdims=True)
        return jnp.broadcast_to(lax.rsqrt(ss * (1.0 / d) + RMS_EPS), (nr, LANES))

    def lanes(r):
        return jnp.concatenate([r] * (d // LANES), axis=-1)

    @pl.when(f == 0)
    def _():
        vec_ref[0] = bcast(g_ref[pl.ds(gi, 1), :] * (1.0 + mod_ref[0, pl.ds(mi + 1, 1), :]))
        vec_ref[1] = bcast(mod_ref[0, pl.ds(mi, 1), :])
        vec_ref[2] = bcast(HALF * mod_ref[0, pl.ds(mi + 2, 1), :])
        if final_norm:
            vec_ref[3] = bcast(fg_ref[...])
        if next_pre is not None:
            gi2, mi2 = next_pre
            vec_ref[3] = bcast(g_ref[pl.ds(gi2, 1), :] * (1.0 + mod_ref[0, pl.ds(mi2 + 1, 1), :]))
            vec_ref[4] = bcast(mod_ref[0, pl.ds(mi2, 1), :])

    @pl.when(f < nstage)
    def _():
        base = pl.multiple_of(f * rc, rc)

        def stats(r0):
            r_ref[pl.ds(r0, nr), :] = inv_rms(x_ref[0, pl.ds(r0, nr), :])

        def apply(r0):
            pre = x_ref[0, pl.ds(r0, nr), :] * lanes(r_ref[pl.ds(r0, nr), :]) * vec_ref[0] + vec_ref[1]
            pre_ref[pl.ds(base + r0, nr), :] = pre.astype(BF16)
            acc_ref[pl.ds(base + r0, nr), :] = jnp.zeros((nr, d), F32)

        _for_rows(rc, nr, stats)
        _for_rows(rc, nr, apply)

    @pl.when((f >= nstage) & (f < nstage + nf))
    def _():
        p = pre_ref[...]
        gate = _dot(p, wg_ref[...])
        up = _dot(p, wu_ref[...])
        act = (gate * jax.nn.sigmoid(gate) * up).astype(BF16)
        acc_ref[...] += _dot(act, wo_ref[...])
        step = (pl.program_id(0) * pl.num_programs(1) + pl.program_id(1)) * nf + (f - nstage)
        for src, dst, nblk in zip(src_refs, dst_refs, cast_blocks):
            @pl.when(step < nblk)
            def _(src=src, dst=dst):
                dst[...] = src[...].astype(BF16)

    @pl.when(f >= nstage + nf)
    def _():
        base = pl.multiple_of((f - (nstage + nf)) * rc, rc)

        renorm = final_norm or next_pre is not None

        def residual(r0):
            r = pl.ds(r0, nr)
            y = x_ref[0, r, :] + vec_ref[2] * acc_ref[pl.ds(base + r0, nr), :]
            o_ref[0, r, :] = y
            if renorm:
                r_ref[r, :] = inv_rms(y)

        def norm(r0):
            r = pl.ds(r0, nr)
            yn = o_ref[0, r, :] * lanes(r_ref[r, :])
            if final_norm:
                o_ref[0, r, :] = yn * vec_ref[3]
            if next_pre is not None:
                u_ref[0, r, :] = (yn * vec_ref[3] + vec_ref[4]).astype(BF16)

        _for_rows(rc, nr, residual)
        if renorm:
            _for_rows(rc, nr, norm)


def _cast_plan(arr, nsteps):
    r, c = arr.shape
    for parts in (8, 4, 2, 1):
        if c % (parts * LANES):
            continue
        br = BF16_ROWS
        while br <= r:
            if r % br == 0 and (r // br) * parts <= nsteps:
                return br, c // parts
            br *= 2
    return None


def _ffn(x, norm_g, mods, mod_row, w_in, w_out, final_g, *, gi, mi, tm, tf, final_norm=False, next_pre=None,
         casts=()):
    bsz, s, d = x.shape
    ff = w_out.shape[0]
    nf = ff // tf
    tm = _tile(s, tm)
    rc = _tile(tm, FFN_STAGE_ROWS)
    nstage = tm // rc
    ni = s // tm

    def hid(f):
        return jnp.clip(f - nstage, 0, nf - 1)

    cast_blocks, cast_in, cast_out, cast_shapes = [], [], [], []
    for arr, (br, bc) in casts:
        nrb, ncb = arr.shape[0] // br, arr.shape[1] // bc
        assert nrb * br == arr.shape[0] and ncb * bc == arr.shape[1] and nrb * ncb <= bsz * ni * nf

        def blk(b, i, f, nrb=nrb, ncb=ncb):
            step = jnp.minimum((b * ni + i) * nf + hid(f), nrb * ncb - 1)
            return step // ncb, step % ncb

        cast_blocks.append(nrb * ncb)
        cast_in.append(pl.BlockSpec((br, bc), blk))
        cast_out.append(pl.BlockSpec((br, bc), blk))
        cast_shapes.append(jax.ShapeDtypeStruct(arr.shape, BF16))
    kern = functools.partial(_ffn_kernel, gi=gi, mi=mi, nf=nf, nstage=nstage, final_norm=final_norm,
                             next_pre=next_pre, cast_blocks=tuple(cast_blocks))

    def x_blk(b, i, f):
        return b, i * nstage + jnp.where(f < nstage, f, jnp.clip(f - (nstage + nf), 0, nstage - 1)), 0

    def o_blk(b, i, f):
        return b, i * nstage + jnp.clip(f - (nstage + nf), 0, nstage - 1), 0

    out_shape = [jax.ShapeDtypeStruct((bsz, s, d), F32)]
    out_specs = [pl.BlockSpec((1, rc, d), o_blk)]
    if next_pre is not None:
        out_shape.append(jax.ShapeDtypeStruct((bsz, s, d), BF16))
        out_specs.append(pl.BlockSpec((1, rc, d), o_blk))
    return pl.pallas_call(
        kern,
        out_shape=out_shape + cast_shapes,
        grid=(bsz, ni, nf + 2 * nstage),
        in_specs=[pl.BlockSpec((1, rc, d), x_blk),
                  pl.BlockSpec(norm_g.shape, lambda b, i, f: (0, 0)),
                  pl.BlockSpec((1, N_MOD, d), lambda b, i, f: (mod_row(b), 0, 0)),
                  pl.BlockSpec((d, tf), lambda b, i, f: (0, hid(f))),
                  pl.BlockSpec((d, tf), lambda b, i, f: (0, nf + hid(f))),
                  pl.BlockSpec((tf, d), lambda b, i, f: (hid(f), 0)),
                  pl.BlockSpec((1, d), lambda b, i, f: (0, 0))] + cast_in,
        out_specs=out_specs + cast_out,
        scratch_shapes=[pltpu.VMEM((tm, d), BF16), pltpu.VMEM((tm, d), F32),
                        pltpu.VMEM((5, NORM_ROWS, d), F32), pltpu.VMEM((rc, LANES), F32)],
        compiler_params=_cparams(("arbitrary", "arbitrary", "arbitrary")),
        name="ffn",
    )(x, norm_g, mods, w_in, w_in, w_out, final_g, *[arr for arr, _ in casts])


def _mm_kernel(a_ref, b_ref, o_ref):
    o_ref[0] = _dot(a_ref[0], b_ref[...]).astype(o_ref.dtype)


def _mm(a, w, out_dtype, *, tm, tn, col0=0, ncols=None):
    bsz, s, k = a.shape
    n = w.shape[1] - col0 if ncols is None else ncols
    tm, tn = _tile(s, tm), _tile(math.gcd(n, col0) if col0 else n, tn)
    j0 = col0 // tn
    return pl.pallas_call(
        _mm_kernel,
        out_shape=jax.ShapeDtypeStruct((bsz, s, n), out_dtype),
        grid=(bsz, s // tm, n // tn),
        in_specs=[pl.BlockSpec((1, tm, k), lambda b, i, j: (b, i, 0)),
                  pl.BlockSpec((k, tn), lambda b, i, j: (0, j0 + j))],
        out_specs=pl.BlockSpec((1, tm, tn), lambda b, i, j: (b, i, j)),
        compiler_params=_cparams(("arbitrary", "arbitrary", "arbitrary")),
        name="mm",
    )(a, w)


def _mixout_kernel(a_ref, p_ref, wa_ref, wb_ref, ga_ref, gb_ref, wo_ref, x_ref, mod_ref, o_ref, m_ref, *, mi, nj):
    j = pl.program_id(2)
    tn = wa_ref.shape[1]

    @pl.when(j < nj)
    def _():
        ya = _dot(a_ref[0], wa_ref[...])
        yb = _dot(p_ref[0], wb_ref[...])
        m = jax.nn.sigmoid(ga_ref[0].astype(F32)) * ya + jax.nn.sigmoid(gb_ref[0].astype(F32)) * yb
        m_ref[:, pl.ds(pl.multiple_of(j * tn, tn), tn)] = m.astype(BF16)

    @pl.when(j >= nj)
    def _():
        o_ref[0] = x_ref[0] + mod_ref[0, pl.ds(mi, 1), :] * _dot(m_ref[...], wo_ref[...])


def _mixout(ag, yp, wa, wb, gates, wo, x, mods, mod_row, *, mi, tm, tn):
    bsz, s, ka = ag.shape
    kb = yp.shape[2]
    d = wa.shape[1]
    tm, tn = _tile(s, tm), _tile(d, tn)
    nj = d // tn

    def first(j):
        return jnp.minimum(j, nj - 1)

    def second(j):
        return jnp.maximum(j - nj, 0)

    return pl.pallas_call(
        functools.partial(_mixout_kernel, mi=mi, nj=nj),
        out_shape=jax.ShapeDtypeStruct((bsz, s, d), F32),
        grid=(bsz, s // tm, 2 * nj),
        in_specs=[pl.BlockSpec((1, tm, ka), lambda b, i, j: (b, i, 0)),
                  pl.BlockSpec((1, tm, kb), lambda b, i, j: (b, i, 0)),
                  pl.BlockSpec((ka, tn), lambda b, i, j: (0, first(j))),
                  pl.BlockSpec((kb, tn), lambda b, i, j: (0, first(j))),
                  pl.BlockSpec((1, tm, tn), lambda b, i, j: (b, i, first(j))),
                  pl.BlockSpec((1, tm, tn), lambda b, i, j: (b, i, nj + first(j))),
                  pl.BlockSpec((d, tn), lambda b, i, j: (0, second(j))),
                  pl.BlockSpec((1, tm, tn), lambda b, i, j: (b, i, second(j))),
                  pl.BlockSpec((1, N_MOD, tn), lambda b, i, j: (mod_row(b), 0, second(j)))],
        out_specs=pl.BlockSpec((1, tm, tn), lambda b, i, j: (b, i, second(j))),
        scratch_shapes=[pltpu.VMEM((tm, d), BF16)],
        compiler_params=_cparams(("arbitrary", "arbitrary", "arbitrary")),
        name="mixout",
    )(ag, yp, wa, wb, gates, gates, wo, x, mods)


def _ssm_tables(lam_re, lam_im, log_step, b_re, b_im, c_re, c_im):
    t = SSM_CHUNK
    hp = lax.Precision.HIGHEST
    lr = jnp.minimum(lam_re.astype(F32), LAMBDA_RE_MAX)
    li = lam_im.astype(F32)
    step = jnp.exp(log_step.astype(F32))[..., None]
    m = jnp.arange(t + 1, dtype=F32)[:, None, None, None]
    mag = jnp.exp(m * (lr * step)[None])
    ang = m * (li * step)[None]
    pw_re, pw_im = mag * jnp.cos(ang), mag * jnp.sin(ang)
    nr, ni = pw_re[1] - 1.0, pw_im[1]
    den = lr * lr + li * li
    q_re, q_im = (nr * lr + ni * li) / den, (ni * lr - nr * li) / den
    bb_re = q_re[..., None] * b_re - q_im[..., None] * b_im
    bb_im = q_re[..., None] * b_im + q_im[..., None] * b_re
    cr, ci = c_re.astype(F32), c_im.astype(F32)

    cl_re = cr[:, :, None] * pw_re[:t].transpose(1, 2, 0, 3)[:, :, :, None, :] \
        - ci[:, :, None] * pw_im[:t].transpose(1, 2, 0, 3)[:, :, :, None, :]
    cl_im = cr[:, :, None] * pw_im[:t].transpose(1, 2, 0, 3)[:, :, :, None, :] \
        + ci[:, :, None] * pw_re[:t].transpose(1, 2, 0, 3)[:, :, :, None, :]
    kern = jnp.einsum('dgtkp,dgpq->dgtkq', cl_re, bb_re, precision=hp) \
        - jnp.einsum('dgtkp,dgpq->dgtkq', cl_im, bb_im, precision=hp)
    g, kk = kern.shape[1], kern.shape[3]
    zero = (jnp.arange(t) == 0).astype(F32)[None, :, None, None]
    kpos = kern[0] + zero * kern[1]
    kneg = kern[1] + zero * kern[0]
    rpos = kpos.transpose(0, 3, 1, 2).reshape(g, kk, t * kk)
    rneg = kneg.transpose(0, 3, 1, 2).reshape(g, kk, t * kk)

    pf_re, pf_im = pw_re[:t][::-1, 0], pw_im[:t][::-1, 0]
    pb_re, pb_im = pw_re[:t, 1], pw_im[:t, 1]

    def st(p_re, p_im, d):
        re = p_re[:, :, :, None] * bb_re[d][None] - p_im[:, :, :, None] * bb_im[d][None]
        im = p_re[:, :, :, None] * bb_im[d][None] + p_im[:, :, :, None] * bb_re[d][None]
        return re.transpose(1, 0, 3, 2), im.transpose(1, 0, 3, 2)

    wf_re, wf_im = st(pf_re, pf_im, 0)
    wb_re, wb_im = st(pb_re, pb_im, 1)
    p = lr.shape[-1]
    wst = jnp.concatenate([wf_re, wb_re, wf_im, wb_im], axis=-1).reshape(g, t * kk, 4 * p)

    vf_pw_re, vf_pw_im = pw_re[1:, 0], pw_im[1:, 0]
    vb_pw_re, vb_pw_im = pw_re[1:, 1][::-1], pw_im[1:, 1][::-1]

    def rd(p_re, p_im, d):
        d_re = cr[d][None] * p_re[:, :, None, :] - ci[d][None] * p_im[:, :, None, :]
        d_im = cr[d][None] * p_im[:, :, None, :] + ci[d][None] * p_re[:, :, None, :]
        return d_re.transpose(1, 3, 0, 2), -d_im.transpose(1, 3, 0, 2)

    vf_re, vf_im = rd(vf_pw_re, vf_pw_im, 0)
    vb_re, vb_im = rd(vb_pw_re, vb_pw_im, 1)
    vrd = jnp.concatenate([vf_re, vb_re, vf_im, vb_im], axis=1).reshape(g, 4 * p, t * kk)

    a_re = jnp.concatenate([pw_re[t, 0], pw_re[t, 1]], axis=-1)[:, None, :]
    a_im = jnp.concatenate([pw_im[t, 0], pw_im[t, 1]], axis=-1)[:, None, :]
    return rpos.astype(BF16), rneg.astype(BF16), wst.astype(BF16), vrd.astype(BF16), a_re, a_im


SSM_GPB = LANES // SSM_GROUP


def _ssm_block_tables(rpos, rneg, wst, vrd, a_re, a_im):
    rpos, rneg, wst, vrd = lax.optimization_barrier((rpos, rneg, wst, vrd))
    g, _, tk = rpos.shape
    t, k, gpb = SSM_CHUNK, SSM_GROUP, SSM_GPB
    nb = g // gpb
    p2 = wst.shape[2] // 2
    tl, hw = t * gpb * k, gpb * p2
    rpos = rpos.reshape(nb, gpb * k, tk)
    rneg = rneg.reshape(nb, gpb * k, tk)
    rw = wst.reshape(nb, gpb, t, k, 2 * p2).transpose(0, 2, 1, 3, 4).reshape(nb, t, gpb * k, 2 * p2)
    rv = vrd.reshape(nb, gpb, 2, p2, tk).transpose(0, 2, 1, 3, 4).reshape(nb, 2, hw, tk)
    jj, kk = np.arange(tk) // k, np.arange(tk) % k
    sel = (jj[None, :, None] == np.arange(t)[:, None, None]) & (kk[None, :, None] == (np.arange(LANES) % k)[None, None, :])
    sel = jnp.asarray(sel, BF16)
    toep_b, wst_b, vrd_b = pl.pallas_call(
        functools.partial(_ssm_expand_kernel, t=t, k=k, p2=p2),
        out_shape=[jax.ShapeDtypeStruct((nb, tl, tl), BF16), jax.ShapeDtypeStruct((nb, tl, 2 * hw), BF16),
                   jax.ShapeDtypeStruct((nb, 2 * hw, tl), BF16)],
        grid=(nb, t),
        in_specs=[pl.BlockSpec((1, gpb * k, tk), lambda b, i: (b, 0, 0)),
                  pl.BlockSpec((1, gpb * k, tk), lambda b, i: (b, 0, 0)),
                  pl.BlockSpec((1, 1, gpb * k, 2 * p2), lambda b, i: (b, i, 0, 0)),
                  pl.BlockSpec((1, 2, hw, tk), lambda b, i: (b, 0, 0, 0)),
                  pl.BlockSpec(sel.shape, lambda b, i: (0, 0, 0))],
        out_specs=[pl.BlockSpec((1, gpb * k, tl), lambda b, i: (b, i, 0)),
                   pl.BlockSpec((1, gpb * k, 2 * hw), lambda b, i: (b, i, 0)),
                   pl.BlockSpec((1, 2 * hw, gpb * k), lambda b, i: (b, 0, i))],
        scratch_shapes=[pltpu.VMEM((2 * t - 1, gpb * k, gpb * k), BF16)],
        compiler_params=_cparams(("arbitrary", "arbitrary")),
        name="ssm_expand",
    )(rpos, rneg, rw, rv, sel)
    return toep_b, wst_b, vrd_b, a_re.reshape(nb, 1, hw), a_im.reshape(nb, 1, hw)


def _ssm_expand_kernel(rpos_ref, rneg_ref, rw_ref, rv_ref, sel_ref, toep_ref, wst_ref, vrd_ref, bd_ref, *, t, k, p2):
    i = pl.program_id(1)
    lb = rpos_ref.shape[1]
    hw = rv_ref.shape[2]
    gpb = lb // k
    @pl.when(i == 0)
    def _():
        same_g = (lax.broadcasted_iota(jnp.int32, (lb, lb), 0) // k) == (lax.broadcasted_iota(jnp.int32, (lb, lb), 1) // k)
        for m in range(t):
            bd_ref[t - 1 + m] = jnp.where(same_g, _dot(rpos_ref[0], sel_ref[m]), 0.0).astype(BF16)
            if m:
                bd_ref[t - 1 - m] = jnp.where(same_g, _dot(rneg_ref[0], sel_ref[m]), 0.0).astype(BF16)

    for j in range(t):
        toep_ref[0, :, j * lb:(j + 1) * lb] = bd_ref[j - i + (t - 1)]
    w = rw_ref[0, 0]
    same_gw = (lax.broadcasted_iota(jnp.int32, (lb, hw), 0) // k) == (lax.broadcasted_iota(jnp.int32, (lb, hw), 1) // p2)
    for r in range(2):
        wr = jnp.concatenate([w[:, r * p2:(r + 1) * p2]] * gpb, axis=-1)
        wst_ref[0, :, r * hw:(r + 1) * hw] = jnp.where(same_gw, wr, jnp.zeros_like(wr))
    same_gv = (lax.broadcasted_iota(jnp.int32, (hw, lb), 0) // p2) == (lax.broadcasted_iota(jnp.int32, (hw, lb), 1) // k)
    for r in range(2):
        v = _dot(rv_ref[0, r], sel_ref[i])
        vrd_ref[0, r * hw:(r + 1) * hw, :] = jnp.where(same_gv, v, 0.0).astype(BF16)


def _ssm_kernel(u_ref, ws_ref, ar_ref, ai_ref, h0_ref, *refs, nc, want_y):
    t = SSM_CHUNK
    if want_y:
        wt_ref, v_ref, d_ref, z_ref, hfin_ref, s_ref, yi_ref, hp_ref, tmp_ref, y_ref = refs
    else:
        hfin_ref, s_ref = refs
    a = jnp.concatenate([u_ref[0, pl.ds(j, nc, stride=t), :] for j in range(t)], axis=-1).astype(BF16)
    s = _dot(a, ws_ref[0])
    nlb = s.shape[1] // LANES
    gpb = nlb // 2
    for k in range(nlb):
        s_ref[pl.ds(k, nc, stride=nlb), :] = s[:, k * LANES:(k + 1) * LANES]
    if want_y:
        yi_ref[...] = _dot(a, wt_ref[0])

    ar = ar_ref[0]
    ai = ai_ref[0]
    fwd = lax.broadcasted_iota(jnp.int32, (gpb, LANES), 1) < (LANES // 2)

    def body(c, carry):
        h_re, h_im = carry
        rf = pl.multiple_of(c * nlb, nlb)
        rb = pl.multiple_of((nc - 1 - c) * nlb, nlb)
        if want_y:
            hp_ref[pl.ds(rf, gpb), :] = h_re
            hp_ref[pl.ds(rf + gpb, gpb), :] = h_im
            tmp_ref[pl.ds(rb, gpb), :] = h_re
            tmp_ref[pl.ds(rb + gpb, gpb), :] = h_im
        s_re = jnp.where(fwd, s_ref[pl.ds(rf, gpb), :], s_ref[pl.ds(rb, gpb), :])
        s_im = jnp.where(fwd, s_ref[pl.ds(rf + gpb, gpb), :], s_ref[pl.ds(rb + gpb, gpb), :])
        return ar * h_re - ai * h_im + s_re, ar * h_im + ai * h_re + s_im

    h0 = h0_ref[0, 0]
    h_re, h_im = lax.fori_loop(0, nc, body, (h0[:gpb], h0[gpb:]))
    hfin_ref[0, 0] = jnp.concatenate([h_re, h_im], axis=0)

    if want_y:
        fwd2 = lax.broadcasted_iota(jnp.int32, hp_ref.shape, 1) < (LANES // 2)
        hp_ref[...] = jnp.where(fwd2, hp_ref[...], tmp_ref[...])
        hp = jnp.concatenate([hp_ref[pl.ds(k, nc, stride=nlb), :] for k in range(nlb)], axis=-1).astype(BF16)
        y = yi_ref[...] + _dot(hp, v_ref[0])
        for j in range(t):
            y_ref[pl.ds(j, nc, stride=t), :] = y[:, j * LANES:(j + 1) * LANES]

        def rows(r0):
            r = pl.ds(r0, ROW_CHUNK)
            z_ref[0, r, :] = jax.nn.gelu(y_ref[r, :] + d_ref[...] * u_ref[0, r, :]).astype(BF16)

        _for_row_chunks(nc * t, rows)


def _ssm(u, tables, h0, d, *, want_y):
    toep_b, wst_b, vrd_b, ar, ai = tables
    bsz, n, _ = u.shape
    nb, tl, sw = wst_b.shape
    nc = n // SSM_CHUNK
    single = pl.Buffered(1)
    nlb = sw // LANES
    gpb = nlb // 2
    shapes = [jax.ShapeDtypeStruct((bsz, nb, nlb, LANES), F32)]
    specs = [pl.BlockSpec((1, 1, nlb, LANES), lambda k, b: (b, k, 0, 0))]
    scratch = [pltpu.VMEM((nc * nlb, LANES), F32)]
    in_specs = [pl.BlockSpec((1, n, LANES), lambda k, b: (b, 0, k)),
                pl.BlockSpec((1, tl, sw), lambda k, b: (k, 0, 0)),
                pl.BlockSpec((1, gpb, LANES), lambda k, b: (k, 0, 0)),
                pl.BlockSpec((1, gpb, LANES), lambda k, b: (k, 0, 0)),
                pl.BlockSpec((1, 1, nlb, LANES), lambda k, b: (b, k, 0, 0))]
    args = [u, wst_b, ar.reshape(nb, gpb, LANES), ai.reshape(nb, gpb, LANES), h0]
    if want_y:
        in_specs += [pl.BlockSpec((1, tl, tl), lambda k, b: (k, 0, 0)),
                     pl.BlockSpec((1, sw, tl), lambda k, b: (k, 0, 0), pipeline_mode=single),
                     pl.BlockSpec((1, LANES), lambda k, b: (0, k))]
        args += [toep_b, vrd_b, d]
        shapes.insert(0, jax.ShapeDtypeStruct((bsz, n, nb * LANES), BF16))
        specs.insert(0, pl.BlockSpec((1, n, LANES), lambda k, b: (b, 0, k)))
        scratch += [pltpu.VMEM((nc, tl), F32), pltpu.VMEM((nc * nlb, LANES), F32),
                    pltpu.VMEM((nc * nlb, LANES), F32), pltpu.VMEM((n, LANES), F32)]
    return pl.pallas_call(
        functools.partial(_ssm_kernel, nc=nc, want_y=want_y),
        out_shape=shapes,
        grid=(nb, bsz),
        in_specs=in_specs,
        out_specs=specs,
        scratch_shapes=scratch,
        compiler_params=_cparams(("arbitrary", "arbitrary")),
        name="ssm",
    )(*args)


def _glu_kernel(z_ref, w_ref, o_ref, *, ws):
    z = z_ref[0]
    a = _dot(z, w_ref[:, :ws])
    b = _dot(z, w_ref[:, ws:])
    o_ref[0] = (a * jax.nn.sigmoid(b)).astype(BF16)


def _glu(z, w_glu, *, tm):
    bsz, s, ws = z.shape
    tm = _tile(s, tm)
    return pl.pallas_call(
        functools.partial(_glu_kernel, ws=ws),
        out_shape=jax.ShapeDtypeStruct((bsz, s, ws), BF16),
        grid=(bsz, s // tm),
        in_specs=[pl.BlockSpec((1, tm, ws), lambda b, i: (b, i, 0)),
                  pl.BlockSpec(w_glu.shape, lambda b, i: (0, 0))],
        out_specs=pl.BlockSpec((1, tm, ws), lambda b, i: (b, i, 0)),
        compiler_params=_cparams(("arbitrary", "arbitrary")),
        name="glu",
    )(z, w_glu)


POOL_TILE = 4 * GRID_W
POOL_UNROLL = 4


def _pool_consts(n):
    t = np.arange(POOL_TILE)
    pcs, invs = [], []
    r = np.arange(n) // GRID_W
    c = np.arange(n) % GRID_W
    rows = n // GRID_W
    for w in POOL_WINDOWS:
        lo, hi = w // 2, w - w // 2
        same_row = (t[:, None] // GRID_W) == (t[None, :] // GRID_W)
        dc = (t[None, :] % GRID_W) - (t[:, None] % GRID_W)
        pcs.append((same_row & (dc >= -lo) & (dc < hi)).astype(np.float32))
        cnt_r = np.minimum(r + hi, rows) - np.maximum(r - lo, 0)
        cnt_c = np.minimum(c + hi, GRID_W) - np.maximum(c - lo, 0)
        invs.append((1.0 / (cnt_r * cnt_c)).astype(np.float32)[:, None])
    return jnp.asarray(np.stack(pcs), BF16), jnp.asarray(np.stack(invs), F32)


def _pool_kernel(v_ref, pc_ref, inv_ref, pw_ref, sc_ref, o_ref, cs_ref, *, n, pad):
    wi = pl.program_id(0)
    nt = n // POOL_TILE
    ch = v_ref.shape[2]
    zeros = jnp.zeros((pad, ch), F32)
    cs_ref[pl.ds(0, pad), :] = zeros
    cs_ref[pl.ds(pad + n, pad), :] = zeros
    pc = pc_ref[0]

    def col_body(i, _):
        off = pl.multiple_of(i * POOL_TILE, POOL_TILE)
        v = v_ref[0, pl.ds(off, POOL_TILE), :]
        hi = v.astype(BF16)
        lo = (v - hi.astype(F32)).astype(BF16)
        cs_ref[pl.ds(pad + off, POOL_TILE), :] = _dot(pc, hi) + _dot(pc, lo)
        return 0

    lax.fori_loop(0, nt, col_body, 0, unroll=POOL_UNROLL)

    for k, w in enumerate(POOL_WINDOWS):
        @pl.when(wi == k)
        def _(w=w):
            def row_body(i, _):
                off = pl.multiple_of(i * POOL_TILE, POOL_TILE)
                acc = cs_ref[pl.ds(pad + off - (w // 2) * GRID_W, POOL_TILE), :]
                for dlt in range(-(w // 2) + 1, w - w // 2):
                    acc = acc + cs_ref[pl.ds(pad + off + dlt * GRID_W, POOL_TILE), :]
                mixed = acc * inv_ref[0, pl.ds(off, POOL_TILE), :] - v_ref[0, pl.ds(off, POOL_TILE), :]
                y = _dot(mixed.astype(BF16), pw_ref[0]) * sc_ref[0]
                o_ref[0, pl.ds(off, POOL_TILE), :] = y.astype(BF16)
                return 0

            lax.fori_loop(0, nt, row_body, 0, unroll=POOL_UNROLL)


def _pool(usp, col0, pool_w, pool_scale, pc, inv):
    bsz, n, _ = usp.shape
    nw, ch, _ = pool_w.shape
    pad = (max(POOL_WINDOWS) // 2) * GRID_W
    cb0 = col0 // ch
    return pl.pallas_call(
        functools.partial(_pool_kernel, n=n, pad=pad),
        out_shape=jax.ShapeDtypeStruct((bsz, n, nw * ch), BF16),
        grid=(nw, bsz),
        in_specs=[pl.BlockSpec((1, n, ch), lambda w, b: (b, 0, cb0 + w)),
                  pl.BlockSpec((1, POOL_TILE, POOL_TILE), lambda w, b: (w, 0, 0)),
                  pl.BlockSpec((1, n, 1), lambda w, b: (w, 0, 0)),
                  pl.BlockSpec((1, ch, ch), lambda w, b: (w, 0, 0)),
                  pl.BlockSpec((1, 1, ch), lambda w, b: (w, 0, 0))],
        out_specs=pl.BlockSpec((1, n, ch), lambda w, b: (b, 0, w)),
        scratch_shapes=[pltpu.VMEM((n + 2 * pad, ch), F32)],
        compiler_params=_cparams(("arbitrary", "arbitrary")),
        name="pool",
    )(usp, pc, inv, pool_w, pool_scale)


def kernel(x, c, ctx, c_ctx, w_mod, b_mod, norm_g, final_g, ffn1_w_in, ffn1_w_out, ffn2_w_in, ffn2_w_out,
           w_in, ssm_lambda_re, ssm_lambda_im, ssm_log_step, ssm_b_re, ssm_b_im, ssm_c_re, ssm_c_im, ssm_d,
           w_glu, w_branch_a, pool_w, pool_scale, w_branch_b, w_out):
    bsz, seq, d = x.shape
    assert w_mod.shape[0] == 1, "single-layer problem"
    ssm_w = ssm_d.shape[1]
    pool_width = pool_scale.shape[1]
    nw = len(POOL_WINDOWS)
    pch = pool_width // nw
    p = ssm_lambda_re.shape[-1]
    assert seq % POOL_TILE == 0 and ctx.shape[1] % SSM_CHUNK == 0 and ssm_w % LANES == 0

    f1_in, f1_out = ffn1_w_in[0].astype(BF16), ffn1_w_out[0].astype(BF16)
    w_s = w_in[0][:, :ssm_w].astype(BF16)
    late = [ffn2_w_in[0], ffn2_w_out[0], w_in[0], w_out[0]]
    wbb, wba, wglu, pw = (w_branch_b[0].astype(BF16), w_branch_a[0].astype(BF16), w_glu[0].astype(BF16),
                          pool_w[0].astype(BF16))
    psc = pool_scale[0].reshape(nw, 1, pch)
    ng = norm_g[0]
    fg = final_g.reshape(1, d)

    rows = -(-(bsz + 1) // 8) * 8
    c8 = jnp.zeros((rows, d), F32).at[:bsz].set(c).at[bsz].set(c_ctx)
    mods = _mod(c8, w_mod[0], b_mod, tn=_tile(w_mod.shape[2], MOD_COL_TILE)).reshape(rows, N_MOD, d)
    lat = lambda b: b
    con = lambda b: bsz
    ffn_tiles = dict(tm=FFN_ROW_TILE, tf=FFN_HIDDEN_TILE)
    mm_tiles = dict(tm=MM_ROW_TILE, tn=MM_COL_TILE)

    tables = _ssm_block_tables(*_ssm_tables(ssm_lambda_re[0], ssm_lambda_im[0], ssm_log_step[0],
                                            ssm_b_re[0], ssm_b_im[0], ssm_c_re[0], ssm_c_im[0]))

    nctx = ctx.shape[1]
    _, uc = _ffn(ctx.reshape(1, bsz * nctx, d), ng, mods, con, f1_in, f1_out, fg, gi=0, mi=0, next_pre=(1, 3),
                 **ffn_tiles)
    us_c = _mm(uc, w_s, F32, **mm_tiles).reshape(bsz, nctx, ssm_w)
    assert 2 * p == LANES, "fwd | bwd states of one group fill one lane tile"
    h_ctx, = _ssm(us_c, tables, jnp.zeros((bsz, ssm_w // LANES, 2 * SSM_GPB, LANES), F32), ssm_d, want_y=False)

    nsteps = bsz * (seq // _tile(seq, FFN_ROW_TILE)) * (f1_out.shape[0] // FFN_HIDDEN_TILE)
    plans = [_cast_plan(a, nsteps) for a in late]
    x1, u, *cast = _ffn(x, ng, mods, lat, f1_in, f1_out, fg, gi=0, mi=0, next_pre=(1, 3),
                        casts=[(a, pln) for a, pln in zip(late, plans) if pln is not None], **ffn_tiles)
    cast = iter(cast)
    f2_in, f2_out, w_all, wo = [a.astype(BF16) if pln is None else next(cast) for a, pln in zip(late, plans)]
    usp = _mm(u, w_all, F32, ncols=ssm_w + pool_width, **mm_tiles)
    gates = _mm(u, w_all, BF16, col0=ssm_w + pool_width, **mm_tiles)

    z, _ = _ssm(usp, tables, h_ctx, ssm_d, want_y=True)
    ag = _glu(z, wglu, tm=GLU_ROW_TILE)

    pc, inv = _pool_consts(seq)
    yp = _pool(usp, ssm_w, pw, psc, pc, inv)

    x2 = _mixout(ag, yp, wba, wbb, gates, wo, x1, mods, lat, mi=5, tm=MIX_ROW_TILE, tn=MM_COL_TILE)
    out, = _ffn(x2, ng, mods, lat, f2_in, f2_out, fg, gi=2, mi=6, final_norm=True, **ffn_tiles)
    return out
```

```python
import functools
import math

import jax
import jax.numpy as jnp
import numpy as np
from jax import lax
from jax.experimental import pallas as pl
from jax.experimental.pallas import tpu as pltpu

BF16 = jnp.bfloat16
F32 = jnp.float32

RMS_EPS = 1e-6
LAMBDA_RE_MAX = -1e-4
HALF = 0.5
N_MOD = 9
SSM_GROUP = 16
POOL_WINDOWS = (2, 4, 8, 16)
GRID_W = 64
SSM_CHUNK = 16

LANES = 128
BF16_ROWS = 16
V7X_VMEM_BYTES = 64 * 1024 * 1024
VMEM_LIMIT = V7X_VMEM_BYTES - 4 * 1024 * 1024

FFN_ROW_TILE = 1024
FFN_HIDDEN_TILE = 256
MM_ROW_TILE = 2048
MIX_ROW_TILE = 1024
MM_COL_TILE = 512
GLU_ROW_TILE = 1024
MOD_COL_TILE = 512


def _cparams(sem):
    return pltpu.CompilerParams(dimension_semantics=sem, vmem_limit_bytes=VMEM_LIMIT)


def _tile(n, pref):
    t = min(n, pref)
    while n % t:
        t //= 2
    return t


ROW_CHUNK = 64


def _for_row_chunks(n, fn):
    def body(i, carry):
        fn(pl.multiple_of(i * ROW_CHUNK, ROW_CHUNK))
        return carry

    lax.fori_loop(0, n // ROW_CHUNK, body, 0)


NORM_ROWS = 16
NORM_UNROLL = 4


def _for_rows(n, step, fn):
    def body(i, carry):
        fn(pl.multiple_of(i * step, step))
        return carry

    lax.fori_loop(0, n // step, body, 0, unroll=NORM_UNROLL)


def _dot(a, b):
    return jnp.dot(a, b, preferred_element_type=F32)


def _mod_kernel(c_ref, w_ref, b_ref, o_ref):
    c = c_ref[...]
    s = (c * jax.nn.sigmoid(c)).astype(BF16)
    o_ref[...] = _dot(s, w_ref[...].astype(BF16)) + b_ref[...]


def _mod(c8, w_mod, b_mod, tn):
    d, n = w_mod.shape
    return pl.pallas_call(
        _mod_kernel,
        out_shape=jax.ShapeDtypeStruct((c8.shape[0], n), F32),
        grid=(n // tn,),
        in_specs=[pl.BlockSpec(c8.shape, lambda j: (0, 0)),
                  pl.BlockSpec((d, tn), lambda j: (0, j)),
                  pl.BlockSpec((1, tn), lambda j: (0, j))],
        out_specs=pl.BlockSpec((c8.shape[0], tn), lambda j: (0, j)),
        compiler_params=_cparams(("arbitrary",)),
        name="mod",
    )(c8, w_mod, b_mod)


FFN_STAGE_ROWS = 128


def _ffn_kernel(x_ref, g_ref, mod_ref, wg_ref, wu_ref, wo_ref, fg_ref, *refs,
                gi, mi, nf, nstage, final_norm, next_pre, cast_blocks):
    nc = len(cast_blocks)
    src_refs, refs = refs[:nc], refs[nc:]
    if next_pre is None:
        o_ref, refs = refs[0], refs[1:]
    else:
        o_ref, u_ref, refs = refs[0], refs[1], refs[2:]
    dst_refs, (pre_ref, acc_ref, vec_ref, r_ref) = refs[:nc], refs[nc:]
    f = pl.program_id(2)
    rc, d = x_ref.shape[1], x_ref.shape[2]
    nr = NORM_ROWS

    def bcast(v):
        return jnp.broadcast_to(v, (nr, d))

    def inv_rms(y):
        ss = jnp.sum(y * y, axis=-1, keepdims=True)
        return jnp.broadcast_to(lax.rsqrt(ss * (1.0 / d) + RMS_EPS), (nr, LANES))

    def lanes(r):
        return jnp.concatenate([r] * (d // LANES), axis=-1)

    @pl.when(f == 0)
    def _():
        vec_ref[0] = bcast(g_ref[pl.ds(gi, 1), :] * (1.0 + mod_ref[0, pl.ds(mi + 1, 1), :]))
        vec_ref[1] = bcast(mod_ref[0, pl.ds(mi, 1), :])
        vec_ref[2] = bcast(HALF * mod_ref[0, pl.ds(mi + 2, 1), :])
        if final_norm:
            vec_ref[3] = bcast(fg_ref[...])
        if next_pre is not None:
            gi2, mi2 = next_pre
            vec_ref[3] = bcast(g_ref[pl.ds(gi2, 1), :] * (1.0 + mod_ref[0, pl.ds(mi2 + 1, 1), :]))
            vec_ref[4] = bcast(mod_ref[0, pl.ds(mi2, 1), :])

    @pl.when(f < nstage)
    def _():
        base = pl.multiple_of(f * rc, rc)

        def stats(r0):
            r_ref[pl.ds(r0, nr), :] = inv_rms(x_ref[0, pl.ds(r0, nr), :])

        def apply(r0):
            pre = x_ref[0, pl.ds(r0, nr), :] * lanes(r_ref[pl.ds(r0, nr), :]) * vec_ref[0] + vec_ref[1]
            pre_ref[pl.ds(base + r0, nr), :] = pre.astype(BF16)
            acc_ref[pl.ds(base + r0, nr), :] = jnp.zeros((nr, d), F32)

        _for_rows(rc, nr, stats)
        _for_rows(rc, nr, apply)

    @pl.when((f >= nstage) & (f < nstage + nf))
    def _():
        p = pre_ref[...]
        gate = _dot(p, wg_ref[...])
        up = _dot(p, wu_ref[...])
        act = (gate * jax.nn.sigmoid(gate) * up).astype(BF16)
        acc_ref[...] += _dot(act, wo_ref[...])
        step = (pl.program_id(0) * pl.num_programs(1) + pl.program_id(1)) * nf + (f - nstage)
        for src, dst, nblk in zip(src_refs, dst_refs, cast_blocks):
            @pl.when(step < nblk)
            def _(src=src, dst=dst):
                dst[...] = src[...].astype(BF16)

    @pl.when(f >= nstage + nf)
    def _():
        base = pl.multiple_of((f - (nstage + nf)) * rc, rc)

        renorm = final_norm or next_pre is not None

        def residual(r0):
            r = pl.ds(r0, nr)
            y = x_ref[0, r, :] + vec_ref[2] * acc_ref[pl.ds(base + r0, nr), :]
            o_ref[0, r, :] = y
            if renorm:
                r_ref[r, :] = inv_rms(y)

        def norm(r0):
            r = pl.ds(r0, nr)
            yn = o_ref[0, r, :] * lanes(r_ref[r, :])
            if final_norm:
                o_ref[0, r, :] = yn * vec_ref[3]
            if next_pre is not None:
                u_ref[0, r, :] = (yn * vec_ref[3] + vec_ref[4]).astype(BF16)

        _for_rows(rc, nr, residual)
        if renorm:
            _for_rows(rc, nr, norm)


def _cast_plan(arr, nsteps):
    r, c = arr.shape
    for parts in (8, 4, 2, 1):
        if c % (parts * LANES):
            continue
        br = BF16_ROWS
        while br <= r:
            if r % br == 0 and (r // br) * parts <= nsteps:
                return br, c // parts
            br *= 2
    return None


def _ffn(x, norm_g, mods, mod_row, w_in, w_out, final_g, *, gi, mi, tm, tf, final_norm=False, next_pre=None,
         casts=(), stage_rows=FFN_STAGE_ROWS):
    bsz, s, d = x.shape
    ff = w_out.shape[0]
    nf = ff // tf
    tm = _tile(s, tm)
    rc = _tile(tm, stage_rows)
    nstage = tm // rc
    ni = s // tm

    def hid(f):
        return jnp.clip(f - nstage, 0, nf - 1)

    cast_blocks, cast_in, cast_out, cast_shapes = [], [], [], []
    for arr, (br, bc) in casts:
        nrb, ncb = arr.shape[0] // br, arr.shape[1] // bc
        assert nrb * br == arr.shape[0] and ncb * bc == arr.shape[1] and nrb * ncb <= bsz * ni * nf

        def blk(b, i, f, nrb=nrb, ncb=ncb):
            step = jnp.minimum((b * ni + i) * nf + hid(f), nrb * ncb - 1)
            return step // ncb, step % ncb

        cast_blocks.append(nrb * ncb)
        cast_in.append(pl.BlockSpec((br, bc), blk))
        cast_out.append(pl.BlockSpec((br, bc), blk))
        cast_shapes.append(jax.ShapeDtypeStruct(arr.shape, BF16))
    kern = functools.partial(_ffn_kernel, gi=gi, mi=mi, nf=nf, nstage=nstage, final_norm=final_norm,
                             next_pre=next_pre, cast_blocks=tuple(cast_blocks))

    def x_blk(b, i, f):
        return b, i * nstage + jnp.where(f < nstage, f, jnp.clip(f - (nstage + nf), 0, nstage - 1)), 0

    def o_blk(b, i, f):
        return b, i * nstage + jnp.clip(f - (nstage + nf), 0, nstage - 1), 0

    out_shape = [jax.ShapeDtypeStruct((bsz, s, d), F32)]
    out_specs = [pl.BlockSpec((1, rc, d), o_blk)]
    if next_pre is not None:
        out_shape.append(jax.ShapeDtypeStruct((bsz, s, d), BF16))
        out_specs.append(pl.BlockSpec((1, rc, d), o_blk))
    return pl.pallas_call(
        kern,
        out_shape=out_shape + cast_shapes,
        grid=(bsz, ni, nf + 2 * nstage),
        in_specs=[pl.BlockSpec((1, rc, d), x_blk),
                  pl.BlockSpec(norm_g.shape, lambda b, i, f: (0, 0)),
                  pl.BlockSpec((1, N_MOD, d), lambda b, i, f: (mod_row(b), 0, 0)),
                  pl.BlockSpec((d, tf), lambda b, i, f: (0, hid(f))),
                  pl.BlockSpec((d, tf), lambda b, i, f: (0, nf + hid(f))),
                  pl.BlockSpec((tf, d), lambda b, i, f: (hid(f), 0)),
                  pl.BlockSpec((1, d), lambda b, i, f: (0, 0))] + cast_in,
        out_specs=out_specs + cast_out,
        scratch_shapes=[pltpu.VMEM((tm, d), BF16), pltpu.VMEM((tm, d), F32),
                        pltpu.VMEM((5, NORM_ROWS, d), F32), pltpu.VMEM((rc, LANES), F32)],
        compiler_params=_cparams(("arbitrary", "arbitrary", "arbitrary")),
        name="ffn",
    )(x, norm_g, mods, w_in, w_in, w_out, final_g, *[arr for arr, _ in casts])


def _mm_kernel(a_ref, b_ref, o_ref):
    o_ref[0] = _dot(a_ref[0], b_ref[...]).astype(o_ref.dtype)


def _mm(a, w, out_dtype, *, tm, tn, col0=0, ncols=None):
    bsz, s, k = a.shape
    n = w.shape[1] - col0 if ncols is None else ncols
    tm, tn = _tile(s, tm), _tile(math.gcd(n, col0) if col0 else n, tn)
    j0 = col0 // tn
    return pl.pallas_call(
        _mm_kernel,
        out_shape=jax.ShapeDtypeStruct((bsz, s, n), out_dtype),
        grid=(bsz, s // tm, n // tn),
        in_specs=[pl.BlockSpec((1, tm, k), lambda b, i, j: (b, i, 0)),
                  pl.BlockSpec((k, tn), lambda b, i, j: (0, j0 + j))],
        out_specs=pl.BlockSpec((1, tm, tn), lambda b, i, j: (b, i, j)),
        compiler_params=_cparams(("arbitrary", "arbitrary", "arbitrary")),
        name="mm",
    )(a, w)


def _mixout_kernel(a_ref, p_ref, wa_ref, wb_ref, ga_ref, gb_ref, wo_ref, x_ref, mod_ref, o_ref, m_ref, *, mi, nj):
    j = pl.program_id(2)
    tn = wa_ref.shape[1]

    @pl.when(j < nj)
    def _():
        ya = _dot(a_ref[0], wa_ref[...])
        yb = _dot(p_ref[0], wb_ref[...])
        m = jax.nn.sigmoid(ga_ref[0].astype(F32)) * ya + jax.nn.sigmoid(gb_ref[0].astype(F32)) * yb
        m_ref[:, pl.ds(pl.multiple_of(j * tn, tn), tn)] = m.astype(BF16)

    @pl.when(j >= nj)
    def _():
        o_ref[0] = x_ref[0] + mod_ref[0, pl.ds(mi, 1), :] * _dot(m_ref[...], wo_ref[...])


def _mixout(ag, yp, wa, wb, gates, wo, x, mods, mod_row, *, mi, tm, tn):
    bsz, s, ka = ag.shape
    kb = yp.shape[2]
    d = wa.shape[1]
    tm, tn = _tile(s, tm), _tile(d, tn)
    nj = d // tn

    def first(j):
        return jnp.minimum(j, nj - 1)

    def second(j):
        return jnp.maximum(j - nj, 0)

    return pl.pallas_call(
        functools.partial(_mixout_kernel, mi=mi, nj=nj),
        out_shape=jax.ShapeDtypeStruct((bsz, s, d), F32),
        grid=(bsz, s // tm, 2 * nj),
        in_specs=[pl.BlockSpec((1, tm, ka), lambda b, i, j: (b, i, 0)),
                  pl.BlockSpec((1, tm, kb), lambda b, i, j: (b, i, 0)),
                  pl.BlockSpec((ka, tn), lambda b, i, j: (0, first(j))),
                  pl.BlockSpec((kb, tn), lambda b, i, j: (0, first(j))),
                  pl.BlockSpec((1, tm, tn), lambda b, i, j: (b, i, first(j))),
                  pl.BlockSpec((1, tm, tn), lambda b, i, j: (b, i, nj + first(j))),
                  pl.BlockSpec((d, tn), lambda b, i, j: (0, second(j))),
                  pl.BlockSpec((1, tm, tn), lambda b, i, j: (b, i, second(j))),
                  pl.BlockSpec((1, N_MOD, tn), lambda b, i, j: (mod_row(b), 0, second(j)))],
        out_specs=pl.BlockSpec((1, tm, tn), lambda b, i, j: (b, i, second(j))),
        scratch_shapes=[pltpu.VMEM((tm, d), BF16)],
        compiler_params=_cparams(("arbitrary", "arbitrary", "arbitrary")),
        name="mixout",
    )(ag, yp, wa, wb, gates, gates, wo, x, mods)


def _ssm_tables(lam_re, lam_im, log_step, b_re, b_im, c_re, c_im):
    t = SSM_CHUNK
    hp = lax.Precision.HIGHEST
    lr = jnp.minimum(lam_re.astype(F32), LAMBDA_RE_MAX)
    li = lam_im.astype(F32)
    step = jnp.exp(log_step.astype(F32))[..., None]
    m = jnp.arange(t + 1, dtype=F32)[:, None, None, None]
    mag = jnp.exp(m * (lr * step)[None])
    ang = m * (li * step)[None]
    pw_re, pw_im = mag * jnp.cos(ang), mag * jnp.sin(ang)
    nr, ni = pw_re[1] - 1.0, pw_im[1]
    den = lr * lr + li * li
    q_re, q_im = (nr * lr + ni * li) / den, (ni * lr - nr * li) / den
    bb_re = q_re[..., None] * b_re - q_im[..., None] * b_im
    bb_im = q_re[..., None] * b_im + q_im[..., None] * b_re
    cr, ci = c_re.astype(F32), c_im.astype(F32)

    cl_re = cr[:, :, None] * pw_re[:t].transpose(1, 2, 0, 3)[:, :, :, None, :] \
        - ci[:, :, None] * pw_im[:t].transpose(1, 2, 0, 3)[:, :, :, None, :]
    cl_im = cr[:, :, None] * pw_im[:t].transpose(1, 2, 0, 3)[:, :, :, None, :] \
        + ci[:, :, None] * pw_re[:t].transpose(1, 2, 0, 3)[:, :, :, None, :]
    kern = jnp.einsum('dgtkp,dgpq->dgtkq', cl_re, bb_re, precision=hp) \
        - jnp.einsum('dgtkp,dgpq->dgtkq', cl_im, bb_im, precision=hp)
    g, kk = kern.shape[1], kern.shape[3]
    zero = (jnp.arange(t) == 0).astype(F32)[None, :, None, None]
    kpos = kern[0] + zero * kern[1]
    kneg = kern[1] + zero * kern[0]
    rpos = kpos.transpose(0, 3, 1, 2).reshape(g, kk, t * kk)
    rneg = kneg.transpose(0, 3, 1, 2).reshape(g, kk, t * kk)

    pf_re, pf_im = pw_re[:t][::-1, 0], pw_im[:t][::-1, 0]
    pb_re, pb_im = pw_re[:t, 1], pw_im[:t, 1]

    def st(p_re, p_im, d):
        re = p_re[:, :, :, None] * bb_re[d][None] - p_im[:, :, :, None] * bb_im[d][None]
        im = p_re[:, :, :, None] * bb_im[d][None] + p_im[:, :, :, None] * bb_re[d][None]
        return re.transpose(1, 0, 3, 2), im.transpose(1, 0, 3, 2)

    wf_re, wf_im = st(pf_re, pf_im, 0)
    wb_re, wb_im = st(pb_re, pb_im, 1)
    p = lr.shape[-1]
    wst = jnp.concatenate([wf_re, wb_re, wf_im, wb_im], axis=-1).reshape(g, t * kk, 4 * p)

    vf_pw_re, vf_pw_im = pw_re[1:, 0], pw_im[1:, 0]
    vb_pw_re, vb_pw_im = pw_re[1:, 1][::-1], pw_im[1:, 1][::-1]

    def rd(p_re, p_im, d):
        d_re = cr[d][None] * p_re[:, :, None, :] - ci[d][None] * p_im[:, :, None, :]
        d_im = cr[d][None] * p_im[:, :, None, :] + ci[d][None] * p_re[:, :, None, :]
        return d_re.transpose(1, 3, 0, 2), -d_im.transpose(1, 3, 0, 2)

    vf_re, vf_im = rd(vf_pw_re, vf_pw_im, 0)
    vb_re, vb_im = rd(vb_pw_re, vb_pw_im, 1)
    vrd = jnp.concatenate([vf_re, vb_re, vf_im, vb_im], axis=1).reshape(g, 4 * p, t * kk)

    a_re = jnp.concatenate([pw_re[t, 0], pw_re[t, 1]], axis=-1)[:, None, :]
    a_im = jnp.concatenate([pw_im[t, 0], pw_im[t, 1]], axis=-1)[:, None, :]
    return rpos.astype(BF16), rneg.astype(BF16), wst.astype(BF16), vrd.astype(BF16), a_re, a_im


SSM_GPB = LANES // SSM_GROUP
EXPAND_ROWS = 4


def _ssm_block_tables(rpos, rneg, wst, vrd, a_re, a_im):
    rpos, rneg, wst, vrd = lax.optimization_barrier((rpos, rneg, wst, vrd))
    g, _, tk = rpos.shape
    t, k, gpb = SSM_CHUNK, SSM_GROUP, SSM_GPB
    nb = g // gpb
    p2 = wst.shape[2] // 2
    tl, hw = t * gpb * k, gpb * p2
    rpos = rpos.reshape(nb, gpb * k, tk)
    rneg = rneg.reshape(nb, gpb * k, tk)
    rw = wst.reshape(nb, gpb, t, k, 2 * p2).transpose(0, 2, 1, 3, 4).reshape(nb, t, gpb * k, 2 * p2)
    rv = vrd.reshape(nb, gpb, 2, p2, tk).transpose(0, 2, 1, 3, 4).reshape(nb, 2, hw, tk)
    jj, kk = np.arange(tk) // k, np.arange(tk) % k
    sel = (jj[None, :, None] == np.arange(t)[:, None, None]) & (kk[None, :, None] == (np.arange(LANES) % k)[None, None, :])
    sel = jnp.asarray(sel, BF16)
    toep_b, wst_b, vrd_b = pl.pallas_call(
        functools.partial(_ssm_expand_kernel, t=t, k=k, p2=p2),
        out_shape=[jax.ShapeDtypeStruct((nb, tl, tl), BF16), jax.ShapeDtypeStruct((nb, tl, 2 * hw), BF16),
                   jax.ShapeDtypeStruct((nb, 2 * hw, tl), BF16)],
        grid=(nb, t // EXPAND_ROWS),
        in_specs=[pl.BlockSpec((1, gpb * k, tk), lambda b, i: (b, 0, 0)),
                  pl.BlockSpec((1, gpb * k, tk), lambda b, i: (b, 0, 0)),
                  pl.BlockSpec((1, EXPAND_ROWS, gpb * k, 2 * p2), lambda b, i: (b, i, 0, 0)),
                  pl.BlockSpec((1, 2, hw, tk), lambda b, i: (b, 0, 0, 0)),
                  pl.BlockSpec(sel.shape, lambda b, i: (0, 0, 0))],
        out_specs=[pl.BlockSpec((1, EXPAND_ROWS * gpb * k, tl), lambda b, i: (b, i, 0)),
                   pl.BlockSpec((1, EXPAND_ROWS * gpb * k, 2 * hw), lambda b, i: (b, i, 0)),
                   pl.BlockSpec((1, 2 * hw, EXPAND_ROWS * gpb * k), lambda b, i: (b, 0, i))],
        scratch_shapes=[pltpu.VMEM((2 * t - 1, gpb * k, gpb * k), BF16)],
        compiler_params=_cparams(("arbitrary", "arbitrary")),
        name="ssm_expand",
    )(rpos, rneg, rw, rv, sel)
    return toep_b, wst_b, vrd_b, a_re.reshape(nb, 1, hw), a_im.reshape(nb, 1, hw)


def _ssm_expand_kernel(rpos_ref, rneg_ref, rw_ref, rv_ref, sel_ref, toep_ref, wst_ref, vrd_ref, bd_ref, *, t, k, p2):
    q = pl.program_id(1)
    lb = rpos_ref.shape[1]
    hw = rv_ref.shape[2]
    gpb = lb // k

    @pl.when(q == 0)
    def _():
        same_g = (lax.broadcasted_iota(jnp.int32, (lb, lb), 0) // k) == (lax.broadcasted_iota(jnp.int32, (lb, lb), 1) // k)
        for m in range(t):
            bd_ref[t - 1 + m] = jnp.where(same_g, _dot(rpos_ref[0], sel_ref[m]), 0.0).astype(BF16)
            if m:
                bd_ref[t - 1 - m] = jnp.where(same_g, _dot(rneg_ref[0], sel_ref[m]), 0.0).astype(BF16)

    same_gw = (lax.broadcasted_iota(jnp.int32, (lb, hw), 0) // k) == (lax.broadcasted_iota(jnp.int32, (lb, hw), 1) // p2)
    same_gv = (lax.broadcasted_iota(jnp.int32, (hw, lb), 0) // p2) == (lax.broadcasted_iota(jnp.int32, (hw, lb), 1) // k)
    for e in range(EXPAND_ROWS):
        i = q * EXPAND_ROWS + e
        rows = slice(e * lb, (e + 1) * lb)
        for j in range(t):
            toep_ref[0, rows, j * lb:(j + 1) * lb] = bd_ref[j - i + (t - 1)]
        w = rw_ref[0, e]
        for r in range(2):
            wr = jnp.concatenate([w[:, r * p2:(r + 1) * p2]] * gpb, axis=-1)
            wst_ref[0, rows, r * hw:(r + 1) * hw] = jnp.where(same_gw, wr, jnp.zeros_like(wr))
        for r in range(2):
            v = _dot(rv_ref[0, r], sel_ref[i])
            vrd_ref[0, r * hw:(r + 1) * hw, rows] = jnp.where(same_gv, v, 0.0).astype(BF16)


def _ssm_kernel(u_ref, ws_ref, ar_ref, ai_ref, h0_ref, *refs, nc, want_y):
    t = SSM_CHUNK
    if want_y:
        wt_ref, v_ref, d_ref, z_ref, hfin_ref, s_ref, yi_ref, hp_ref, tmp_ref, y_ref = refs
    else:
        hfin_ref, s_ref = refs
    a = jnp.concatenate([u_ref[0, pl.ds(j, nc, stride=t), :] for j in range(t)], axis=-1).astype(BF16)
    s = _dot(a, ws_ref[0])
    nlb = s.shape[1] // LANES
    gpb = nlb // 2
    for k in range(nlb):
        s_ref[pl.ds(k, nc, stride=nlb), :] = s[:, k * LANES:(k + 1) * LANES]
    if want_y:
        yi_ref[...] = _dot(a, wt_ref[0])

    ar = ar_ref[0]
    ai = ai_ref[0]
    fwd = lax.broadcasted_iota(jnp.int32, (gpb, LANES), 1) < (LANES // 2)

    def body(c, carry):
        h_re, h_im = carry
        rf = pl.multiple_of(c * nlb, nlb)
        rb = pl.multiple_of((nc - 1 - c) * nlb, nlb)
        if want_y:
            hp_ref[pl.ds(rf, gpb), :] = h_re
            hp_ref[pl.ds(rf + gpb, gpb), :] = h_im
            tmp_ref[pl.ds(rb, gpb), :] = h_re
            tmp_ref[pl.ds(rb + gpb, gpb), :] = h_im
        s_re = jnp.where(fwd, s_ref[pl.ds(rf, gpb), :], s_ref[pl.ds(rb, gpb), :])
        s_im = jnp.where(fwd, s_ref[pl.ds(rf + gpb, gpb), :], s_ref[pl.ds(rb + gpb, gpb), :])
        return ar * h_re - ai * h_im + s_re, ar * h_im + ai * h_re + s_im

    h0 = h0_ref[0, 0]
    h_re, h_im = lax.fori_loop(0, nc, body, (h0[:gpb], h0[gpb:]))
    hfin_ref[0, 0] = jnp.concatenate([h_re, h_im], axis=0)

    if want_y:
        fwd2 = lax.broadcasted_iota(jnp.int32, hp_ref.shape, 1) < (LANES // 2)
        hp_ref[...] = jnp.where(fwd2, hp_ref[...], tmp_ref[...])
        hp = jnp.concatenate([hp_ref[pl.ds(k, nc, stride=nlb), :] for k in range(nlb)], axis=-1).astype(BF16)
        y = yi_ref[...] + _dot(hp, v_ref[0])
        for j in range(t):
            y_ref[pl.ds(j, nc, stride=t), :] = y[:, j * LANES:(j + 1) * LANES]

        def rows(r0):
            r = pl.ds(r0, ROW_CHUNK)
            z_ref[0, r, :] = jax.nn.gelu(y_ref[r, :] + d_ref[...] * u_ref[0, r, :]).astype(BF16)

        _for_row_chunks(nc * t, rows)


def _ssm(u, tables, h0, d, *, want_y):
    toep_b, wst_b, vrd_b, ar, ai = tables
    bsz, n, _ = u.shape
    nb, tl, sw = wst_b.shape
    nc = n // SSM_CHUNK
    single = pl.Buffered(1)
    nlb = sw // LANES
    gpb = nlb // 2
    shapes = [jax.ShapeDtypeStruct((bsz, nb, nlb, LANES), F32)]
    specs = [pl.BlockSpec((1, 1, nlb, LANES), lambda k, b: (b, k, 0, 0))]
    scratch = [pltpu.VMEM((nc * nlb, LANES), F32)]
    in_specs = [pl.BlockSpec((1, n, LANES), lambda k, b: (b, 0, k)),
                pl.BlockSpec((1, tl, sw), lambda k, b: (k, 0, 0)),
                pl.BlockSpec((1, gpb, LANES), lambda k, b: (k, 0, 0)),
                pl.BlockSpec((1, gpb, LANES), lambda k, b: (k, 0, 0)),
                pl.BlockSpec((1, 1, nlb, LANES), lambda k, b: (b, k, 0, 0))]
    args = [u, wst_b, ar.reshape(nb, gpb, LANES), ai.reshape(nb, gpb, LANES), h0]
    if want_y:
        in_specs += [pl.BlockSpec((1, tl, tl), lambda k, b: (k, 0, 0)),
                     pl.BlockSpec((1, sw, tl), lambda k, b: (k, 0, 0), pipeline_mode=single),
                     pl.BlockSpec((1, LANES), lambda k, b: (0, k))]
        args += [toep_b, vrd_b, d]
        shapes.insert(0, jax.ShapeDtypeStruct((bsz, n, nb * LANES), BF16))
        specs.insert(0, pl.BlockSpec((1, n, LANES), lambda k, b: (b, 0, k)))
        scratch += [pltpu.VMEM((nc, tl), F32), pltpu.VMEM((nc * nlb, LANES), F32),
                    pltpu.VMEM((nc * nlb, LANES), F32), pltpu.VMEM((n, LANES), F32)]
    return pl.pallas_call(
        functools.partial(_ssm_kernel, nc=nc, want_y=want_y),
        out_shape=shapes,
        grid=(nb, bsz),
        in_specs=in_specs,
        out_specs=specs,
        scratch_shapes=scratch,
        compiler_params=_cparams(("arbitrary", "arbitrary")),
        name="ssm",
    )(*args)


def _glu_kernel(z_ref, w_ref, o_ref, *, ws):
    z = z_ref[0]
    a = _dot(z, w_ref[:, :ws])
    b = _dot(z, w_ref[:, ws:])
    o_ref[0] = (a * jax.nn.sigmoid(b)).astype(BF16)


def _glu(z, w_glu, *, tm):
    bsz, s, ws = z.shape
    tm = _tile(s, tm)
    return pl.pallas_call(
        functools.partial(_glu_kernel, ws=ws),
        out_shape=jax.ShapeDtypeStruct((bsz, s, ws), BF16),
        grid=(bsz, s // tm),
        in_specs=[pl.BlockSpec((1, tm, ws), lambda b, i: (b, i, 0)),
                  pl.BlockSpec(w_glu.shape, lambda b, i: (0, 0))],
        out_specs=pl.BlockSpec((1, tm, ws), lambda b, i: (b, i, 0)),
        compiler_params=_cparams(("arbitrary", "arbitrary")),
        name="glu",
    )(z, w_glu)


POOL_TILE = 4 * GRID_W
POOL_UNROLL = 4


def _pool_consts(n):
    t = np.arange(POOL_TILE)
    pcs, invs = [], []
    r = np.arange(n) // GRID_W
    c = np.arange(n) % GRID_W
    rows = n // GRID_W
    for w in POOL_WINDOWS:
        lo, hi = w // 2, w - w // 2
        same_row = (t[:, None] // GRID_W) == (t[None, :] // GRID_W)
        dc = (t[None, :] % GRID_W) - (t[:, None] % GRID_W)
        pcs.append((same_row & (dc >= -lo) & (dc < hi)).astype(np.float32))
        cnt_r = np.minimum(r + hi, rows) - np.maximum(r - lo, 0)
        cnt_c = np.minimum(c + hi, GRID_W) - np.maximum(c - lo, 0)
        invs.append((1.0 / (cnt_r * cnt_c)).astype(np.float32)[:, None])
    return jnp.asarray(np.stack(pcs), BF16), jnp.asarray(np.stack(invs), F32)


def _pool_kernel(v_ref, pc_ref, inv_ref, pw_ref, sc_ref, o_ref, cs_ref, *, n, pad):
    wi = pl.program_id(0)
    nt = n // POOL_TILE
    ch = v_ref.shape[2]
    zeros = jnp.zeros((pad, ch), F32)
    cs_ref[pl.ds(0, pad), :] = zeros
    cs_ref[pl.ds(pad + n, pad), :] = zeros
    pc = pc_ref[0]

    def col_body(i, _):
        off = pl.multiple_of(i * POOL_TILE, POOL_TILE)
        v = v_ref[0, pl.ds(off, POOL_TILE), :]
        hi = v.astype(BF16)
        lo = (v - hi.astype(F32)).astype(BF16)
        cs_ref[pl.ds(pad + off, POOL_TILE), :] = _dot(pc, hi) + _dot(pc, lo)
        return 0

    lax.fori_loop(0, nt, col_body, 0, unroll=POOL_UNROLL)

    for k, w in enumerate(POOL_WINDOWS):
        @pl.when(wi == k)
        def _(w=w):
            def row_body(i, _):
                off = pl.multiple_of(i * POOL_TILE, POOL_TILE)
                acc = cs_ref[pl.ds(pad + off - (w // 2) * GRID_W, POOL_TILE), :]
                for dlt in range(-(w // 2) + 1, w - w // 2):
                    acc = acc + cs_ref[pl.ds(pad + off + dlt * GRID_W, POOL_TILE), :]
                mixed = acc * inv_ref[0, pl.ds(off, POOL_TILE), :] - v_ref[0, pl.ds(off, POOL_TILE), :]
                y = _dot(mixed.astype(BF16), pw_ref[0]) * sc_ref[0]
                o_ref[0, pl.ds(off, POOL_TILE), :] = y.astype(BF16)
                return 0

            lax.fori_loop(0, nt, row_body, 0, unroll=POOL_UNROLL)


def _pool(usp, col0, pool_w, pool_scale, pc, inv):
    bsz, n, _ = usp.shape
    nw, ch, _ = pool_w.shape
    pad = (max(POOL_WINDOWS) // 2) * GRID_W
    cb0 = col0 // ch
    return pl.pallas_call(
        functools.partial(_pool_kernel, n=n, pad=pad),
        out_shape=jax.ShapeDtypeStruct((bsz, n, nw * ch), BF16),
        grid=(nw, bsz),
        in_specs=[pl.BlockSpec((1, n, ch), lambda w, b: (b, 0, cb0 + w)),
                  pl.BlockSpec((1, POOL_TILE, POOL_TILE), lambda w, b: (w, 0, 0)),
                  pl.BlockSpec((1, n, 1), lambda w, b: (w, 0, 0)),
                  pl.BlockSpec((1, ch, ch), lambda w, b: (w, 0, 0)),
                  pl.BlockSpec((1, 1, ch), lambda w, b: (w, 0, 0))],
        out_specs=pl.BlockSpec((1, n, ch), lambda w, b: (b, 0, w)),
        scratch_shapes=[pltpu.VMEM((n + 2 * pad, ch), F32)],
        compiler_params=_cparams(("arbitrary", "arbitrary")),
        name="pool",
    )(usp, pc, inv, pool_w, pool_scale)


def kernel(x, c, ctx, c_ctx, w_mod, b_mod, norm_g, final_g, ffn1_w_in, ffn1_w_out, ffn2_w_in, ffn2_w_out,
           w_in, ssm_lambda_re, ssm_lambda_im, ssm_log_step, ssm_b_re, ssm_b_im, ssm_c_re, ssm_c_im, ssm_d,
           w_glu, w_branch_a, pool_w, pool_scale, w_branch_b, w_out):
    bsz, seq, d = x.shape
    assert w_mod.shape[0] == 1, "single-layer problem"
    ssm_w = ssm_d.shape[1]
    pool_width = pool_scale.shape[1]
    nw = len(POOL_WINDOWS)
    pch = pool_width // nw
    p = ssm_lambda_re.shape[-1]
    assert seq % POOL_TILE == 0 and ctx.shape[1] % SSM_CHUNK == 0 and ssm_w % LANES == 0

    f1_in, f1_out = ffn1_w_in[0].astype(BF16), ffn1_w_out[0].astype(BF16)
    w_s = w_in[0][:, :ssm_w].astype(BF16)
    late = [ffn2_w_in[0], ffn2_w_out[0], w_in[0], w_out[0]]
    wbb, wba, wglu, pw = (w_branch_b[0].astype(BF16), w_branch_a[0].astype(BF16), w_glu[0].astype(BF16),
                          pool_w[0].astype(BF16))
    psc = pool_scale[0].reshape(nw, 1, pch)
    ng = norm_g[0]
    fg = final_g.reshape(1, d)

    rows = -(-(bsz + 1) // 8) * 8
    c8 = jnp.zeros((rows, d), F32).at[:bsz].set(c).at[bsz].set(c_ctx)
    mods = _mod(c8, w_mod[0], b_mod, tn=_tile(w_mod.shape[2], MOD_COL_TILE)).reshape(rows, N_MOD, d)
    lat = lambda b: b
    con = lambda b: bsz
    ffn_tiles = dict(tm=FFN_ROW_TILE, tf=FFN_HIDDEN_TILE)
    mm_tiles = dict(tm=MM_ROW_TILE, tn=MM_COL_TILE)

    tables = _ssm_block_tables(*_ssm_tables(ssm_lambda_re[0], ssm_lambda_im[0], ssm_log_step[0],
                                            ssm_b_re[0], ssm_b_im[0], ssm_c_re[0], ssm_c_im[0]))

    nctx = ctx.shape[1]
    _, uc = _ffn(ctx.reshape(1, bsz * nctx, d), ng, mods, con, f1_in, f1_out, fg, gi=0, mi=0, next_pre=(1, 3),
                 **ffn_tiles)
    us_c = _mm(uc, w_s, F32, **mm_tiles).reshape(bsz, nctx, ssm_w)
    assert 2 * p == LANES, "fwd | bwd states of one group fill one lane tile"
    h_ctx, = _ssm(us_c, tables, jnp.zeros((bsz, ssm_w // LANES, 2 * SSM_GPB, LANES), F32), ssm_d, want_y=False)

    nsteps = bsz * (seq // _tile(seq, FFN_ROW_TILE)) * (f1_out.shape[0] // FFN_HIDDEN_TILE)
    plans = [_cast_plan(a, nsteps) for a in late]
    x1, u, *cast = _ffn(x, ng, mods, lat, f1_in, f1_out, fg, gi=0, mi=0, next_pre=(1, 3),
                        casts=[(a, pln) for a, pln in zip(late, plans) if pln is not None], **ffn_tiles)
    cast = iter(cast)
    f2_in, f2_out, w_all, wo = [a.astype(BF16) if pln is None else next(cast) for a, pln in zip(late, plans)]
    usp = _mm(u, w_all, F32, ncols=ssm_w + pool_width, **mm_tiles)
    gates = _mm(u, w_all, BF16, col0=ssm_w + pool_width, **mm_tiles)

    z, _ = _ssm(usp, tables, h_ctx, ssm_d, want_y=True)
    ag = _glu(z, wglu, tm=GLU_ROW_TILE)

    pc, inv = _pool_consts(seq)
    yp = _pool(usp, ssm_w, pw, psc, pc, inv)

    x2 = _mixout(ag, yp, wba, wbb, gates, wo, x1, mods, lat, mi=5, tm=MIX_ROW_TILE, tn=MM_COL_TILE)
    out, = _ffn(x2, ng, mods, lat, f2_in, f2_out, fg, gi=2, mi=6, final_norm=True, stage_rows=2 * FFN_STAGE_ROWS,
                **ffn_tiles)
    return out
```

```python
import functools
import math

import jax
import jax.numpy as jnp
import numpy as np
from jax import lax
from jax.experimental import pallas as pl
from jax.experimental.pallas import tpu as pltpu

BF16 = jnp.bfloat16
F32 = jnp.float32

RMS_EPS = 1e-6
LAMBDA_RE_MAX = -1e-4
HALF = 0.5
N_MOD = 9
SSM_GROUP = 16
POOL_WINDOWS = (2, 4, 8, 16)
GRID_W = 64
SSM_CHUNK = 16

LANES = 128
BF16_ROWS = 16
V7X_VMEM_BYTES = 64 * 1024 * 1024
VMEM_LIMIT = V7X_VMEM_BYTES - 4 * 1024 * 1024

FFN_ROW_TILE = 1024
FFN_HIDDEN_TILE = 256
MM_ROW_TILE = 2048
MIX_ROW_TILE = 1024
MM_COL_TILE = 512
GLU_ROW_TILE = 1024
MOD_COL_TILE = 1024


def _cparams(sem):
    return pltpu.CompilerParams(dimension_semantics=sem, vmem_limit_bytes=VMEM_LIMIT)


def _tile(n, pref):
    t = min(n, pref)
    while n % t:
        t //= 2
    return t


ROW_CHUNK = 64


def _for_row_chunks(n, fn):
    def body(i, carry):
        fn(pl.multiple_of(i * ROW_CHUNK, ROW_CHUNK))
        return carry

    lax.fori_loop(0, n // ROW_CHUNK, body, 0)


NORM_ROWS = 16
NORM_UNROLL = 4


def _for_rows(n, step, fn):
    def body(i, carry):
        fn(pl.multiple_of(i * step, step))
        return carry

    lax.fori_loop(0, n // step, body, 0, unroll=NORM_UNROLL)


def _dot(a, b):
    return jnp.dot(a, b, preferred_element_type=F32)


def _mod_kernel(c_ref, w_ref, b_ref, o_ref):
    c = c_ref[...]
    s = (c * jax.nn.sigmoid(c)).astype(BF16)
    o_ref[...] = _dot(s, w_ref[...].astype(BF16)) + b_ref[...]


def _mod(c8, w_mod, b_mod, tn):
    d, n = w_mod.shape
    return pl.pallas_call(
        _mod_kernel,
        out_shape=jax.ShapeDtypeStruct((c8.shape[0], n), F32),
        grid=(n // tn,),
        in_specs=[pl.BlockSpec(c8.shape, lambda j: (0, 0)),
                  pl.BlockSpec((d, tn), lambda j: (0, j)),
                  pl.BlockSpec((1, tn), lambda j: (0, j))],
        out_specs=pl.BlockSpec((c8.shape[0], tn), lambda j: (0, j)),
        compiler_params=_cparams(("arbitrary",)),
        name="mod",
    )(c8, w_mod, b_mod)


FFN_STAGE_ROWS = 128


def _ffn_kernel(x_ref, g_ref, mod_ref, wg_ref, wu_ref, wo_ref, fg_ref, *refs,
                gi, mi, nf, nstage, final_norm, next_pre, cast_blocks):
    nc = len(cast_blocks)
    src_refs, refs = refs[:nc], refs[nc:]
    if next_pre is None:
        o_ref, refs = refs[0], refs[1:]
    else:
        o_ref, u_ref, refs = refs[0], refs[1], refs[2:]
    dst_refs, (pre_ref, acc_ref, vec_ref, r_ref) = refs[:nc], refs[nc:]
    f = pl.program_id(2)
    rc, d = x_ref.shape[1], x_ref.shape[2]
    nr = NORM_ROWS

    def bcast(v):
        return jnp.broadcast_to(v, (nr, d))

    def inv_rms(y):
        ss = jnp.sum(y * y, axis=-1, keepdims=True)
        return jnp.broadcast_to(lax.rsqrt(ss * (1.0 / d) + RMS_EPS), (nr, LANES))

    def lanes(r):
        return jnp.concatenate([r] * (d // LANES), axis=-1)

    @pl.when(f == 0)
    def _():
        vec_ref[0] = bcast(g_ref[pl.ds(gi, 1), :] * (1.0 + mod_ref[0, pl.ds(mi + 1, 1), :]))
        vec_ref[1] = bcast(mod_ref[0, pl.ds(mi, 1), :])
        vec_ref[2] = bcast(HALF * mod_ref[0, pl.ds(mi + 2, 1), :])
        if final_norm:
            vec_ref[3] = bcast(fg_ref[...])
        if next_pre is not None:
            gi2, mi2 = next_pre
            vec_ref[3] = bcast(g_ref[pl.ds(gi2, 1), :] * (1.0 + mod_ref[0, pl.ds(mi2 + 1, 1), :]))
            vec_ref[4] = bcast(mod_ref[0, pl.ds(mi2, 1), :])

    @pl.when(f < nstage)
    def _():
        base = pl.multiple_of(f * rc, rc)

        def stats(r0):
            r_ref[pl.ds(r0, nr), :] = inv_rms(x_ref[0, pl.ds(r0, nr), :])

        def apply(r0):
            pre = x_ref[0, pl.ds(r0, nr), :] * lanes(r_ref[pl.ds(r0, nr), :]) * vec_ref[0] + vec_ref[1]
            pre_ref[pl.ds(base + r0, nr), :] = pre.astype(BF16)
            acc_ref[pl.ds(base + r0, nr), :] = jnp.zeros((nr, d), F32)

        _for_rows(rc, nr, stats)
        _for_rows(rc, nr, apply)

    @pl.when((f >= nstage) & (f < nstage + nf))
    def _():
        p = pre_ref[...]
        gate = _dot(p, wg_ref[...])
        up = _dot(p, wu_ref[...])
        act = (gate * jax.nn.sigmoid(gate) * up).astype(BF16)
        acc_ref[...] += _dot(act, wo_ref[...])
        step = (pl.program_id(0) * pl.num_programs(1) + pl.program_id(1)) * nf + (f - nstage)
        for src, dst, nblk in zip(src_refs, dst_refs, cast_blocks):
            @pl.when(step < nblk)
            def _(src=src, dst=dst):
                dst[...] = src[...].astype(BF16)

    @pl.when(f >= nstage + nf)
    def _():
        base = pl.multiple_of((f - (nstage + nf)) * rc, rc)

        renorm = final_norm or next_pre is not None

        def residual(r0):
            r = pl.ds(r0, nr)
            y = x_ref[0, r, :] + vec_ref[2] * acc_ref[pl.ds(base + r0, nr), :]
            o_ref[0, r, :] = y
            if renorm:
                r_ref[r, :] = inv_rms(y)

        def norm(r0):
            r = pl.ds(r0, nr)
            yn = o_ref[0, r, :] * lanes(r_ref[r, :])
            if final_norm:
                o_ref[0, r, :] = yn * vec_ref[3]
            if next_pre is not None:
                u_ref[0, r, :] = (yn * vec_ref[3] + vec_ref[4]).astype(BF16)

        _for_rows(rc, nr, residual)
        if renorm:
            _for_rows(rc, nr, norm)


def _cast_plan(arr, nsteps):
    r, c = arr.shape
    for parts in (8, 4, 2, 1):
        if c % (parts * LANES):
            continue
        br = BF16_ROWS
        while br <= r:
            if r % br == 0 and (r // br) * parts <= nsteps:
                return br, c // parts
            br *= 2
    return None


def _ffn(x, norm_g, mods, mod_row, w_in, w_out, final_g, *, gi, mi, tm, tf, final_norm=False, next_pre=None,
         casts=(), stage_rows=FFN_STAGE_ROWS):
    bsz, s, d = x.shape
    ff = w_out.shape[0]
    nf = ff // tf
    tm = _tile(s, tm)
    rc = _tile(tm, stage_rows)
    nstage = tm // rc
    ni = s // tm

    def hid(f):
        return jnp.clip(f - nstage, 0, nf - 1)

    cast_blocks, cast_in, cast_out, cast_shapes = [], [], [], []
    for arr, (br, bc) in casts:
        nrb, ncb = arr.shape[0] // br, arr.shape[1] // bc
        assert nrb * br == arr.shape[0] and ncb * bc == arr.shape[1] and nrb * ncb <= bsz * ni * nf

        def blk(b, i, f, nrb=nrb, ncb=ncb):
            step = jnp.minimum((b * ni + i) * nf + hid(f), nrb * ncb - 1)
            return step // ncb, step % ncb

        cast_blocks.append(nrb * ncb)
        cast_in.append(pl.BlockSpec((br, bc), blk))
        cast_out.append(pl.BlockSpec((br, bc), blk))
        cast_shapes.append(jax.ShapeDtypeStruct(arr.shape, BF16))
    kern = functools.partial(_ffn_kernel, gi=gi, mi=mi, nf=nf, nstage=nstage, final_norm=final_norm,
                             next_pre=next_pre, cast_blocks=tuple(cast_blocks))

    def x_blk(b, i, f):
        return b, i * nstage + jnp.where(f < nstage, f, jnp.clip(f - (nstage + nf), 0, nstage - 1)), 0

    def o_blk(b, i, f):
        return b, i * nstage + jnp.clip(f - (nstage + nf), 0, nstage - 1), 0

    out_shape = [jax.ShapeDtypeStruct((bsz, s, d), F32)]
    out_specs = [pl.BlockSpec((1, rc, d), o_blk)]
    if next_pre is not None:
        out_shape.append(jax.ShapeDtypeStruct((bsz, s, d), BF16))
        out_specs.append(pl.BlockSpec((1, rc, d), o_blk))
    return pl.pallas_call(
        kern,
        out_shape=out_shape + cast_shapes,
        grid=(bsz, ni, nf + 2 * nstage),
        in_specs=[pl.BlockSpec((1, rc, d), x_blk),
                  pl.BlockSpec(norm_g.shape, lambda b, i, f: (0, 0)),
                  pl.BlockSpec((1, N_MOD, d), lambda b, i, f: (mod_row(b), 0, 0)),
                  pl.BlockSpec((d, tf), lambda b, i, f: (0, hid(f))),
                  pl.BlockSpec((d, tf), lambda b, i, f: (0, nf + hid(f))),
                  pl.BlockSpec((tf, d), lambda b, i, f: (hid(f), 0)),
                  pl.BlockSpec((1, d), lambda b, i, f: (0, 0))] + cast_in,
        out_specs=out_specs + cast_out,
        scratch_shapes=[pltpu.VMEM((tm, d), BF16), pltpu.VMEM((tm, d), F32),
                        pltpu.VMEM((5, NORM_ROWS, d), F32), pltpu.VMEM((rc, LANES), F32)],
        compiler_params=_cparams(("arbitrary", "arbitrary", "arbitrary")),
        name="ffn",
    )(x, norm_g, mods, w_in, w_in, w_out, final_g, *[arr for arr, _ in casts])


def _mm_kernel(a_ref, b_ref, o_ref):
    o_ref[0] = _dot(a_ref[0], b_ref[...]).astype(o_ref.dtype)


def _mm(a, w, out_dtype, *, tm, tn, col0=0, ncols=None):
    bsz, s, k = a.shape
    n = w.shape[1] - col0 if ncols is None else ncols
    tm, tn = _tile(s, tm), _tile(math.gcd(n, col0) if col0 else n, tn)
    j0 = col0 // tn
    return pl.pallas_call(
        _mm_kernel,
        out_shape=jax.ShapeDtypeStruct((bsz, s, n), out_dtype),
        grid=(bsz, s // tm, n // tn),
        in_specs=[pl.BlockSpec((1, tm, k), lambda b, i, j: (b, i, 0)),
                  pl.BlockSpec((k, tn), lambda b, i, j: (0, j0 + j))],
        out_specs=pl.BlockSpec((1, tm, tn), lambda b, i, j: (b, i, j)),
        compiler_params=_cparams(("arbitrary", "arbitrary", "arbitrary")),
        name="mm",
    )(a, w)


def _mixout_kernel(a_ref, p_ref, wa_ref, wb_ref, ga_ref, gb_ref, wo_ref, x_ref, mod_ref, o_ref, m_ref, *, mi, nj):
    j = pl.program_id(2)
    tn = wa_ref.shape[1]

    @pl.when(j < nj)
    def _():
        ya = _dot(a_ref[0], wa_ref[...])
        yb = _dot(p_ref[0], wb_ref[...])
        m = jax.nn.sigmoid(ga_ref[0].astype(F32)) * ya + jax.nn.sigmoid(gb_ref[0].astype(F32)) * yb
        m_ref[:, pl.ds(pl.multiple_of(j * tn, tn), tn)] = m.astype(BF16)

    @pl.when(j >= nj)
    def _():
        o_ref[0] = x_ref[0] + mod_ref[0, pl.ds(mi, 1), :] * _dot(m_ref[...], wo_ref[...])


def _mixout(ag, yp, wa, wb, gates, wo, x, mods, mod_row, *, mi, tm, tn):
    bsz, s, ka = ag.shape
    kb = yp.shape[2]
    d = wa.shape[1]
    tm, tn = _tile(s, tm), _tile(d, tn)
    nj = d // tn

    def first(j):
        return jnp.minimum(j, nj - 1)

    def second(j):
        return jnp.maximum(j - nj, 0)

    return pl.pallas_call(
        functools.partial(_mixout_kernel, mi=mi, nj=nj),
        out_shape=jax.ShapeDtypeStruct((bsz, s, d), F32),
        grid=(bsz, s // tm, 2 * nj),
        in_specs=[pl.BlockSpec((1, tm, ka), lambda b, i, j: (b, i, 0)),
                  pl.BlockSpec((1, tm, kb), lambda b, i, j: (b, i, 0)),
                  pl.BlockSpec((ka, tn), lambda b, i, j: (0, first(j))),
                  pl.BlockSpec((kb, tn), lambda b, i, j: (0, first(j))),
                  pl.BlockSpec((1, tm, tn), lambda b, i, j: (b, i, first(j))),
                  pl.BlockSpec((1, tm, tn), lambda b, i, j: (b, i, nj + first(j))),
                  pl.BlockSpec((d, tn), lambda b, i, j: (0, second(j))),
                  pl.BlockSpec((1, tm, tn), lambda b, i, j: (b, i, second(j))),
                  pl.BlockSpec((1, N_MOD, tn), lambda b, i, j: (mod_row(b), 0, second(j)))],
        out_specs=pl.BlockSpec((1, tm, tn), lambda b, i, j: (b, i, second(j))),
        scratch_shapes=[pltpu.VMEM((tm, d), BF16)],
        compiler_params=_cparams(("arbitrary", "arbitrary", "arbitrary")),
        name="mixout",
    )(ag, yp, wa, wb, gates, gates, wo, x, mods)


def _ssm_tables(lam_re, lam_im, log_step, b_re, b_im, c_re, c_im):
    t = SSM_CHUNK
    hp = lax.Precision.HIGHEST
    lr = jnp.minimum(lam_re.astype(F32), LAMBDA_RE_MAX)
    li = lam_im.astype(F32)
    step = jnp.exp(log_step.astype(F32))[..., None]
    m = jnp.arange(t + 1, dtype=F32)[:, None, None, None]
    mag = jnp.exp(m * (lr * step)[None])
    ang = m * (li * step)[None]
    pw_re, pw_im = mag * jnp.cos(ang), mag * jnp.sin(ang)
    nr, ni = pw_re[1] - 1.0, pw_im[1]
    den = lr * lr + li * li
    q_re, q_im = (nr * lr + ni * li) / den, (ni * lr - nr * li) / den
    bb_re = q_re[..., None] * b_re - q_im[..., None] * b_im
    bb_im = q_re[..., None] * b_im + q_im[..., None] * b_re
    cr, ci = c_re.astype(F32), c_im.astype(F32)

    cl_re = cr[:, :, None] * pw_re[:t].transpose(1, 2, 0, 3)[:, :, :, None, :] \
        - ci[:, :, None] * pw_im[:t].transpose(1, 2, 0, 3)[:, :, :, None, :]
    cl_im = cr[:, :, None] * pw_im[:t].transpose(1, 2, 0, 3)[:, :, :, None, :] \
        + ci[:, :, None] * pw_re[:t].transpose(1, 2, 0, 3)[:, :, :, None, :]
    kern = jnp.einsum('dgtkp,dgpq->dgtkq', cl_re, bb_re, precision=hp) \
        - jnp.einsum('dgtkp,dgpq->dgtkq', cl_im, bb_im, precision=hp)
    g, kk = kern.shape[1], kern.shape[3]
    zero = (jnp.arange(t) == 0).astype(F32)[None, :, None, None]
    kpos = kern[0] + zero * kern[1]
    kneg = kern[1] + zero * kern[0]
    rpos = kpos.transpose(0, 3, 1, 2).reshape(g, kk, t * kk)
    rneg = kneg.transpose(0, 3, 1, 2).reshape(g, kk, t * kk)

    pf_re, pf_im = pw_re[:t][::-1, 0], pw_im[:t][::-1, 0]
    pb_re, pb_im = pw_re[:t, 1], pw_im[:t, 1]

    def st(p_re, p_im, d):
        re = p_re[:, :, :, None] * bb_re[d][None] - p_im[:, :, :, None] * bb_im[d][None]
        im = p_re[:, :, :, None] * bb_im[d][None] + p_im[:, :, :, None] * bb_re[d][None]
        return re.transpose(1, 0, 3, 2), im.transpose(1, 0, 3, 2)

    wf_re, wf_im = st(pf_re, pf_im, 0)
    wb_re, wb_im = st(pb_re, pb_im, 1)
    p = lr.shape[-1]
    wst = jnp.concatenate([wf_re, wb_re, wf_im, wb_im], axis=-1).reshape(g, t * kk, 4 * p)

    vf_pw_re, vf_pw_im = pw_re[1:, 0], pw_im[1:, 0]
    vb_pw_re, vb_pw_im = pw_re[1:, 1][::-1], pw_im[1:, 1][::-1]

    def rd(p_re, p_im, d):
        d_re = cr[d][None] * p_re[:, :, None, :] - ci[d][None] * p_im[:, :, None, :]
        d_im = cr[d][None] * p_im[:, :, None, :] + ci[d][None] * p_re[:, :, None, :]
        return d_re.transpose(1, 3, 0, 2), -d_im.transpose(1, 3, 0, 2)

    vf_re, vf_im = rd(vf_pw_re, vf_pw_im, 0)
    vb_re, vb_im = rd(vb_pw_re, vb_pw_im, 1)
    vrd = jnp.concatenate([vf_re, vb_re, vf_im, vb_im], axis=1).reshape(g, 4 * p, t * kk)

    a_re = jnp.concatenate([pw_re[t, 0], pw_re[t, 1]], axis=-1)[:, None, :]
    a_im = jnp.concatenate([pw_im[t, 0], pw_im[t, 1]], axis=-1)[:, None, :]
    return rpos.astype(BF16), rneg.astype(BF16), wst.astype(BF16), vrd.astype(BF16), a_re, a_im


SSM_GPB = LANES // SSM_GROUP
EXPAND_ROWS = 8


def _ssm_block_tables(rpos, rneg, wst, vrd, a_re, a_im):
    rpos, rneg, wst, vrd = lax.optimization_barrier((rpos, rneg, wst, vrd))
    g, _, tk = rpos.shape
    t, k, gpb = SSM_CHUNK, SSM_GROUP, SSM_GPB
    nb = g // gpb
    p2 = wst.shape[2] // 2
    tl, hw = t * gpb * k, gpb * p2
    rpos = rpos.reshape(nb, gpb * k, tk)
    rneg = rneg.reshape(nb, gpb * k, tk)
    rw = wst.reshape(nb, gpb, t, k, 2 * p2).transpose(0, 2, 1, 3, 4).reshape(nb, t, gpb * k, 2 * p2)
    rv = vrd.reshape(nb, gpb, 2, p2, tk).transpose(0, 2, 1, 3, 4).reshape(nb, 2, hw, tk)
    jj, kk = np.arange(tk) // k, np.arange(tk) % k
    sel = (jj[None, :, None] == np.arange(t)[:, None, None]) & (kk[None, :, None] == (np.arange(LANES) % k)[None, None, :])
    sel = jnp.asarray(sel, BF16)
    toep_b, wst_b, vrd_b = pl.pallas_call(
        functools.partial(_ssm_expand_kernel, t=t, k=k, p2=p2),
        out_shape=[jax.ShapeDtypeStruct((nb, tl, tl), BF16), jax.ShapeDtypeStruct((nb, tl, 2 * hw), BF16),
                   jax.ShapeDtypeStruct((nb, 2 * hw, tl), BF16)],
        grid=(nb, t // EXPAND_ROWS),
        in_specs=[pl.BlockSpec((1, gpb * k, tk), lambda b, i: (b, 0, 0)),
                  pl.BlockSpec((1, gpb * k, tk), lambda b, i: (b, 0, 0)),
                  pl.BlockSpec((1, EXPAND_ROWS, gpb * k, 2 * p2), lambda b, i: (b, i, 0, 0)),
                  pl.BlockSpec((1, 2, hw, tk), lambda b, i: (b, 0, 0, 0)),
                  pl.BlockSpec(sel.shape, lambda b, i: (0, 0, 0))],
        out_specs=[pl.BlockSpec((1, EXPAND_ROWS * gpb * k, tl), lambda b, i: (b, i, 0)),
                   pl.BlockSpec((1, EXPAND_ROWS * gpb * k, 2 * hw), lambda b, i: (b, i, 0)),
                   pl.BlockSpec((1, 2 * hw, EXPAND_ROWS * gpb * k), lambda b, i: (b, 0, i))],
        scratch_shapes=[pltpu.VMEM((2 * t - 1, gpb * k, gpb * k), BF16)],
        compiler_params=_cparams(("arbitrary", "arbitrary")),
        name="ssm_expand",
    )(rpos, rneg, rw, rv, sel)
    return toep_b, wst_b, vrd_b, a_re.reshape(nb, 1, hw), a_im.reshape(nb, 1, hw)


def _ssm_expand_kernel(rpos_ref, rneg_ref, rw_ref, rv_ref, sel_ref, toep_ref, wst_ref, vrd_ref, bd_ref, *, t, k, p2):
    q = pl.program_id(1)
    lb = rpos_ref.shape[1]
    hw = rv_ref.shape[2]
    gpb = lb // k

    @pl.when(q == 0)
    def _():
        same_g = (lax.broadcasted_iota(jnp.int32, (lb, lb), 0) // k) == (lax.broadcasted_iota(jnp.int32, (lb, lb), 1) // k)
        for m in range(t):
            bd_ref[t - 1 + m] = jnp.where(same_g, _dot(rpos_ref[0], sel_ref[m]), 0.0).astype(BF16)
            if m:
                bd_ref[t - 1 - m] = jnp.where(same_g, _dot(rneg_ref[0], sel_ref[m]), 0.0).astype(BF16)

    same_gw = (lax.broadcasted_iota(jnp.int32, (lb, hw), 0) // k) == (lax.broadcasted_iota(jnp.int32, (lb, hw), 1) // p2)
    same_gv = (lax.broadcasted_iota(jnp.int32, (hw, lb), 0) // p2) == (lax.broadcasted_iota(jnp.int32, (hw, lb), 1) // k)
    for e in range(EXPAND_ROWS):
        i = q * EXPAND_ROWS + e
        rows = slice(e * lb, (e + 1) * lb)
        for j in range(t):
            toep_ref[0, rows, j * lb:(j + 1) * lb] = bd_ref[j - i + (t - 1)]
        w = rw_ref[0, e]
        for r in range(2):
            wr = jnp.concatenate([w[:, r * p2:(r + 1) * p2]] * gpb, axis=-1)
            wst_ref[0, rows, r * hw:(r + 1) * hw] = jnp.where(same_gw, wr, jnp.zeros_like(wr))
        for r in range(2):
            v = _dot(rv_ref[0, r], sel_ref[i])
            vrd_ref[0, r * hw:(r + 1) * hw, rows] = jnp.where(same_gv, v, 0.0).astype(BF16)


def _ssm_kernel(u_ref, ws_ref, ar_ref, ai_ref, h0_ref, *refs, nc, want_y):
    t = SSM_CHUNK
    if want_y:
        wt_ref, v_ref, d_ref, z_ref, hfin_ref, s_ref, yi_ref, hp_ref, tmp_ref, y_ref = refs
    else:
        hfin_ref, s_ref = refs
    a = jnp.concatenate([u_ref[0, pl.ds(j, nc, stride=t), :] for j in range(t)], axis=-1).astype(BF16)
    s = _dot(a, ws_ref[0])
    nlb = s.shape[1] // LANES
    gpb = nlb // 2
    for k in range(nlb):
        s_ref[pl.ds(k, nc, stride=nlb), :] = s[:, k * LANES:(k + 1) * LANES]
    if want_y:
        yi_ref[...] = _dot(a, wt_ref[0])

    ar = ar_ref[0]
    ai = ai_ref[0]
    fwd = lax.broadcasted_iota(jnp.int32, (gpb, LANES), 1) < (LANES // 2)

    def body(c, carry):
        h_re, h_im = carry
        rf = pl.multiple_of(c * nlb, nlb)
        rb = pl.multiple_of((nc - 1 - c) * nlb, nlb)
        if want_y:
            hp_ref[pl.ds(rf, gpb), :] = h_re
            hp_ref[pl.ds(rf + gpb, gpb), :] = h_im
            tmp_ref[pl.ds(rb, gpb), :] = h_re
            tmp_ref[pl.ds(rb + gpb, gpb), :] = h_im
        s_re = jnp.where(fwd, s_ref[pl.ds(rf, gpb), :], s_ref[pl.ds(rb, gpb), :])
        s_im = jnp.where(fwd, s_ref[pl.ds(rf + gpb, gpb), :], s_ref[pl.ds(rb + gpb, gpb), :])
        return ar * h_re - ai * h_im + s_re, ar * h_im + ai * h_re + s_im

    h0 = h0_ref[0, 0]
    h_re, h_im = lax.fori_loop(0, nc, body, (h0[:gpb], h0[gpb:]))
    hfin_ref[0, 0] = jnp.concatenate([h_re, h_im], axis=0)

    if want_y:
        fwd2 = lax.broadcasted_iota(jnp.int32, hp_ref.shape, 1) < (LANES // 2)
        hp_ref[...] = jnp.where(fwd2, hp_ref[...], tmp_ref[...])
        hp = jnp.concatenate([hp_ref[pl.ds(k, nc, stride=nlb), :] for k in range(nlb)], axis=-1).astype(BF16)
        y = yi_ref[...] + _dot(hp, v_ref[0])
        for j in range(t):
            y_ref[pl.ds(j, nc, stride=t), :] = y[:, j * LANES:(j + 1) * LANES]

        def rows(r0):
            r = pl.ds(r0, ROW_CHUNK)
            z_ref[0, r, :] = jax.nn.gelu(y_ref[r, :] + d_ref[...] * u_ref[0, r, :]).astype(BF16)

        _for_row_chunks(nc * t, rows)


def _ssm(u, tables, h0, d, *, want_y):
    toep_b, wst_b, vrd_b, ar, ai = tables
    bsz, n, _ = u.shape
    nb, tl, sw = wst_b.shape
    nc = n // SSM_CHUNK
    single = pl.Buffered(1)
    nlb = sw // LANES
    gpb = nlb // 2
    shapes = [jax.ShapeDtypeStruct((bsz, nb, nlb, LANES), F32)]
    specs = [pl.BlockSpec((1, 1, nlb, LANES), lambda k, b: (b, k, 0, 0))]
    scratch = [pltpu.VMEM((nc * nlb, LANES), F32)]
    in_specs = [pl.BlockSpec((1, n, LANES), lambda k, b: (b, 0, k)),
                pl.BlockSpec((1, tl, sw), lambda k, b: (k, 0, 0)),
                pl.BlockSpec((1, gpb, LANES), lambda k, b: (k, 0, 0)),
                pl.BlockSpec((1, gpb, LANES), lambda k, b: (k, 0, 0)),
                pl.BlockSpec((1, 1, nlb, LANES), lambda k, b: (b, k, 0, 0))]
    args = [u, wst_b, ar.reshape(nb, gpb, LANES), ai.reshape(nb, gpb, LANES), h0]
    if want_y:
        in_specs += [pl.BlockSpec((1, tl, tl), lambda k, b: (k, 0, 0)),
                     pl.BlockSpec((1, sw, tl), lambda k, b: (k, 0, 0), pipeline_mode=single),
                     pl.BlockSpec((1, LANES), lambda k, b: (0, k))]
        args += [toep_b, vrd_b, d]
        shapes.insert(0, jax.ShapeDtypeStruct((bsz, n, nb * LANES), BF16))
        specs.insert(0, pl.BlockSpec((1, n, LANES), lambda k, b: (b, 0, k)))
        scratch += [pltpu.VMEM((nc, tl), F32), pltpu.VMEM((nc * nlb, LANES), F32),
                    pltpu.VMEM((nc * nlb, LANES), F32), pltpu.VMEM((n, LANES), F32)]
    return pl.pallas_call(
        functools.partial(_ssm_kernel, nc=nc, want_y=want_y),
        out_shape=shapes,
        grid=(nb, bsz),
        in_specs=in_specs,
        out_specs=specs,
        scratch_shapes=scratch,
        compiler_params=_cparams(("arbitrary", "arbitrary")),
        name="ssm",
    )(*args)


def _glu_kernel(z_ref, w_ref, o_ref, *, ws):
    z = z_ref[0]
    a = _dot(z, w_ref[:, :ws])
    b = _dot(z, w_ref[:, ws:])
    o_ref[0] = (a * jax.nn.sigmoid(b)).astype(BF16)


def _glu(z, w_glu, *, tm):
    bsz, s, ws = z.shape
    tm = _tile(s, tm)
    return pl.pallas_call(
        functools.partial(_glu_kernel, ws=ws),
        out_shape=jax.ShapeDtypeStruct((bsz, s, ws), BF16),
        grid=(bsz, s // tm),
        in_specs=[pl.BlockSpec((1, tm, ws), lambda b, i: (b, i, 0)),
                  pl.BlockSpec(w_glu.shape, lambda b, i: (0, 0))],
        out_specs=pl.BlockSpec((1, tm, ws), lambda b, i: (b, i, 0)),
        compiler_params=_cparams(("arbitrary", "arbitrary")),
        name="glu",
    )(z, w_glu)


POOL_TILE = 4 * GRID_W
POOL_UNROLL = 4


def _pool_consts(n):
    t = np.arange(POOL_TILE)
    pcs, invs = [], []
    r = np.arange(n) // GRID_W
    c = np.arange(n) % GRID_W
    rows = n // GRID_W
    for w in POOL_WINDOWS:
        lo, hi = w // 2, w - w // 2
        same_row = (t[:, None] // GRID_W) == (t[None, :] // GRID_W)
        dc = (t[None, :] % GRID_W) - (t[:, None] % GRID_W)
        pcs.append((same_row & (dc >= -lo) & (dc < hi)).astype(np.float32))
        cnt_r = np.minimum(r + hi, rows) - np.maximum(r - lo, 0)
        cnt_c = np.minimum(c + hi, GRID_W) - np.maximum(c - lo, 0)
        invs.append((1.0 / (cnt_r * cnt_c)).astype(np.float32)[:, None])
    return jnp.asarray(np.stack(pcs), BF16), jnp.asarray(np.stack(invs), F32)


def _pool_kernel(v_ref, pc_ref, inv_ref, pw_ref, sc_ref, o_ref, cs_ref, *, n, pad):
    wi = pl.program_id(0)
    nt = n // POOL_TILE
    ch = v_ref.shape[2]
    zeros = jnp.zeros((pad, ch), F32)
    cs_ref[pl.ds(0, pad), :] = zeros
    cs_ref[pl.ds(pad + n, pad), :] = zeros
    pc = pc_ref[0]

    def col_body(i, _):
        off = pl.multiple_of(i * POOL_TILE, POOL_TILE)
        v = v_ref[0, pl.ds(off, POOL_TILE), :]
        hi = v.astype(BF16)
        lo = (v - hi.astype(F32)).astype(BF16)
        cs_ref[pl.ds(pad + off, POOL_TILE), :] = _dot(pc, hi) + _dot(pc, lo)
        return 0

    lax.fori_loop(0, nt, col_body, 0, unroll=POOL_UNROLL)

    for k, w in enumerate(POOL_WINDOWS):
        @pl.when(wi == k)
        def _(w=w):
            def row_body(i, _):
                off = pl.multiple_of(i * POOL_TILE, POOL_TILE)
                acc = cs_ref[pl.ds(pad + off - (w // 2) * GRID_W, POOL_TILE), :]
                for dlt in range(-(w // 2) + 1, w - w // 2):
                    acc = acc + cs_ref[pl.ds(pad + off + dlt * GRID_W, POOL_TILE), :]
                mixed = acc * inv_ref[0, pl.ds(off, POOL_TILE), :] - v_ref[0, pl.ds(off, POOL_TILE), :]
                y = _dot(mixed.astype(BF16), pw_ref[0]) * sc_ref[0]
                o_ref[0, pl.ds(off, POOL_TILE), :] = y.astype(BF16)
                return 0

            lax.fori_loop(0, nt, row_body, 0, unroll=POOL_UNROLL)


def _pool(usp, col0, pool_w, pool_scale, pc, inv):
    bsz, n, _ = usp.shape
    nw, ch, _ = pool_w.shape
    pad = (max(POOL_WINDOWS) // 2) * GRID_W
    cb0 = col0 // ch
    return pl.pallas_call(
        functools.partial(_pool_kernel, n=n, pad=pad),
        out_shape=jax.ShapeDtypeStruct((bsz, n, nw * ch), BF16),
        grid=(nw, bsz),
        in_specs=[pl.BlockSpec((1, n, ch), lambda w, b: (b, 0, cb0 + w)),
                  pl.BlockSpec((1, POOL_TILE, POOL_TILE), lambda w, b: (w, 0, 0)),
                  pl.BlockSpec((1, n, 1), lambda w, b: (w, 0, 0)),
                  pl.BlockSpec((1, ch, ch), lambda w, b: (w, 0, 0)),
                  pl.BlockSpec((1, 1, ch), lambda w, b: (w, 0, 0))],
        out_specs=pl.BlockSpec((1, n, ch), lambda w, b: (b, 0, w)),
        scratch_shapes=[pltpu.VMEM((n + 2 * pad, ch), F32)],
        compiler_params=_cparams(("arbitrary", "arbitrary")),
        name="pool",
    )(usp, pc, inv, pool_w, pool_scale)


def kernel(x, c, ctx, c_ctx, w_mod, b_mod, norm_g, final_g, ffn1_w_in, ffn1_w_out, ffn2_w_in, ffn2_w_out,
           w_in, ssm_lambda_re, ssm_lambda_im, ssm_log_step, ssm_b_re, ssm_b_im, ssm_c_re, ssm_c_im, ssm_d,
           w_glu, w_branch_a, pool_w, pool_scale, w_branch_b, w_out):
    bsz, seq, d = x.shape
    assert w_mod.shape[0] == 1, "single-layer problem"
    ssm_w = ssm_d.shape[1]
    pool_width = pool_scale.shape[1]
    nw = len(POOL_WINDOWS)
    pch = pool_width // nw
    p = ssm_lambda_re.shape[-1]
    assert seq % POOL_TILE == 0 and ctx.shape[1] % SSM_CHUNK == 0 and ssm_w % LANES == 0

    f1_in, f1_out = ffn1_w_in[0].astype(BF16), ffn1_w_out[0].astype(BF16)
    w_s = w_in[0][:, :ssm_w].astype(BF16)
    late = [ffn2_w_in[0], ffn2_w_out[0], w_in[0], w_out[0]]
    wbb, wba, wglu, pw = (w_branch_b[0].astype(BF16), w_branch_a[0].astype(BF16), w_glu[0].astype(BF16),
                          pool_w[0].astype(BF16))
    psc = pool_scale[0].reshape(nw, 1, pch)
    ng = norm_g[0]
    fg = final_g.reshape(1, d)

    rows = -(-(bsz + 1) // 8) * 8
    c8 = jnp.zeros((rows, d), F32).at[:bsz].set(c).at[bsz].set(c_ctx)
    mods = _mod(c8, w_mod[0], b_mod, tn=_tile(w_mod.shape[2], MOD_COL_TILE)).reshape(rows, N_MOD, d)
    lat = lambda b: b
    con = lambda b: bsz
    ffn_tiles = dict(tm=FFN_ROW_TILE, tf=FFN_HIDDEN_TILE)
    mm_tiles = dict(tm=MM_ROW_TILE, tn=MM_COL_TILE)

    tables = _ssm_block_tables(*_ssm_tables(ssm_lambda_re[0], ssm_lambda_im[0], ssm_log_step[0],
                                            ssm_b_re[0], ssm_b_im[0], ssm_c_re[0], ssm_c_im[0]))

    nctx = ctx.shape[1]
    _, uc = _ffn(ctx.reshape(1, bsz * nctx, d), ng, mods, con, f1_in, f1_out, fg, gi=0, mi=0, next_pre=(1, 3),
                 **ffn_tiles)
    us_c = _mm(uc, w_s, F32, **mm_tiles).reshape(bsz, nctx, ssm_w)
    assert 2 * p == LANES, "fwd | bwd states of one group fill one lane tile"
    h_ctx, = _ssm(us_c, tables, jnp.zeros((bsz, ssm_w // LANES, 2 * SSM_GPB, LANES), F32), ssm_d, want_y=False)

    nsteps = bsz * (seq // _tile(seq, FFN_ROW_TILE)) * (f1_out.shape[0] // FFN_HIDDEN_TILE)
    plans = [_cast_plan(a, nsteps) for a in late]
    x1, u, *cast = _ffn(x, ng, mods, lat, f1_in, f1_out, fg, gi=0, mi=0, next_pre=(1, 3),
                        casts=[(a, pln) for a, pln in zip(late, plans) if pln is not None], **ffn_tiles)
    cast = iter(cast)
    f2_in, f2_out, w_all, wo = [a.astype(BF16) if pln is None else next(cast) for a, pln in zip(late, plans)]
    usp = _mm(u, w_all, F32, ncols=ssm_w + pool_width, **mm_tiles)
    gates = _mm(u, w_all, BF16, col0=ssm_w + pool_width, **mm_tiles)

    z, _ = _ssm(usp, tables, h_ctx, ssm_d, want_y=True)
    ag = _glu(z, wglu, tm=GLU_ROW_TILE)

    pc, inv = _pool_consts(seq)
    yp = _pool(usp, ssm_w, pw, psc, pc, inv)

    x2 = _mixout(ag, yp, wba, wbb, gates, wo, x1, mods, lat, mi=5, tm=MIX_ROW_TILE, tn=MM_COL_TILE)
    out, = _ffn(x2, ng, mods, lat, f2_in, f2_out, fg, gi=2, mi=6, final_norm=True, stage_rows=2 * FFN_STAGE_ROWS,
                **ffn_tiles)
    return out
```
